```python
import math
import jax, jax.numpy as jnp
from jax import lax
import numpy as np

D_MODEL = 2048
BATCH = 4
SEQ = 2048
DEPTH = 2
DEC_BATCH = 128
DEC_SEQ = 4
PAST_LEN = 16384
PAGE_SIZE = 128

GDN_HEADS = 6
GDN_DK = 128
GDN_DV = 128
MLSTM_HEADS = 6
MLSTM_DK = 128
MLSTM_DV = 128
LRU_WIDTH = 512
LRU_BLOCKS = 8
LRU_C = 8.0
CONV_W = 4
CHUNK = 64
D_FF = 5632
ALPHA = (2 * DEPTH) ** 0.25
BETA_INIT = (8 * DEPTH) ** -0.25
NORM_EPS = 1e-6

GDN_QK = GDN_HEADS * GDN_DK
GDN_V = GDN_HEADS * GDN_DV
ML_QK = MLSTM_HEADS * MLSTM_DK
ML_V = MLSTM_HEADS * MLSTM_DV
GDN_CONV_CH = 2 * GDN_QK + GDN_V
D_MIX = GDN_V + ML_V + LRU_WIDTH
IN_SIZES = (GDN_CONV_CH, GDN_V, GDN_HEADS, GDN_HEADS,
            ML_QK, ML_QK, ML_V, ML_V, MLSTM_HEADS, MLSTM_HEADS,
            LRU_WIDTH, LRU_WIDTH)
D_IN = sum(IN_SIZES)
IN_SPLITS = tuple(int(s) for s in np.cumsum(IN_SIZES)[:-1])

kernel_name = 'hybrid_gdn_mlstm_rglru_macaron_deepnorm_step'


def _layernorm(x, g, b):
    xf = x.astype(jnp.float32)
    mu = jnp.mean(xf, -1, keepdims=True)
    var = jnp.mean(jnp.square(xf - mu), -1, keepdims=True)
    return ((xf - mu) * lax.rsqrt(var + NORM_EPS) * g + b).astype(x.dtype)


def _rmsnorm(x, g):
    xf = x.astype(jnp.float32)
    return xf * lax.rsqrt(jnp.mean(xf * xf, -1, keepdims=True) + NORM_EPS) * g


def _l2norm(x):
    xf = x.astype(jnp.float32)
    return xf * lax.rsqrt(jnp.sum(xf * xf, -1, keepdims=True) + NORM_EPS)


def _swiglu(x, wg, wu, wd):
    return (jax.nn.silu(x @ wg) * (x @ wu)) @ wd


def _causal_conv(x, buf, w):
    T = x.shape[1]
    xp = jnp.concatenate([buf.astype(x.dtype), x], axis=1)
    y = sum(xp[:, k:k + T] * w[k] for k in range(CONV_W))
    return y, xp[:, xp.shape[1] - (CONV_W - 1):]


def _to_chunks(a, C):
    B, T = a.shape[:2]
    a = a.reshape((B, T // C, C) + a.shape[2:])
    a = jnp.moveaxis(a, 3, 2)
    return jnp.moveaxis(a, 1, 0)


def _from_chunks(o):
    N, B, H, C, D = o.shape
    return jnp.moveaxis(jnp.moveaxis(o, 0, 1), 2, 3).reshape(B, N * C, H, D)


def _gated_delta(q, k, v, g, beta, S0):
    f32 = jnp.float32
    T = q.shape[1]
    DV = v.shape[-1]
    C = math.gcd(T, CHUNK)
    qc, kc, vc = _to_chunks(q.astype(f32), C), _to_chunks(k.astype(f32), C), _to_chunks(v.astype(f32), C)
    gc, bc = _to_chunks(g.astype(f32), C), _to_chunks(beta.astype(f32), C)
    G = jnp.cumsum(gc, axis=-1)
    causal = jnp.tril(jnp.ones((C, C), bool))
    strict = jnp.tril(jnp.ones((C, C), bool), -1)
    decay = jnp.exp(jnp.where(causal, G[..., :, None] - G[..., None, :], -jnp.inf))
    kk = jnp.einsum('nbhid,nbhjd->nbhij', kc, kc)
    M = jnp.where(strict, bc[..., :, None] * kk * decay, 0.0) + jnp.eye(C, dtype=f32)
    rhs = jnp.concatenate([bc[..., None] * vc, (bc * jnp.exp(G))[..., None] * kc], axis=-1)
    sol = lax.linalg.triangular_solve(M, rhs, left_side=True, lower=True, unit_diagonal=True)
    Uv, Wk = sol[..., :DV], sol[..., DV:]
    qk = jnp.where(causal, jnp.einsum('nbhid,nbhjd->nbhij', qc, kc) * decay, 0.0)
    q_dec = qc * jnp.exp(G)[..., None]
    k_tail = kc * jnp.exp(G[..., -1:] - G)[..., None]
    g_tot = jnp.exp(G[..., -1])

    def step(S, xs):
        uv, wk, qk_n, qd, kt, gt = xs
        U = uv - jnp.einsum('bhck,bhkv->bhcv', wk, S)
        o = jnp.einsum('bhck,bhkv->bhcv', qd, S) + jnp.einsum('bhij,bhjv->bhiv', qk_n, U)
        S = gt[..., None, None] * S + jnp.einsum('bhck,bhcv->bhkv', kt, U)
        return S, o

    S, o = lax.scan(step, S0.astype(f32), (Uv, Wk, qk, q_dec, k_tail, g_tot))
    return _from_chunks(o), S


def _mlstm(q, k, v, i_pre, logf, C0, n0, m0):
    f32 = jnp.float32
    T = q.shape[1]
    C = math.gcd(T, CHUNK)
    qc, kc, vc = _to_chunks(q.astype(f32), C), _to_chunks(k.astype(f32), C), _to_chunks(v.astype(f32), C)
    ic, fc = _to_chunks(i_pre.astype(f32), C), _to_chunks(logf.astype(f32), C)
    b = jnp.cumsum(fc, axis=-1)
    causal = jnp.tril(jnp.ones((C, C), bool))
    D = jnp.where(causal, b[..., :, None] - b[..., None, :] + ic[..., None, :], -jnp.inf)
    Dmax = jnp.max(D, axis=-1)

    def step(carry, xs):
        Cm, n, m = carry
        q_n, k_n, v_n, b_n, i_n, D_n, Dm_n = xs
        m_t = jnp.maximum(b_n + m[..., None], Dm_n)
        inter = jnp.exp(b_n + m[..., None] - m_t)
        Sw = jnp.exp(D_n - m_t[..., None]) * jnp.einsum('bhik,bhjk->bhij', q_n, k_n)
        num = inter[..., None] * jnp.einsum('bhck,bhkv->bhcv', q_n, Cm) + jnp.einsum('bhij,bhjv->bhiv', Sw, v_n)
        den = inter * jnp.einsum('bhck,bhk->bhc', q_n, n) + jnp.sum(Sw, -1)
        h = num / jnp.maximum(jnp.abs(den), jnp.exp(-m_t))[..., None]
        m_new = m_t[..., -1]
        wk = jnp.exp(b_n[..., -1:] - b_n + i_n - m_new[..., None])
        dec = jnp.exp(b_n[..., -1] + m - m_new)
        Cm = dec[..., None, None] * Cm + jnp.einsum('bhc,bhck,bhcv->bhkv', wk, k_n, v_n)
        n = dec[..., None] * n + jnp.einsum('bhc,bhck->bhk', wk, k_n)
        return (Cm, n, m_new), h

    carry0 = (C0.astype(f32), n0.astype(f32), m0.astype(f32))
    (Cm, n, m), h = lax.scan(step, carry0, (qc, kc, vc, b, ic, D, Dmax))
    return _from_chunks(h), Cm, n, m


def _rglru(x, h0, wr, br, wi, bi, lam):
    f32 = jnp.float32
    B, T, W = x.shape
    xb = x.reshape(B, T, LRU_BLOCKS, W // LRU_BLOCKS)
    r = jax.nn.sigmoid(jnp.einsum('btnd,nde->btne', xb, wr).reshape(B, T, W) + br)
    ig = jax.nn.sigmoid(jnp.einsum('btnd,nde->btne', xb, wi).reshape(B, T, W) + bi)
    log_a = (-LRU_C * r * jax.nn.softplus(-lam)).astype(f32)
    a = jnp.exp(log_a)
    u = jnp.sqrt(-jnp.expm1(2.0 * log_a)) * (ig * x).astype(f32)
    u = u.at[:, 0].add(a[:, 0] * h0.astype(f32))

    def comb(l, rr):
        a1, b1 = l
        a2, b2 = rr
        return a1 * a2, a2 * b1 + b2

    _, h = lax.associative_scan(comb, (a, u), axis=1)
    return h, h[:, -1]


def _layer(x, gS, gconv, mC, mn, mm, lh, lconv,
           ffn1_wg, ffn1_wu, ffn1_wd, ln_g, ln_b, w_in,
           gdn_conv_w, gdn_A_log, gdn_dt_bias, gdn_norm_w,
           mlstm_i_bias, mlstm_f_bias, mlstm_norm_w,
           lru_conv_w, lru_conv_b, lru_wr, lru_br, lru_wi, lru_bi, lru_lambda,
           w_out, ffn2_wg, ffn2_wu, ffn2_wd):
    B, T, _ = x.shape
    f32 = jnp.float32
    x = _layernorm(ALPHA * x + 0.5 * _swiglu(x, ffn1_wg, ffn1_wu, ffn1_wd), ln_g[0], ln_b[0])
    u = x @ w_in
    (g_qkv, g_z, g_b, g_a, m_q, m_k, m_v, m_o, m_i, m_f, l_x, l_y) = jnp.split(u, IN_SPLITS, axis=-1)
    c_qkv, gconv_new = _causal_conv(g_qkv, gconv, gdn_conv_w)
    c_qkv = jax.nn.silu(c_qkv)
    gq, gk, gv = jnp.split(c_qkv, [GDN_QK, 2 * GDN_QK], axis=-1)
    gq = _l2norm(gq.reshape(B, T, GDN_HEADS, GDN_DK)) * GDN_DK ** -0.5
    gk = _l2norm(gk.reshape(B, T, GDN_HEADS, GDN_DK))
    gv = gv.reshape(B, T, GDN_HEADS, GDN_DV)
    beta = jax.nn.sigmoid(g_b.astype(f32))
    g = -jnp.exp(gdn_A_log.astype(f32)) * jax.nn.softplus(g_a.astype(f32) + gdn_dt_bias)
    o_g, gS_new = _gated_delta(gq, gk, gv, g, beta, gS)
    o_g = _rmsnorm(o_g, gdn_norm_w) * jax.nn.silu(g_z.reshape(B, T, GDN_HEADS, GDN_DV).astype(f32))
    mq = m_q.reshape(B, T, MLSTM_HEADS, MLSTM_DK)
    mk = m_k.reshape(B, T, MLSTM_HEADS, MLSTM_DK) * MLSTM_DK ** -0.5
    mv = m_v.reshape(B, T, MLSTM_HEADS, MLSTM_DV)
    i_pre = m_i.astype(f32) + mlstm_i_bias
    logf = jax.nn.log_sigmoid(m_f.astype(f32) + mlstm_f_bias)
    h_m, mC_new, mn_new, mm_new = _mlstm(mq, mk, mv, i_pre, logf, mC, mn, mm)
    h_m = _rmsnorm(h_m, mlstm_norm_w.reshape(MLSTM_HEADS, MLSTM_DV)) * \
        jax.nn.sigmoid(m_o.reshape(B, T, MLSTM_HEADS, MLSTM_DV).astype(f32))
    c_x, lconv_new = _causal_conv(l_x, lconv, lru_conv_w)
    c_x = c_x + lru_conv_b
    h_l, lh_new = _rglru(c_x, lh, lru_wr, lru_br, lru_wi, lru_bi, lru_lambda)
    o_l = h_l * jax.nn.gelu(l_y.astype(f32))
    mix = jnp.concatenate([o_g.reshape(B, T, GDN_V).astype(x.dtype),
                           h_m.reshape(B, T, ML_V).astype(x.dtype),
                           o_l.astype(x.dtype)], axis=-1) @ w_out
    x = _layernorm(ALPHA * x + mix, ln_g[1], ln_b[1])
    x = _layernorm(ALPHA * x + 0.5 * _swiglu(x, ffn2_wg, ffn2_wu, ffn2_wd), ln_g[2], ln_b[2])
    return x, (gS_new, gconv_new, mC_new, mn_new, mm_new, lh_new, lconv_new)


def _trunk(x, gS, gconv, mC, mn, mm, lh, lconv, weights):
    per_layer = []
    for d in range(DEPTH):
        x, st = _layer(x, gS[d], gconv[d], mC[d], mn[d], mm[d], lh[d], lconv[d],
                       *[w[d] for w in weights])
        per_layer.append(st)
    stacked = [jnp.stack(s) for s in zip(*per_layer)]
    return x, stacked


def setup_inputs(seed: int = 0) -> dict:
    key = jax.random.key(seed)
    ks = iter(jax.random.split(key, 48))
    nrm = lambda shape, s: jax.random.normal(next(ks), shape, jnp.float32) * s
    f32 = jnp.float32
    inp = {}
    inp['x_prompt'] = nrm((BATCH, SEQ, D_MODEL), 1.0)
    inp['x_sample'] = nrm((DEC_BATCH, DEC_SEQ, D_MODEL), 1.0)
    inp['state_gdn_S'] = nrm((DEPTH, DEC_BATCH, GDN_HEADS, GDN_DK, GDN_DV), 0.3)
    inp['state_gdn_conv'] = nrm((DEPTH, DEC_BATCH, CONV_W - 1, GDN_CONV_CH), 1.0)
    inp['state_mlstm_C'] = nrm((DEPTH, DEC_BATCH, MLSTM_HEADS, MLSTM_DK, MLSTM_DV), 0.3)
    inp['state_mlstm_n'] = nrm((DEPTH, DEC_BATCH, MLSTM_HEADS, MLSTM_DK), 0.3)
    inp['state_mlstm_m'] = nrm((DEPTH, DEC_BATCH, MLSTM_HEADS), 1.0)
    inp['state_lru_h'] = nrm((DEPTH, DEC_BATCH, LRU_WIDTH), 0.5)
    inp['state_lru_conv'] = nrm((DEPTH, DEC_BATCH, CONV_W - 1, LRU_WIDTH), 1.0)
    inp['ffn1_wg'] = nrm((DEPTH, D_MODEL, D_FF), D_MODEL ** -0.5)
    inp['ffn1_wu'] = nrm((DEPTH, D_MODEL, D_FF), D_MODEL ** -0.5)
    inp['ffn1_wd'] = nrm((DEPTH, D_FF, D_MODEL), D_FF ** -0.5 * BETA_INIT)
    inp['ln_g'] = 1.0 + nrm((DEPTH, 3, D_MODEL), 0.02)
    inp['ln_b'] = nrm((DEPTH, 3, D_MODEL), 0.02)
    inp['w_in'] = nrm((DEPTH, D_MODEL, D_IN), D_MODEL ** -0.5)
    inp['gdn_conv_w'] = nrm((DEPTH, CONV_W, GDN_CONV_CH), CONV_W ** -0.5)
    inp['gdn_A_log'] = jnp.log(jax.random.uniform(next(ks), (DEPTH, GDN_HEADS), f32, 1.0, 16.0))
    dt = jnp.exp(jax.random.uniform(next(ks), (DEPTH, GDN_HEADS), f32, math.log(1e-3), math.log(1e-1)))
    inp['gdn_dt_bias'] = dt + jnp.log(-jnp.expm1(-dt))
    inp['gdn_norm_w'] = 1.0 + nrm((DEPTH, GDN_DV), 0.02)
    inp['mlstm_i_bias'] = nrm((DEPTH, MLSTM_HEADS), 0.1)
    inp['mlstm_f_bias'] = jnp.linspace(3.0, 6.0, MLSTM_HEADS, dtype=f32)[None] + nrm((DEPTH, MLSTM_HEADS), 0.1)
    inp['mlstm_norm_w'] = 1.0 + nrm((DEPTH, ML_V), 0.02)
    inp['lru_conv_w'] = nrm((DEPTH, CONV_W, LRU_WIDTH), CONV_W ** -0.5)
    inp['lru_conv_b'] = nrm((DEPTH, LRU_WIDTH), 0.02)
    blk = LRU_WIDTH // LRU_BLOCKS
    inp['lru_wr'] = nrm((DEPTH, LRU_BLOCKS, blk, blk), blk ** -0.5)
    inp['lru_br'] = nrm((DEPTH, LRU_WIDTH), 0.02)
    inp['lru_wi'] = nrm((DEPTH, LRU_BLOCKS, blk, blk), blk ** -0.5)
    inp['lru_bi'] = nrm((DEPTH, LRU_WIDTH), 0.02)
    s = jax.random.uniform(next(ks), (DEPTH, LRU_WIDTH), f32, 0.9 ** (1.0 / LRU_C), 0.999 ** (1.0 / LRU_C))
    inp['lru_lambda'] = jnp.log(s) - jnp.log1p(-s)
    inp['w_out'] = nrm((DEPTH, D_MIX, D_MODEL), D_MIX ** -0.5 * BETA_INIT)
    inp['ffn2_wg'] = nrm((DEPTH, D_MODEL, D_FF), D_MODEL ** -0.5)
    inp['ffn2_wu'] = nrm((DEPTH, D_MODEL, D_FF), D_MODEL ** -0.5)
    inp['ffn2_wd'] = nrm((DEPTH, D_FF, D_MODEL), D_FF ** -0.5 * BETA_INIT)
    return inp


def reference(x_prompt, x_sample, state_gdn_S, state_gdn_conv, state_mlstm_C, state_mlstm_n,
              state_mlstm_m, state_lru_h, state_lru_conv,
              ffn1_wg, ffn1_wu, ffn1_wd, ln_g, ln_b, w_in,
              gdn_conv_w, gdn_A_log, gdn_dt_bias, gdn_norm_w,
              mlstm_i_bias, mlstm_f_bias, mlstm_norm_w,
              lru_conv_w, lru_conv_b, lru_wr, lru_br, lru_wi, lru_bi, lru_lambda,
              w_out, ffn2_wg, ffn2_wu, ffn2_wd):
    weights = (ffn1_wg, ffn1_wu, ffn1_wd, ln_g, ln_b, w_in,
               gdn_conv_w, gdn_A_log, gdn_dt_bias, gdn_norm_w,
               mlstm_i_bias, mlstm_f_bias, mlstm_norm_w,
               lru_conv_w, lru_conv_b, lru_wr, lru_br, lru_wi, lru_bi, lru_lambda,
               w_out, ffn2_wg, ffn2_wu, ffn2_wd)
    nb = x_prompt.shape[0]
    zero_states = [jnp.zeros((DEPTH, nb) + s.shape[2:], jnp.float32)
                   for s in (state_gdn_S, state_gdn_conv, state_mlstm_C, state_mlstm_n,
                             state_mlstm_m, state_lru_h, state_lru_conv)]
    y_prompt, (p_gS, p_gconv, p_mC, p_mn, p_mm, p_lh, p_lconv) = _trunk(x_prompt, *zero_states, weights)
    y_sample, (s_gS, s_gconv, s_mC, s_mn, s_mm, s_lh, s_lconv) = _trunk(
        x_sample, state_gdn_S, state_gdn_conv, state_mlstm_C, state_mlstm_n,
        state_mlstm_m, state_lru_h, state_lru_conv, weights)
    return (y_prompt, y_sample,
            p_gS, p_gconv, p_mC, p_mn, p_mm, p_lh, p_lconv,
            s_gS, s_gconv, s_mC, s_mn, s_mm, s_lh, s_lconv)
```

```python
import functools
import math

import jax
import jax.numpy as jnp
from jax import lax
from jax.experimental import pallas as pl
from jax.experimental.pallas import tpu as pltpu

f32 = jnp.float32
bf16 = jnp.bfloat16
HI = lax.Precision.HIGHEST

DEPTH = 2
D_MODEL = 2048
D_FF = 5632
HEADS = 6
DK = 128
LRU_W = 512
LRU_BLOCKS = 8
LRU_C = 8.0
CONV_W = 4
CHUNK = 64
ALPHA = (2 * DEPTH) ** 0.25
NORM_EPS = 1e-6
NEG = -1e30

QK = HEADS * DK
CONV_CH = 3 * QK
CB_Q, CB_K, CB_V, CB_Z = 0, 6, 12, 18
CB_MQ, CB_MK, CB_MV, CB_MO = 24, 30, 36, 42
CB_LX, CB_LY = 12, 13
CB_GATES = 56
COL_LX = 6144
D_INP = 7296
GROUP = 8

VMEM_LIMIT = 56 * 1024 * 1024


def _cparams(sem):
    return pltpu.CompilerParams(dimension_semantics=sem, vmem_limit_bytes=VMEM_LIMIT)


def _silu(x):
    return x * jax.nn.sigmoid(x)


def _softplus(x):
    return jnp.maximum(x, 0.0) + jnp.log1p(jnp.exp(-jnp.abs(x)))


def _expm1(x):
    u = jnp.exp(x)
    um1 = u - 1.0
    lg = jnp.where(u == 1.0, 1.0, jnp.log(jnp.where(u == 0.0, 1.0, u)))
    return jnp.where(u == 1.0, x, jnp.where(u == 0.0, -1.0, um1 * x / lg))


def _layernorm(y, g, b):
    mu = jnp.mean(y, -1, keepdims=True)
    d = y - mu
    var = jnp.mean(d * d, -1, keepdims=True)
    return d * lax.rsqrt(var + NORM_EPS) * g + b


def _rms(x):
    return x * lax.rsqrt(jnp.mean(x * x, -1, keepdims=True) + NORM_EPS)


def _l2(x):
    return x * lax.rsqrt(jnp.sum(x * x, -1, keepdims=True) + NORM_EPS)


def _colsel(gates, idx):
    lane = lax.broadcasted_iota(jnp.int32, gates.shape, 1)
    return jnp.sum(jnp.where(lane == idx, gates, 0.0), axis=1, keepdims=True)


def _dot(a, b):
    return jnp.dot(a, b, preferred_element_type=f32)


def _dot_tn(a, b):
    return lax.dot_general(a, b, (((0,), (0,)), ((), ())), preferred_element_type=f32)


def _bmm(a, b, precision=None):
    return jnp.einsum('nij,njk->nik', a, b, precision=precision, preferred_element_type=f32)


def _bmm_nt(a, b):
    return jnp.einsum('nid,njd->nij', a, b, preferred_element_type=f32)


def _ij(C):
    return (lax.broadcasted_iota(jnp.int32, (C, C), 0), lax.broadcasted_iota(jnp.int32, (C, C), 1))


def _cumsum_row(col3, ii, jj):
    return jnp.sum(jnp.where((ii <= jj)[None], col3, 0.0), axis=1, keepdims=True)


def _row2col(row3, ii, jj):
    return jnp.sum(jnp.where((ii == jj)[None], row3, 0.0), axis=2, keepdims=True)


def _col2row(col3, ii, jj):
    return jnp.sum(jnp.where((ii == jj)[None], col3, 0.0), axis=1, keepdims=True)


def _conv_zero_hist(x, w_ref, xp_ref):
    T = x.shape[0]
    xp_ref[0:8, :] = jnp.zeros((8, x.shape[1]), f32)
    xp_ref[8:T + 8, :] = x
    y = w_ref[3:4, :] * x
    for k in range(CONV_W - 1):
        y = y + w_ref[k:k + 1, :] * xp_ref[pl.ds(5 + k, T), :]
    return y


def _conv_groups(x, w_ref):
    y = w_ref[3:4, :] * x
    for s in range(1, CONV_W):
        y = y + w_ref[3 - s:4 - s, :] * pltpu.roll(x, s, axis=0)
    return y


def _ffn_ln_kernel(x_ref, wg_ref, wu_ref, wd_ref, g_ref, b_ref, o_ref, acc_ref, xb_ref):
    j = pl.program_id(1)

    @pl.when(j == 0)
    def _():
        acc_ref[...] = jnp.zeros_like(acc_ref)
        xb_ref[...] = x_ref[...].astype(bf16)

    xb = xb_ref[...]
    hg = _dot(xb, wg_ref[...])
    hu = _dot(xb, wu_ref[...])
    h = (_silu(hg) * hu).astype(bf16)
    acc_ref[...] += _dot(h, wd_ref[...])

    @pl.when(j == pl.num_programs(1) - 1)
    def _():
        y = ALPHA * x_ref[...] + 0.5 * acc_ref[...]
        o_ref[...] = _layernorm(y, g_ref[...], b_ref[...])


def _ffn_ln(x, wg, wu, wd, g, b, *, tm=512, tf=512):
    M, D = x.shape
    F = wg.shape[1]
    return pl.pallas_call(
        _ffn_ln_kernel,
        grid=(M // tm, F // tf),
        in_specs=[
            pl.BlockSpec((tm, D), lambda i, j: (i, 0)),
            pl.BlockSpec((D, tf), lambda i, j: (0, j)),
            pl.BlockSpec((D, tf), lambda i, j: (0, j)),
            pl.BlockSpec((tf, D), lambda i, j: (j, 0)),
            pl.BlockSpec((1, D), lambda i, j: (0, 0)),
            pl.BlockSpec((1, D), lambda i, j: (0, 0)),
        ],
        out_specs=pl.BlockSpec((tm, D), lambda i, j: (i, 0)),
        out_shape=jax.ShapeDtypeStruct((M, D), f32),
        scratch_shapes=[pltpu.VMEM((tm, D), f32), pltpu.VMEM((tm, D), bf16)],
        compiler_params=_cparams(("parallel", "arbitrary")),
        name="ffn_ln",
    )(x, wg, wu, wd, g.reshape(1, D), b.reshape(1, D))


def _inproj_kernel(x_ref, w_ref, o_ref):
    o_ref[...] = _dot(x_ref[...].astype(bf16), w_ref[...])


def _inproj(x, w, *, tm=512, tn=2432):
    M, D = x.shape
    N = w.shape[1]
    return pl.pallas_call(
        _inproj_kernel,
        grid=(N // tn, M // tm),
        in_specs=[pl.BlockSpec((tm, D), lambda n, i: (i, 0)),
                  pl.BlockSpec((D, tn), lambda n, i: (0, n))],
        out_specs=pl.BlockSpec((tm, tn), lambda n, i: (i, n)),
        out_shape=jax.ShapeDtypeStruct((M, N), f32),
        compiler_params=_cparams(("parallel", "arbitrary")),
        name="in_proj",
    )(x, w)


def _outproj_ln_kernel(mix_ref, x_ref, w_ref, g_ref, b_ref, o_ref):
    y = ALPHA * x_ref[...] + _dot(mix_ref[...].astype(bf16), w_ref[...])
    o_ref[...] = _layernorm(y, g_ref[...], b_ref[...])


def _outproj_ln(mix, x, w, g, b, *, tm=512):
    M, D = x.shape
    return pl.pallas_call(
        _outproj_ln_kernel,
        grid=(M // tm,),
        in_specs=[pl.BlockSpec((tm, D), lambda i: (i, 0)),
                  pl.BlockSpec((tm, D), lambda i: (i, 0)),
                  pl.BlockSpec((D, D), lambda i: (0, 0)),
                  pl.BlockSpec((1, D), lambda i: (0, 0)),
                  pl.BlockSpec((1, D), lambda i: (0, 0))],
        out_specs=pl.BlockSpec((tm, D), lambda i: (i, 0)),
        out_shape=jax.ShapeDtypeStruct((M, D), f32),
        compiler_params=_cparams(("parallel",)),
        name="out_proj_ln",
    )(mix, x, w, g.reshape(1, D), b.reshape(1, D))


def _gdn_prompt_kernel(q_ref, k_ref, v_ref, z_ref, gt_ref, wq_ref, wk_ref, wv_ref, hc_ref, nw_ref,
                       o_ref, s_ref,
                       xp_ref, uv_ref, wks_ref, qk_ref, qd_ref, kt_ref, gtot_ref, *, C):
    h = pl.program_id(1)
    T = q_ref.shape[0]
    N = T // C
    q = _l2(_silu(_conv_zero_hist(q_ref[...], wq_ref, xp_ref))) * (DK ** -0.5)
    k = _l2(_silu(_conv_zero_hist(k_ref[...], wk_ref, xp_ref)))
    v = _silu(_conv_zero_hist(v_ref[...], wv_ref, xp_ref))
    gates = gt_ref[...]
    beta = jax.nn.sigmoid(_colsel(gates, h))
    g = -jnp.exp(hc_ref[0:1, 0:1]) * _softplus(_colsel(gates, HEADS + h) + hc_ref[1:2, 0:1])

    q3, k3, v3 = q.reshape(N, C, DK), k.reshape(N, C, DK), v.reshape(N, C, DK)
    g3, b3 = g.reshape(N, C, 1), beta.reshape(N, C, 1)
    ii, jj = _ij(C)
    g_row = _cumsum_row(g3, ii, jj)
    g_col = _row2col(g_row, ii, jj)
    decay = jnp.exp(jnp.where((ii >= jj)[None], g_col - g_row, NEG))
    kb = k3.astype(bf16)
    kk = _bmm_nt(kb, kb)
    p = jnp.where((ii > jj)[None], -(b3 * kk * decay), 0.0)
    tinv = (ii == jj).astype(f32)[None] + p
    for _ in range(int(math.log2(C)) - 1):
        p = _bmm(p, p, HI)
        tinv = tinv + _bmm(tinv, p, HI)
    e_g = jnp.exp(g_col)
    uv_ref[...] = _bmm(tinv, b3 * v3, HI)
    wks_ref[...] = _bmm(tinv, (b3 * e_g) * k3, HI).astype(bf16)
    qk_ref[...] = (_bmm_nt(q3.astype(bf16), kb) * decay).astype(bf16)
    g_last = g_row[:, :, C - 1:C]
    qd_ref[...] = (q3 * e_g).astype(bf16)
    kt_ref[...] = (k3 * jnp.exp(g_last - g_col)).astype(bf16)
    gtot_ref[...] = jnp.broadcast_to(jnp.exp(g_last), (N, 1, DK))

    def step(n, S):
        Sb = S.astype(bf16)
        U = uv_ref[n] - _dot(wks_ref[n], Sb)
        Ub = U.astype(bf16)
        o_ref[pl.ds(pl.multiple_of(n * C, C), C), :] = _dot(qd_ref[n], Sb) + _dot(qk_ref[n], Ub)
        return gtot_ref[n] * S + _dot_tn(kt_ref[n], Ub)

    s_ref[...] = lax.fori_loop(0, N, step, jnp.zeros((DK, DK), f32))
    o_ref[...] = _rms(o_ref[...]) * nw_ref[...] * _silu(z_ref[...])


def _gdn_prompt(U, B, T, conv_w, hc, norm_w, *, C=CHUNK):
    N = T // C
    tok = lambda cb: pl.BlockSpec((T, DK), lambda b, h: (b, cb + h))
    cw = lambda cb: pl.BlockSpec((CONV_W, DK), lambda b, h: (0, cb + h))
    return pl.pallas_call(
        functools.partial(_gdn_prompt_kernel, C=C),
        grid=(B, HEADS),
        in_specs=[tok(CB_Q), tok(CB_K), tok(CB_V), tok(CB_Z),
                  pl.BlockSpec((T, DK), lambda b, h: (b, CB_GATES)),
                  cw(CB_Q), cw(CB_K), cw(CB_V),
                  pl.BlockSpec((None, 8, DK), lambda b, h: (h, 0, 0)),
                  pl.BlockSpec((1, DK), lambda b, h: (0, 0))],
        out_specs=[pl.BlockSpec((T, DK), lambda b, h: (b, h)),
                   pl.BlockSpec((None, None, DK, DK), lambda b, h: (b, h, 0, 0))],
        out_shape=[jax.ShapeDtypeStruct((B * T, QK), f32),
                   jax.ShapeDtypeStruct((B, HEADS, DK, DK), f32)],
        scratch_shapes=[pltpu.VMEM((T + 8, DK), f32),
                        pltpu.VMEM((N, C, DK), f32), pltpu.VMEM((N, C, DK), bf16),
                        pltpu.VMEM((N, C, C), bf16), pltpu.VMEM((N, C, DK), bf16),
                        pltpu.VMEM((N, C, DK), bf16), pltpu.VMEM((N, 1, DK), f32)],
        compiler_params=_cparams(("parallel", "parallel")),
        name="gdn_prompt",
    )(U, U, U, U, U, conv_w, conv_w, conv_w, hc, norm_w.reshape(1, DK))


def _mlstm_prompt_kernel(q_ref, k_ref, v_ref, og_ref, gt_ref, hc_ref, nw_ref,
                         h_ref, c_ref, n_ref, m_ref,
                         d_ref, qk_ref, dm_ref, b_ref, i_ref, *, C):
    h = pl.program_id(1)
    T = q_ref.shape[0]
    N = T // C
    scale = DK ** -0.5
    gates = gt_ref[...]
    i_pre = _colsel(gates, 2 * HEADS + h) + hc_ref[2:3, 0:1]
    logf = -_softplus(-(_colsel(gates, 3 * HEADS + h) + hc_ref[3:4, 0:1]))
    i3, f3 = i_pre.reshape(N, C, 1), logf.reshape(N, C, 1)
    ii, jj = _ij(C)
    b_row = _cumsum_row(f3, ii, jj)
    b_col = _row2col(b_row, ii, jj)
    i_row = _col2row(i3, ii, jj)
    D = jnp.where((ii >= jj)[None], b_col - b_row + i_row, NEG)
    d_ref[...] = D
    dm_ref[...] = jnp.max(D, axis=2, keepdims=True)
    b_ref[...] = b_col
    i_ref[...] = i3
    q3 = q_ref[...].reshape(N, C, DK).astype(bf16)
    k3 = (k_ref[...] * scale).reshape(N, C, DK).astype(bf16)
    qk_ref[...] = _bmm_nt(q3, k3)

    def step(n, carry):
        Cm, nr, m = carry
        rows = pl.ds(pl.multiple_of(n * C, C), C)
        qn = q_ref[rows, :]
        kn = k_ref[rows, :] * scale
        vb = v_ref[rows, :].astype(bf16)
        bn = b_ref[n]
        m_t = jnp.maximum(bn + m, dm_ref[n])
        inter = jnp.exp(bn + m - m_t)
        Sw = jnp.exp(d_ref[n] - m_t) * qk_ref[n]
        num = inter * _dot(qn.astype(bf16), Cm.astype(bf16)) + _dot(Sw.astype(bf16), vb)
        den = inter * jnp.sum(qn * nr, -1, keepdims=True) + jnp.sum(Sw, -1, keepdims=True)
        h_ref[rows, :] = num / jnp.maximum(jnp.abs(den), jnp.exp(-m_t))
        m_new = m_t[C - 1:C]
        b_last = bn[C - 1:C]
        kw = jnp.exp(b_last - bn + i_ref[n] - m_new) * kn
        dec = jnp.exp(b_last + m - m_new)
        Cm = dec * Cm + _dot_tn(kw.astype(bf16), vb)
        nr = dec * nr + jnp.sum(kw, axis=0, keepdims=True)
        return Cm, nr, m_new

    Cm, nr, m = lax.fori_loop(
        0, N, step, (jnp.zeros((DK, DK), f32), jnp.zeros((1, DK), f32), jnp.zeros((1, 1), f32)))
    c_ref[...] = Cm
    n_ref[...] = nr
    m_ref[...] = jnp.broadcast_to(m, (1, DK))
    h_ref[...] = _rms(h_ref[...]) * nw_ref[...] * jax.nn.sigmoid(og_ref[...])


def _mlstm_prompt(U, B, T, hc, norm_w, *, C=CHUNK):
    N = T // C
    tok = lambda cb: pl.BlockSpec((T, DK), lambda b, h: (b, cb + h))
    st = lambda r: pl.BlockSpec((None, None, r, DK), lambda b, h: (b, h, 0, 0))
    return pl.pallas_call(
        functools.partial(_mlstm_prompt_kernel, C=C),
        grid=(B, HEADS),
        in_specs=[tok(CB_MQ), tok(CB_MK), tok(CB_MV), tok(CB_MO),
                  pl.BlockSpec((T, DK), lambda b, h: (b, CB_GATES)),
                  pl.BlockSpec((None, 8, DK), lambda b, h: (h, 0, 0)),
                  pl.BlockSpec((None, 1, DK), lambda b, h: (h, 0, 0))],
        out_specs=[pl.BlockSpec((T, DK), lambda b, h: (b, h)), st(DK), st(1), st(1)],
        out_shape=[jax.ShapeDtypeStruct((B * T, QK), f32),
                   jax.ShapeDtypeStruct((B, HEADS, DK, DK), f32),
                   jax.ShapeDtypeStruct((B, HEADS, 1, DK), f32),
                   jax.ShapeDtypeStruct((B, HEADS, 1, DK), f32)],
        scratch_shapes=[pltpu.VMEM((N, C, C), f32), pltpu.VMEM((N, C, C), f32),
                        pltpu.VMEM((N, C, 1), f32), pltpu.VMEM((N, C, 1), f32),
                        pltpu.VMEM((N, C, 1), f32)],
        compiler_params=_cparams(("parallel", "parallel")),
        name="mlstm_prompt",
    )(U, U, U, U, U, hc, norm_w.reshape(HEADS, 1, DK))


def _lru_gates(c, wr_ref, br_ref, wi_ref, bi_ref, lam_ref):
    cb = c.astype(bf16)
    r = jax.nn.sigmoid(_dot(cb, wr_ref[...]) + br_ref[...])
    ig = jax.nn.sigmoid(_dot(cb, wi_ref[...]) + bi_ref[...])
    log_a = -LRU_C * r * _softplus(-lam_ref[...])
    a = jnp.exp(log_a)
    u = jnp.sqrt(-_expm1(2.0 * log_a)) * (ig * c)
    return a, u


def _lru_prompt_kernel(x_ref, y_ref, cw_ref, cb_ref, wr_ref, br_ref, wi_ref, bi_ref, lam_ref,
                       o_ref, hl_ref, xp_ref, tail_ref, hc_ref):
    tc = pl.program_id(1)
    Tc, W = x_ref.shape

    @pl.when(tc == 0)
    def _():
        tail_ref[...] = jnp.zeros_like(tail_ref)
        hc_ref[...] = jnp.zeros_like(hc_ref)

    x = x_ref[...]
    xp_ref[0:8, :] = tail_ref[...]
    xp_ref[8:Tc + 8, :] = x
    c = cw_ref[3:4, :] * x + cb_ref[...]
    for k in range(CONV_W - 1):
        c = c + cw_ref[k:k + 1, :] * xp_ref[pl.ds(5 + k, Tc), :]
    tail_ref[...] = x[Tc - 8:Tc]
    a, u = _lru_gates(c, wr_ref, br_ref, wi_ref, bi_ref, lam_ref)
    row = lax.broadcasted_iota(jnp.int32, (Tc, W), 0)
    d = 1
    while d < Tc:
        keep = row >= d
        a_sh = jnp.where(keep, pltpu.roll(a, d, axis=0), 1.0)
        u_sh = jnp.where(keep, pltpu.roll(u, d, axis=0), 0.0)
        u = a * u_sh + u
        a = a * a_sh
        d *= 2
    hs = u + a * hc_ref[0:1, :]
    last = hs[Tc - 1:Tc]
    hc_ref[...] = jnp.broadcast_to(last, hc_ref.shape)
    hl_ref[...] = last
    o_ref[...] = hs * jax.nn.gelu(y_ref[...])


def _lru_prompt(U, B, T, cw, cb, wr, br, wi, bi, lam, *, Tc=512):
    W = LRU_W
    nt = T // Tc
    vec = lambda: pl.BlockSpec((1, W), lambda b, t: (0, 0))
    return pl.pallas_call(
        _lru_prompt_kernel,
        grid=(B, nt),
        in_specs=[pl.BlockSpec((Tc, W), lambda b, t: (b * nt + t, CB_LX)),
                  pl.BlockSpec((Tc, W), lambda b, t: (b * nt + t, CB_LY)),
                  pl.BlockSpec((CONV_W, W), lambda b, t: (0, 0)), vec(),
                  pl.BlockSpec((W, W), lambda b, t: (0, 0)), vec(),
                  pl.BlockSpec((W, W), lambda b, t: (0, 0)), vec(), vec()],
        out_specs=[pl.BlockSpec((Tc, W), lambda b, t: (b * nt + t, 0)),
                   pl.BlockSpec((None, 1, W), lambda b, t: (b, 0, 0))],
        out_shape=[jax.ShapeDtypeStruct((B * T, W), f32),
                   jax.ShapeDtypeStruct((B, 1, W), f32)],
        scratch_shapes=[pltpu.VMEM((Tc + 8, W), f32), pltpu.VMEM((8, W), f32), pltpu.VMEM((8, W), f32)],
        compiler_params=_cparams(("parallel", "arbitrary")),
        name="lru_prompt",
    )(U, U, cw, cb.reshape(1, W), wr, br.reshape(1, W), wi, bi.reshape(1, W), lam.reshape(1, W))


def _gdn_sample_kernel(q_ref, k_ref, v_ref, z_ref, gt_ref, wq_ref, wk_ref, wv_ref, hc_ref, nw_ref, s0_ref,
                       o_ref, s_ref):
    h = pl.program_id(0)
    BB = s0_ref.shape[0]
    q = _l2(_silu(_conv_groups(q_ref[...], wq_ref))) * (DK ** -0.5)
    k = _l2(_silu(_conv_groups(k_ref[...], wk_ref)))
    v = _silu(_conv_groups(v_ref[...], wv_ref))
    gates = gt_ref[...]
    beta = jax.nn.sigmoid(_colsel(gates, h))
    g = -jnp.exp(hc_ref[0:1, 0:1]) * _softplus(_colsel(gates, HEADS + h) + hc_ref[1:2, 0:1])
    eg = jnp.exp(g)
    qT, kT = q.T, k.T
    o_ref[...] = jnp.zeros_like(o_ref)
    for b in range(BB):
        S = s0_ref[b]
        for t in range(GROUP - 4):
            r = b * GROUP + 4 + t
            kc = kT[:, r:r + 1]
            e = eg[r:r + 1]
            kS = jnp.sum(kc * S, axis=0, keepdims=True)
            w = beta[r:r + 1] * (v[r:r + 1] - e * kS)
            S = e * S + kc * w
            o_ref[r:r + 1, :] = jnp.sum(qT[:, r:r + 1] * S, axis=0, keepdims=True)
        s_ref[b] = S
    o_ref[...] = _rms(o_ref[...]) * nw_ref[...] * _silu(z_ref[...])


def _gdn_sample(U8, S0, conv_w, hc, norm_w, *, BB=16):
    B = S0.shape[0]
    R = BB * GROUP
    tok = lambda cb: pl.BlockSpec((R, DK), lambda h, b: (b, cb + h))
    cw = lambda cb: pl.BlockSpec((CONV_W, DK), lambda h, b: (0, cb + h))
    st = pl.BlockSpec((BB, None, DK, DK), lambda h, b: (b, h, 0, 0))
    return pl.pallas_call(
        _gdn_sample_kernel,
        grid=(HEADS, B // BB),
        in_specs=[tok(CB_Q), tok(CB_K), tok(CB_V), tok(CB_Z),
                  pl.BlockSpec((R, DK), lambda h, b: (b, CB_GATES)),
                  cw(CB_Q), cw(CB_K), cw(CB_V),
                  pl.BlockSpec((None, 8, DK), lambda h, b: (h, 0, 0)),
                  pl.BlockSpec((1, DK), lambda h, b: (0, 0)), st],
        out_specs=[pl.BlockSpec((R, DK), lambda h, b: (b, h)), st],
        out_shape=[jax.ShapeDtypeStruct((B * GROUP, QK), f32),
                   jax.ShapeDtypeStruct((B, HEADS, DK, DK), f32)],
        compiler_params=_cparams(("parallel", "parallel")),
        name="gdn_sample",
    )(U8, U8, U8, U8, U8, conv_w, conv_w, conv_w, hc, norm_w.reshape(1, DK), S0)


def _mlstm_sample_kernel(q_ref, k_ref, v_ref, og_ref, gt_ref, hc_ref, nw_ref, c0_ref, n0_ref, m0_ref,
                         h_ref, c_ref, n_ref, m_ref):
    h = pl.program_id(0)
    BB = c0_ref.shape[0]
    q = q_ref[...]
    k = k_ref[...] * (DK ** -0.5)
    v = v_ref[...]
    gates = gt_ref[...]
    i_pre = _colsel(gates, 2 * HEADS + h) + hc_ref[2:3, 0:1]
    logf = -_softplus(-(_colsel(gates, 3 * HEADS + h) + hc_ref[3:4, 0:1]))
    qT, kT = q.T, k.T
    h_ref[...] = jnp.zeros_like(h_ref)
    for b in range(BB):
        Cm = c0_ref[b]
        nr = n0_ref[b]
        m = m0_ref[b][:, 0:1]
        for t in range(GROUP - 4):
            r = b * GROUP + 4 + t
            lf = logf[r:r + 1]
            m_t = jnp.maximum(lf + m, i_pre[r:r + 1])
            fd = jnp.exp(lf + m - m_t)
            ig = jnp.exp(i_pre[r:r + 1] - m_t)
            Cm = fd * Cm + (ig * kT[:, r:r + 1]) * v[r:r + 1]
            nr = fd * nr + ig * k[r:r + 1]
            num = jnp.sum(qT[:, r:r + 1] * Cm, axis=0, keepdims=True)
            den = jnp.sum(q[r:r + 1] * nr, axis=1, keepdims=True)
            h_ref[r:r + 1, :] = num / jnp.maximum(jnp.abs(den), jnp.exp(-m_t))
            m = m_t
        c_ref[b] = Cm
        n_ref[b] = nr
        m_ref[b] = jnp.broadcast_to(m, (1, DK))
    h_ref[...] = _rms(h_ref[...]) * nw_ref[...] * jax.nn.sigmoid(og_ref[...])


def _mlstm_sample(U8, C0, n0, m0, hc, norm_w, *, BB=16):
    B = C0.shape[0]
    R = BB * GROUP
    tok = lambda cb: pl.BlockSpec((R, DK), lambda h, b: (b, cb + h))
    st = lambda r: pl.BlockSpec((BB, None, r, DK), lambda h, b: (b, h, 0, 0))
    n0 = n0.reshape(B, HEADS, 1, DK)
    m0 = jnp.broadcast_to(m0[:, :, None, None], (B, HEADS, 1, DK))
    return pl.pallas_call(
        _mlstm_sample_kernel,
        grid=(HEADS, B // BB),
        in_specs=[tok(CB_MQ), tok(CB_MK), tok(CB_MV), tok(CB_MO),
                  pl.BlockSpec((R, DK), lambda h, b: (b, CB_GATES)),
                  pl.BlockSpec((None, 8, DK), lambda h, b: (h, 0, 0)),
                  pl.BlockSpec((None, 1, DK), lambda h, b: (h, 0, 0)),
                  st(DK), st(1), st(1)],
        out_specs=[pl.BlockSpec((R, DK), lambda h, b: (b, h)), st(DK), st(1), st(1)],
        out_shape=[jax.ShapeDtypeStruct((B * GROUP, QK), f32),
                   jax.ShapeDtypeStruct((B, HEADS, DK, DK), f32),
                   jax.ShapeDtypeStruct((B, HEADS, 1, DK), f32),
                   jax.ShapeDtypeStruct((B, HEADS, 1, DK), f32)],
        compiler_params=_cparams(("parallel", "parallel")),
        name="mlstm_sample",
    )(U8, U8, U8, U8, U8, hc, norm_w.reshape(HEADS, 1, DK), C0, n0, m0)


def _lru_sample_kernel(x_ref, y_ref, h0_ref, cw_ref, cb_ref, wr_ref, br_ref, wi_ref, bi_ref, lam_ref,
                       o_ref, h_ref):
    c = _conv_groups(x_ref[...], cw_ref) + cb_ref[...]
    a, u = _lru_gates(c, wr_ref, br_ref, wi_ref, bi_ref, lam_ref)
    hs = h0_ref[...]
    pos = lax.broadcasted_iota(jnp.int32, hs.shape, 0) % GROUP
    for t in range(GROUP - 4):
        hs = jnp.where(pos == 4 + t, a * pltpu.roll(hs, 1, axis=0) + u, hs)
    h_ref[...] = hs
    o_ref[...] = hs * jax.nn.gelu(y_ref[...])


def _lru_sample(U8, h0g, cw, cb, wr, br, wi, bi, lam, *, R=128):
    W = LRU_W
    rows = U8.shape[0]
    vec = lambda: pl.BlockSpec((1, W), lambda i: (0, 0))
    return pl.pallas_call(
        _lru_sample_kernel,
        grid=(rows // R,),
        in_specs=[pl.BlockSpec((R, W), lambda i: (i, CB_LX)),
                  pl.BlockSpec((R, W), lambda i: (i, CB_LY)),
                  pl.BlockSpec((R, W), lambda i: (i, 0)),
                  pl.BlockSpec((CONV_W, W), lambda i: (0, 0)), vec(),
                  pl.BlockSpec((W, W), lambda i: (0, 0)), vec(),
                  pl.BlockSpec((W, W), lambda i: (0, 0)), vec(), vec()],
        out_specs=[pl.BlockSpec((R, W), lambda i: (i, 0)), pl.BlockSpec((R, W), lambda i: (i, 0))],
        out_shape=[jax.ShapeDtypeStruct((rows, W), f32), jax.ShapeDtypeStruct((rows, W), f32)],
        compiler_params=_cparams(("parallel",)),
        name="lru_sample",
    )(U8, U8, h0g, cw, cb.reshape(1, W), wr, br.reshape(1, W), wi, bi.reshape(1, W), lam.reshape(1, W))


def _permute_w_in(w_in):
    s = {}
    off = 0
    for name, size in (("qkv", CONV_CH), ("z", QK), ("gb", HEADS), ("ga", HEADS), ("mq", QK), ("mk", QK),
                       ("mv", QK), ("mo", QK), ("mi", HEADS), ("mf", HEADS), ("lx", LRU_W), ("ly", LRU_W)):
        s[name] = w_in[:, off:off + size]
        off += size
    gates = jnp.concatenate([s["gb"], s["ga"], s["mi"], s["mf"]], axis=1)
    gates = jnp.pad(gates, ((0, 0), (0, DK - 4 * HEADS)))
    return jnp.concatenate([s["qkv"], s["z"], s["mq"], s["mk"], s["mv"], s["mo"], s["lx"], s["ly"], gates], axis=1)


def _block_diag(w):
    nb, d, e = w.shape
    eye = jnp.eye(nb, dtype=w.dtype)
    return (w[:, :, None, :] * eye[:, None, :, None]).reshape(nb * d, nb * e)


def _head_consts(*vecs):
    rows = [jnp.broadcast_to(v[:, None, None], (HEADS, 1, DK)) for v in vecs]
    rows.append(jnp.zeros((HEADS, 8 - len(vecs), DK), f32))
    return jnp.concatenate(rows, axis=1)


def kernel(x_prompt, x_sample, state_gdn_S, state_gdn_conv, state_mlstm_C, state_mlstm_n, state_mlstm_m, state_lru_h, state_lru_conv, ffn1_wg, ffn1_wu, ffn1_wd, ln_g, ln_b, w_in, gdn_conv_w, gdn_A_log, gdn_dt_bias, gdn_norm_w, mlstm_i_bias, mlstm_f_bias, mlstm_norm_w, lru_conv_w, lru_conv_b, lru_wr, lru_br, lru_wi, lru_bi, lru_lambda, w_out, ffn2_wg, ffn2_wu, ffn2_wd):
    BP, TP, D = x_prompt.shape
    BS, TS, _ = x_sample.shape
    MP, MS = BP * TP, BS * TS
    x = jnp.concatenate([x_prompt.reshape(MP, D), x_sample.reshape(MS, D)], axis=0)

    outs_p = [[] for _ in range(7)]
    outs_s = [[] for _ in range(7)]
    for d in range(DEPTH):
        x1 = _ffn_ln(x, ffn1_wg[d].astype(bf16), ffn1_wu[d].astype(bf16), ffn1_wd[d].astype(bf16),
                     ln_g[d, 0], ln_b[d, 0])
        U = _inproj(x1, _permute_w_in(w_in[d]).astype(bf16))
        hc = _head_consts(gdn_A_log[d], gdn_dt_bias[d], mlstm_i_bias[d], mlstm_f_bias[d])
        wr = _block_diag(lru_wr[d]).astype(bf16)
        wi = _block_diag(lru_wi[d]).astype(bf16)
        lru_w = (lru_conv_w[d], lru_conv_b[d], wr, lru_br[d], wi, lru_bi[d], lru_lambda[d])

        og_p, gS_p = _gdn_prompt(U, BP, TP, gdn_conv_w[d], hc, gdn_norm_w[d])
        hm_p, mC_p, mn_p, mm_p = _mlstm_prompt(U, BP, TP, hc, mlstm_norm_w[d])
        ol_p, lh_p = _lru_prompt(U, BP, TP, *lru_w)
        Up = U[:MP].reshape(BP, TP, D_INP)
        outs_p[0].append(gS_p)
        outs_p[1].append(Up[:, TP - 3:, 0:CONV_CH])
        outs_p[2].append(mC_p)
        outs_p[3].append(mn_p.reshape(BP, HEADS, DK))
        outs_p[4].append(mm_p[:, :, 0, 0])
        outs_p[5].append(lh_p.reshape(BP, LRU_W))
        outs_p[6].append(Up[:, TP - 3:, COL_LX:COL_LX + LRU_W])

        Us = U[MP:].reshape(BS, TS, D_INP)
        hist = jnp.concatenate([
            state_gdn_conv[d], jnp.zeros((BS, 3, COL_LX - CONV_CH), f32),
            state_lru_conv[d], jnp.zeros((BS, 3, D_INP - COL_LX - LRU_W), f32)], axis=2)
        U8 = jnp.concatenate([jnp.zeros((BS, 1, D_INP), f32), hist, Us], axis=1).reshape(BS * GROUP, D_INP)
        h0g = jnp.zeros((BS, GROUP, LRU_W), f32).at[:, 3].set(state_lru_h[d]).reshape(BS * GROUP, LRU_W)
        og_s, gS_s = _gdn_sample(U8, state_gdn_S[d], gdn_conv_w[d], hc, gdn_norm_w[d])
        hm_s, mC_s, mn_s, mm_s = _mlstm_sample(U8, state_mlstm_C[d], state_mlstm_n[d], state_mlstm_m[d],
                                               hc, mlstm_norm_w[d])
        ol_s, hl_s = _lru_sample(U8, h0g, *lru_w)
        toks = lambda a: a.reshape(BS, GROUP, -1)[:, 4:].reshape(MS, -1)
        outs_s[0].append(gS_s)
        outs_s[1].append(Us[:, TS - 3:, 0:CONV_CH])
        outs_s[2].append(mC_s)
        outs_s[3].append(mn_s.reshape(BS, HEADS, DK))
        outs_s[4].append(mm_s[:, :, 0, 0])
        outs_s[5].append(hl_s.reshape(BS, GROUP, LRU_W)[:, GROUP - 1])
        outs_s[6].append(Us[:, TS - 3:, COL_LX:COL_LX + LRU_W])

        mix = jnp.concatenate([jnp.concatenate([og_p, hm_p, ol_p], axis=1),
                               jnp.concatenate([toks(og_s), toks(hm_s), toks(ol_s)], axis=1)], axis=0)
        x2 = _outproj_ln(mix, x1, w_out[d].astype(bf16), ln_g[d, 1], ln_b[d, 1])
        x = _ffn_ln(x2, ffn2_wg[d].astype(bf16), ffn2_wu[d].astype(bf16), ffn2_wd[d].astype(bf16),
                    ln_g[d, 2], ln_b[d, 2])

    y_prompt = x[:MP].reshape(BP, TP, D)
    y_sample = x[MP:].reshape(BS, TS, D)
    return (y_prompt, y_sample, *[jnp.stack(o) for o in outs_p], *[jnp.stack(o) for o in outs_s])
```

```python
import functools
import math

import jax
import jax.numpy as jnp
from jax import lax
from jax.experimental import pallas as pl
from jax.experimental.pallas import tpu as pltpu

f32 = jnp.float32
bf16 = jnp.bfloat16

DEPTH = 2
D_MODEL = 2048
D_FF = 5632
HEADS = 6
DK = 128
LRU_W = 512
LRU_BLOCKS = 8
LRU_C = 8.0
CONV_W = 4
CHUNK = 64
ALPHA = (2 * DEPTH) ** 0.25
NORM_EPS = 1e-6
NEG = -1e30

QK = HEADS * DK
CONV_CH = 3 * QK
CB_Q, CB_K, CB_V, CB_Z = 0, 6, 12, 18
CB_MQ, CB_MK, CB_MV, CB_MO = 24, 30, 36, 42
CB_LX, CB_LY = 12, 13
CB_GATES = 56
COL_LX = 6144
D_INP = 7296
GROUP = 8
TOK0 = 4
GDN_HEADS_PER_STEP = 2
MLSTM_HEADS_PER_STEP = 3

VMEM_LIMIT = 60 * 1024 * 1024


def _cparams(sem):
    return pltpu.CompilerParams(dimension_semantics=sem, vmem_limit_bytes=VMEM_LIMIT)


def _silu(x):
    return x * jax.nn.sigmoid(x)


def _softplus(x):
    return jnp.maximum(x, 0.0) + jnp.log1p(jnp.exp(-jnp.abs(x)))


def _expm1(x):
    u = jnp.exp(x)
    um1 = u - 1.0
    lg = jnp.where(u == 1.0, 1.0, jnp.log(jnp.where(u == 0.0, 1.0, u)))
    return jnp.where(u == 1.0, x, jnp.where(u == 0.0, -1.0, um1 * x / lg))


def _layernorm(y, g, b):
    mu = jnp.mean(y, -1, keepdims=True)
    d = y - mu
    var = jnp.mean(d * d, -1, keepdims=True)
    return d * lax.rsqrt(var + NORM_EPS) * g + b


def _rms(x):
    return x * lax.rsqrt(jnp.mean(x * x, -1, keepdims=True) + NORM_EPS)


def _l2(x):
    return x * lax.rsqrt(jnp.sum(x * x, -1, keepdims=True) + NORM_EPS)


def _colsel(gates, idx):
    lane = lax.broadcasted_iota(jnp.int32, gates.shape, 1)
    return jnp.sum(jnp.where(lane == idx, gates, 0.0), axis=1, keepdims=True)


def _dot(a, b):
    return jnp.dot(a, b, preferred_element_type=f32)


def _dot_tn(a, b):
    return lax.dot_general(a, b, (((0,), (0,)), ((), ())), preferred_element_type=f32)


def _bmm(a, b):
    return jnp.einsum('nij,njk->nik', a, b, preferred_element_type=f32)


def _bmm_nt(a, b):
    return jnp.einsum('nid,njd->nij', a, b, preferred_element_type=f32)


def _ij(C):
    return (lax.broadcasted_iota(jnp.int32, (C, C), 0), lax.broadcasted_iota(jnp.int32, (C, C), 1))


def _cumsum_row(col3, ii, jj):
    return jnp.sum(jnp.where((ii <= jj)[None], col3, 0.0), axis=1, keepdims=True)


def _row2col(row3, ii, jj):
    return jnp.sum(jnp.where((ii == jj)[None], row3, 0.0), axis=2, keepdims=True)


def _col2row(col3, ii, jj):
    return jnp.sum(jnp.where((ii == jj)[None], col3, 0.0), axis=1, keepdims=True)


def _conv_zero_hist(x, w, xp_ref):
    T = x.shape[0]
    xp_ref[0:8, :] = jnp.zeros((8, x.shape[1]), f32)
    xp_ref[8:T + 8, :] = x
    y = w[3:4] * x
    for k in range(CONV_W - 1):
        y = y + w[k:k + 1] * xp_ref[pl.ds(5 + k, T), :]
    return y


def _conv_groups(x, w_ref):
    y = w_ref[3:4, :] * x
    for s in range(1, CONV_W):
        y = y + w_ref[3 - s:4 - s, :] * pltpu.roll(x, s, axis=0)
    return y


def _group_pos(shape):
    return lax.broadcasted_iota(jnp.int32, shape, 0) & (GROUP - 1)


def _ffn_ln_kernel(x_ref, wg_ref, wu_ref, wd_ref, g_ref, b_ref, o_ref, acc_ref, xb_ref):
    j = pl.program_id(1)

    @pl.when(j == 0)
    def _():
        acc_ref[...] = jnp.zeros_like(acc_ref)
        xb_ref[...] = x_ref[...].astype(bf16)

    xb = xb_ref[...]
    hg = _dot(xb, wg_ref[...])
    hu = _dot(xb, wu_ref[...])
    h = (_silu(hg) * hu).astype(bf16)
    acc_ref[...] += _dot(h, wd_ref[...])

    @pl.when(j == pl.num_programs(1) - 1)
    def _():
        y = ALPHA * x_ref[...] + 0.5 * acc_ref[...]
        o_ref[...] = _layernorm(y, g_ref[...], b_ref[...])


def _ffn_ln(x, wg, wu, wd, g, b, *, tm=512, tf=512):
    M, D = x.shape
    F = wg.shape[1]
    return pl.pallas_call(
        _ffn_ln_kernel,
        grid=(M // tm, F // tf),
        in_specs=[
            pl.BlockSpec((tm, D), lambda i, j: (i, 0)),
            pl.BlockSpec((D, tf), lambda i, j: (0, j)),
            pl.BlockSpec((D, tf), lambda i, j: (0, j)),
            pl.BlockSpec((tf, D), lambda i, j: (j, 0)),
            pl.BlockSpec((1, D), lambda i, j: (0, 0)),
            pl.BlockSpec((1, D), lambda i, j: (0, 0)),
        ],
        out_specs=pl.BlockSpec((tm, D), lambda i, j: (i, 0)),
        out_shape=jax.ShapeDtypeStruct((M, D), f32),
        scratch_shapes=[pltpu.VMEM((tm, D), f32), pltpu.VMEM((tm, D), bf16)],
        compiler_params=_cparams(("parallel", "arbitrary")),
        name="ffn_ln",
    )(x, wg, wu, wd, g.reshape(1, D), b.reshape(1, D))


def _inproj_kernel(x_ref, w_ref, o_ref):
    o_ref[...] = _dot(x_ref[...].astype(bf16), w_ref[...])


def _inproj(x, w, *, tm=512, tn=2432):
    M, D = x.shape
    N = w.shape[1]
    return pl.pallas_call(
        _inproj_kernel,
        grid=(N // tn, M // tm),
        in_specs=[pl.BlockSpec((tm, D), lambda n, i: (i, 0)),
                  pl.BlockSpec((D, tn), lambda n, i: (0, n))],
        out_specs=pl.BlockSpec((tm, tn), lambda n, i: (i, n)),
        out_shape=jax.ShapeDtypeStruct((M, N), f32),
        compiler_params=_cparams(("parallel", "arbitrary")),
        name="in_proj",
    )(x, w)


def _outproj_ln_kernel(mix_ref, x_ref, w_ref, g_ref, b_ref, o_ref):
    y = ALPHA * x_ref[...] + _dot(mix_ref[...], w_ref[...])
    o_ref[...] = _layernorm(y, g_ref[...], b_ref[...])


def _outproj_ln(mix, x, w, g, b, *, tm=512):
    M, D = x.shape
    return pl.pallas_call(
        _outproj_ln_kernel,
        grid=(M // tm,),
        in_specs=[pl.BlockSpec((tm, D), lambda i: (i, 0)),
                  pl.BlockSpec((tm, D), lambda i: (i, 0)),
                  pl.BlockSpec((D, D), lambda i: (0, 0)),
                  pl.BlockSpec((1, D), lambda i: (0, 0)),
                  pl.BlockSpec((1, D), lambda i: (0, 0))],
        out_specs=pl.BlockSpec((tm, D), lambda i: (i, 0)),
        out_shape=jax.ShapeDtypeStruct((M, D), f32),
        compiler_params=_cparams(("parallel",)),
        name="out_proj_ln",
    )(mix, x, w, g.reshape(1, D), b.reshape(1, D))


def _gdn_prompt_kernel(q_ref, k_ref, v_ref, z_ref, gt_ref, wq_ref, wk_ref, wv_ref, hc_ref, nw_ref,
                       o_ref, s_ref,
                       xp_ref, uv_ref, wks_ref, qk_ref, qd_ref, kt_ref, gtot_ref, *, C, G):
    hg = pl.program_id(1)
    T = q_ref.shape[0]
    N = T // C
    gates = gt_ref[...]
    ii, jj = _ij(C)
    for g in range(G):
        h = hg * G + g
        sl = slice(g * DK, (g + 1) * DK)
        hc = hc_ref[h]
        q = _l2(_silu(_conv_zero_hist(q_ref[:, sl], wq_ref[:, sl], xp_ref))) * (DK ** -0.5)
        k = _l2(_silu(_conv_zero_hist(k_ref[:, sl], wk_ref[:, sl], xp_ref)))
        v = _silu(_conv_zero_hist(v_ref[:, sl], wv_ref[:, sl], xp_ref))
        beta = jax.nn.sigmoid(_colsel(gates, h))
        gl = -jnp.exp(hc[0:1, 0:1]) * _softplus(_colsel(gates, HEADS + h) + hc[1:2, 0:1])

        q3, k3, v3 = q.reshape(N, C, DK), k.reshape(N, C, DK), v.reshape(N, C, DK)
        g3, b3 = gl.reshape(N, C, 1), beta.reshape(N, C, 1)
        g_row = _cumsum_row(g3, ii, jj)
        g_col = _row2col(g_row, ii, jj)
        decay = jnp.exp(jnp.where((ii >= jj)[None], g_col - g_row, NEG))
        kb = k3.astype(bf16)
        kk = _bmm_nt(kb, kb)
        p = jnp.where((ii > jj)[None], -(b3 * kk * decay), 0.0)
        e = p
        for _ in range(int(math.log2(C)) - 1):
            pb = p.astype(bf16)
            p = _bmm(pb, pb)
            e = e + p + _bmm(e.astype(bf16), p.astype(bf16))
        eb = e.astype(bf16)
        e_g = jnp.exp(g_col)
        rv = b3 * v3
        rk = (b3 * e_g) * k3
        uv_ref[g] = rv + _bmm(eb, rv.astype(bf16))
        wks_ref[g] = (rk + _bmm(eb, rk.astype(bf16))).astype(bf16)
        qk_ref[g] = (_bmm_nt(q3.astype(bf16), kb) * decay).astype(bf16)
        g_last = g_row[:, :, C - 1:C]
        qd_ref[g] = (q3 * e_g).astype(bf16)
        kt_ref[g] = (k3 * jnp.exp(g_last - g_col)).astype(bf16)
        gtot_ref[g] = jnp.broadcast_to(jnp.exp(g_last), (N, 1, DK))

    nw = nw_ref[...]

    def step(n, Ss):
        rows = pl.ds(pl.multiple_of(n * C, C), C)
        new = []
        for g in range(G):
            sl = slice(g * DK, (g + 1) * DK)
            S = Ss[g]
            Sb = S.astype(bf16)
            U = uv_ref[g, n] - _dot(wks_ref[g, n], Sb)
            Ub = U.astype(bf16)
            o = _dot(qd_ref[g, n], Sb) + _dot(qk_ref[g, n], Ub)
            o_ref[rows, sl] = (_rms(o) * nw * _silu(z_ref[rows, sl])).astype(o_ref.dtype)
            new.append(gtot_ref[g, n] * S + _dot_tn(kt_ref[g, n], Ub))
        return tuple(new)

    Ss = lax.fori_loop(0, N, step, tuple(jnp.zeros((DK, DK), f32) for _ in range(G)))
    for g in range(G):
        s_ref[g] = Ss[g]


def _gdn_prompt(U, B, T, conv_w, hc, norm_w, *, C=CHUNK, G=GDN_HEADS_PER_STEP):
    N = T // C
    W = G * DK
    tok = lambda cb: pl.BlockSpec((T, W), lambda b, h: (b, cb // G + h))
    cw = lambda cb: pl.BlockSpec((CONV_W, W), lambda b, h: (0, cb // G + h))
    return pl.pallas_call(
        functools.partial(_gdn_prompt_kernel, C=C, G=G),
        grid=(B, HEADS // G),
        in_specs=[tok(CB_Q), tok(CB_K), tok(CB_V), tok(CB_Z),
                  pl.BlockSpec((T, DK), lambda b, h: (b, CB_GATES)),
                  cw(CB_Q), cw(CB_K), cw(CB_V),
                  pl.BlockSpec((HEADS, 8, DK), lambda b, h: (0, 0, 0)),
                  pl.BlockSpec((1, DK), lambda b, h: (0, 0))],
        out_specs=[pl.BlockSpec((T, W), lambda b, h: (b, h)),
                   pl.BlockSpec((None, G, DK, DK), lambda b, h: (b, h, 0, 0))],
        out_shape=[jax.ShapeDtypeStruct((B * T, QK), bf16),
                   jax.ShapeDtypeStruct((B, HEADS, DK, DK), f32)],
        scratch_shapes=[pltpu.VMEM((T + 8, DK), f32),
                        pltpu.VMEM((G, N, C, DK), f32), pltpu.VMEM((G, N, C, DK), bf16),
                        pltpu.VMEM((G, N, C, C), bf16), pltpu.VMEM((G, N, C, DK), bf16),
                        pltpu.VMEM((G, N, C, DK), bf16), pltpu.VMEM((G, N, 1, DK), f32)],
        compiler_params=_cparams(("parallel", "parallel")),
        name="gdn_prompt",
    )(U, U, U, U, U, conv_w, conv_w, conv_w, hc, norm_w.reshape(1, DK))


def _mlstm_prompt_kernel(q_ref, k_ref, v_ref, og_ref, gt_ref, hc_ref, nw_ref,
                         h_ref, c_ref, n_ref, m_ref,
                         d_ref, qk_ref, dm_ref, b_ref, i_ref, *, C, G):
    hg = pl.program_id(1)
    T = q_ref.shape[0]
    N = T // C
    scale = DK ** -0.5
    gates = gt_ref[...]
    ii, jj = _ij(C)
    for g in range(G):
        h = hg * G + g
        sl = slice(g * DK, (g + 1) * DK)
        hc = hc_ref[h]
        i_pre = _colsel(gates, 2 * HEADS + h) + hc[2:3, 0:1]
        logf = -_softplus(-(_colsel(gates, 3 * HEADS + h) + hc[3:4, 0:1]))
        i3, f3 = i_pre.reshape(N, C, 1), logf.reshape(N, C, 1)
        b_row = _cumsum_row(f3, ii, jj)
        b_col = _row2col(b_row, ii, jj)
        i_row = _col2row(i3, ii, jj)
        D = jnp.where((ii >= jj)[None], b_col - b_row + i_row, NEG)
        d_ref[g] = D
        dm_ref[g] = jnp.max(D, axis=2, keepdims=True)
        b_ref[g] = b_col
        i_ref[g] = i3
        q3 = q_ref[:, sl].reshape(N, C, DK).astype(bf16)
        k3 = (k_ref[:, sl] * scale).reshape(N, C, DK).astype(bf16)
        qk_ref[g] = _bmm_nt(q3, k3)

    def step(n, carry):
        rows = pl.ds(pl.multiple_of(n * C, C), C)
        new = []
        for g in range(G):
            sl = slice(g * DK, (g + 1) * DK)
            Cm, nr, m = carry[g]
            qn = q_ref[rows, sl]
            kn = k_ref[rows, sl] * scale
            vb = v_ref[rows, sl].astype(bf16)
            bn = b_ref[g, n]
            m_t = jnp.maximum(bn + m, dm_ref[g, n])
            inter = jnp.exp(bn + m - m_t)
            Sw = jnp.exp(d_ref[g, n] - m_t) * qk_ref[g, n]
            num = inter * _dot(qn.astype(bf16), Cm.astype(bf16)) + _dot(Sw.astype(bf16), vb)
            den = inter * jnp.sum(qn * nr, -1, keepdims=True) + jnp.sum(Sw, -1, keepdims=True)
            hh = num / jnp.maximum(jnp.abs(den), jnp.exp(-m_t))
            h_ref[rows, sl] = (_rms(hh) * nw_ref[hg * G + g] * jax.nn.sigmoid(og_ref[rows, sl])).astype(h_ref.dtype)
            m_new = m_t[C - 1:C]
            b_last = bn[C - 1:C]
            kw = jnp.exp(b_last - bn + i_ref[g, n] - m_new) * kn
            dec = jnp.exp(b_last + m - m_new)
            new.append((dec * Cm + _dot_tn(kw.astype(bf16), vb),
                        dec * nr + jnp.sum(kw, axis=0, keepdims=True), m_new))
        return tuple(new)

    init = tuple((jnp.zeros((DK, DK), f32), jnp.zeros((1, DK), f32), jnp.zeros((1, 1), f32)) for _ in range(G))
    fin = lax.fori_loop(0, N, step, init)
    for g in range(G):
        c_ref[g] = fin[g][0]
        n_ref[g] = fin[g][1]
        m_ref[g] = jnp.broadcast_to(fin[g][2], (1, DK))


def _mlstm_prompt(U, B, T, hc, norm_w, *, C=CHUNK, G=MLSTM_HEADS_PER_STEP):
    N = T // C
    W = G * DK
    tok = lambda cb: pl.BlockSpec((T, W), lambda b, h: (b, cb // G + h))
    st = lambda r: pl.BlockSpec((None, G, r, DK), lambda b, h: (b, h, 0, 0))
    return pl.pallas_call(
        functools.partial(_mlstm_prompt_kernel, C=C, G=G),
        grid=(B, HEADS // G),
        in_specs=[tok(CB_MQ), tok(CB_MK), tok(CB_MV), tok(CB_MO),
                  pl.BlockSpec((T, DK), lambda b, h: (b, CB_GATES)),
                  pl.BlockSpec((HEADS, 8, DK), lambda b, h: (0, 0, 0)),
                  pl.BlockSpec((HEADS, 1, DK), lambda b, h: (0, 0, 0))],
        out_specs=[pl.BlockSpec((T, W), lambda b, h: (b, h)), st(DK), st(1), st(1)],
        out_shape=[jax.ShapeDtypeStruct((B * T, QK), bf16),
                   jax.ShapeDtypeStruct((B, HEADS, DK, DK), f32),
                   jax.ShapeDtypeStruct((B, HEADS, 1, DK), f32),
                   jax.ShapeDtypeStruct((B, HEADS, 1, DK), f32)],
        scratch_shapes=[pltpu.VMEM((G, N, C, C), f32), pltpu.VMEM((G, N, C, C), f32),
                        pltpu.VMEM((G, N, C, 1), f32), pltpu.VMEM((G, N, C, 1), f32),
                        pltpu.VMEM((G, N, C, 1), f32)],
        compiler_params=_cparams(("parallel", "parallel")),
        name="mlstm_prompt",
    )(U, U, U, U, U, hc, norm_w.reshape(HEADS, 1, DK))


def _lru_gates(c, wr_ref, br_ref, wi_ref, bi_ref, lam_ref):
    cb = c.astype(bf16)
    r = jax.nn.sigmoid(_dot(cb, wr_ref[...]) + br_ref[...])
    ig = jax.nn.sigmoid(_dot(cb, wi_ref[...]) + bi_ref[...])
    log_a = -LRU_C * r * _softplus(-lam_ref[...])
    a = jnp.exp(log_a)
    u = jnp.sqrt(-_expm1(2.0 * log_a)) * (ig * c)
    return a, u


def _lru_prompt_kernel(x_ref, y_ref, cw_ref, cb_ref, wr_ref, br_ref, wi_ref, bi_ref, lam_ref,
                       o_ref, hl_ref, xp_ref, tail_ref, hc_ref):
    tc = pl.program_id(1)
    Tc, W = x_ref.shape

    @pl.when(tc == 0)
    def _():
        tail_ref[...] = jnp.zeros_like(tail_ref)
        hc_ref[...] = jnp.zeros_like(hc_ref)

    x = x_ref[...]
    xp_ref[0:8, :] = tail_ref[...]
    xp_ref[8:Tc + 8, :] = x
    c = cw_ref[3:4, :] * x + cb_ref[...]
    for k in range(CONV_W - 1):
        c = c + cw_ref[k:k + 1, :] * xp_ref[pl.ds(5 + k, Tc), :]
    tail_ref[...] = x[Tc - 8:Tc]
    a, u = _lru_gates(c, wr_ref, br_ref, wi_ref, bi_ref, lam_ref)
    row = lax.broadcasted_iota(jnp.int32, (Tc, W), 0)
    d = 1
    while d < Tc:
        keep = row >= d
        a_sh = jnp.where(keep, pltpu.roll(a, d, axis=0), 1.0)
        u_sh = jnp.where(keep, pltpu.roll(u, d, axis=0), 0.0)
        u = a * u_sh + u
        a = a * a_sh
        d *= 2
    hs = u + a * hc_ref[0:1, :]
    last = hs[Tc - 1:Tc]
    hc_ref[...] = jnp.broadcast_to(last, hc_ref.shape)
    hl_ref[...] = last
    o_ref[...] = (hs * jax.nn.gelu(y_ref[...])).astype(o_ref.dtype)


def _lru_prompt(U, B, T, cw, cb, wr, br, wi, bi, lam, *, Tc=512):
    W = LRU_W
    nt = T // Tc
    vec = lambda: pl.BlockSpec((1, W), lambda b, t: (0, 0))
    return pl.pallas_call(
        _lru_prompt_kernel,
        grid=(B, nt),
        in_specs=[pl.BlockSpec((Tc, W), lambda b, t: (b * nt + t, CB_LX)),
                  pl.BlockSpec((Tc, W), lambda b, t: (b * nt + t, CB_LY)),
                  pl.BlockSpec((CONV_W, W), lambda b, t: (0, 0)), vec(),
                  pl.BlockSpec((W, W), lambda b, t: (0, 0)), vec(),
                  pl.BlockSpec((W, W), lambda b, t: (0, 0)), vec(), vec()],
        out_specs=[pl.BlockSpec((Tc, W), lambda b, t: (b * nt + t, 0)),
                   pl.BlockSpec((None, 1, W), lambda b, t: (b, 0, 0))],
        out_shape=[jax.ShapeDtypeStruct((B * T, W), bf16),
                   jax.ShapeDtypeStruct((B, 1, W), f32)],
        scratch_shapes=[pltpu.VMEM((Tc + 8, W), f32), pltpu.VMEM((8, W), f32), pltpu.VMEM((8, W), f32)],
        compiler_params=_cparams(("parallel", "arbitrary")),
        name="lru_prompt",
    )(U, U, cw, cb.reshape(1, W), wr, br.reshape(1, W), wi, bi.reshape(1, W), lam.reshape(1, W))


def _gdn_sample_kernel(q_ref, k_ref, v_ref, z_ref, gt_ref, wq_ref, wk_ref, wv_ref, hc_ref, nw_ref, s0_ref,
                       o_ref, s_ref, raw_ref):
    h = pl.program_id(0)
    BB = s0_ref.shape[0]
    hc = hc_ref[h]
    q = _l2(_silu(_conv_groups(q_ref[...], wq_ref))) * (DK ** -0.5)
    k = _l2(_silu(_conv_groups(k_ref[...], wk_ref)))
    v = _silu(_conv_groups(v_ref[...], wv_ref))
    gates = gt_ref[...]
    beta = jax.nn.sigmoid(_colsel(gates, h))
    gl = -jnp.exp(hc[0:1, 0:1]) * _softplus(_colsel(gates, HEADS + h) + hc[1:2, 0:1])
    eg = jnp.broadcast_to(jnp.exp(gl), v.shape)
    be = beta * eg
    vb = beta * v
    qT, kT = q.T, k.T
    raw_ref[...] = jnp.zeros_like(raw_ref)
    for b in range(BB):
        S = s0_ref[b]
        for t in range(GROUP - TOK0):
            r = b * GROUP + TOK0 + t
            kc = kT[:, r:r + 1]
            kS = jnp.sum(kc * S, axis=0, keepdims=True)
            S = eg[r:r + 1] * S + kc * (vb[r:r + 1] - be[r:r + 1] * kS)
            raw_ref[r:r + 1, :] = jnp.sum(qT[:, r:r + 1] * S, axis=0, keepdims=True)
        s_ref[b] = S
    o_ref[...] = (_rms(raw_ref[...]) * nw_ref[...] * _silu(z_ref[...])).astype(o_ref.dtype)


def _gdn_sample(U8, S0, conv_w, hc, norm_w, *, BB=16):
    B = S0.shape[0]
    R = BB * GROUP
    tok = lambda cb: pl.BlockSpec((R, DK), lambda h, b: (b, cb + h))
    cw = lambda cb: pl.BlockSpec((CONV_W, DK), lambda h, b: (0, cb + h))
    st = pl.BlockSpec((BB, None, DK, DK), lambda h, b: (b, h, 0, 0))
    return pl.pallas_call(
        _gdn_sample_kernel,
        grid=(HEADS, B // BB),
        in_specs=[tok(CB_Q), tok(CB_K), tok(CB_V), tok(CB_Z),
                  pl.BlockSpec((R, DK), lambda h, b: (b, CB_GATES)),
                  cw(CB_Q), cw(CB_K), cw(CB_V),
                  pl.BlockSpec((HEADS, 8, DK), lambda h, b: (0, 0, 0)),
                  pl.BlockSpec((1, DK), lambda h, b: (0, 0)), st],
        out_specs=[pl.BlockSpec((R, DK), lambda h, b: (b, h)), st],
        out_shape=[jax.ShapeDtypeStruct((B * GROUP, QK), bf16),
                   jax.ShapeDtypeStruct((B, HEADS, DK, DK), f32)],
        scratch_shapes=[pltpu.VMEM((R, DK), f32)],
        compiler_params=_cparams(("parallel", "parallel")),
        name="gdn_sample",
    )(U8, U8, U8, U8, U8, conv_w, conv_w, conv_w, hc, norm_w.reshape(1, DK), S0)


def _mlstm_sample_kernel(q_ref, k_ref, v_ref, og_ref, gt_ref, hc_ref, nw_ref, c0_ref, n0_ref, m0_ref,
                         h_ref, c_ref, n_ref, m_ref, raw_ref):
    h = pl.program_id(0)
    BB = c0_ref.shape[0]
    hc = hc_ref[h]
    q = q_ref[...]
    k = k_ref[...] * (DK ** -0.5)
    v = v_ref[...]
    gates = gt_ref[...]
    i_pre = _colsel(gates, 2 * HEADS + h) + hc[2:3, 0:1]
    logf = -_softplus(-(_colsel(gates, 3 * HEADS + h) + hc[3:4, 0:1]))
    pos = _group_pos(q.shape)
    m = m0_ref[...]
    for t in range(GROUP - TOK0):
        m = jnp.where(pos == TOK0 + t, jnp.maximum(logf + pltpu.roll(m, 1, axis=0), i_pre), m)
    fd = jnp.exp(logf + pltpu.roll(m, 1, axis=0) - m)
    ig = jnp.exp(i_pre - m)
    ks = ig * k
    n = n0_ref[...]
    for t in range(GROUP - TOK0):
        n = jnp.where(pos == TOK0 + t, fd * pltpu.roll(n, 1, axis=0) + ks, n)
    den = jnp.sum(q * n, axis=1, keepdims=True)
    inv = 1.0 / jnp.maximum(jnp.abs(den), jnp.exp(-m))
    qT, kT = q.T, ks.T
    raw_ref[...] = jnp.zeros_like(raw_ref)
    for b in range(BB):
        Cm = c0_ref[b]
        for t in range(GROUP - TOK0):
            r = b * GROUP + TOK0 + t
            Cm = fd[r:r + 1] * Cm + kT[:, r:r + 1] * v[r:r + 1]
            raw_ref[r:r + 1, :] = jnp.sum(qT[:, r:r + 1] * Cm, axis=0, keepdims=True)
        c_ref[b] = Cm
    n_ref[...] = n
    m_ref[...] = m
    h_ref[...] = (_rms(raw_ref[...] * inv) * nw_ref[h] * jax.nn.sigmoid(og_ref[...])).astype(h_ref.dtype)


def _mlstm_sample(U8, C0, n0g, m0g, hc, norm_w, *, BB=16):
    B = C0.shape[0]
    R = BB * GROUP
    tok = lambda cb: pl.BlockSpec((R, DK), lambda h, b: (b, cb + h))
    row = pl.BlockSpec((R, DK), lambda h, b: (b, h))
    st = pl.BlockSpec((BB, None, DK, DK), lambda h, b: (b, h, 0, 0))
    return pl.pallas_call(
        _mlstm_sample_kernel,
        grid=(HEADS, B // BB),
        in_specs=[tok(CB_MQ), tok(CB_MK), tok(CB_MV), tok(CB_MO),
                  pl.BlockSpec((R, DK), lambda h, b: (b, CB_GATES)),
                  pl.BlockSpec((HEADS, 8, DK), lambda h, b: (0, 0, 0)),
                  pl.BlockSpec((HEADS, 1, DK), lambda h, b: (0, 0, 0)),
                  st, row, row],
        out_specs=[row, st, row, row],
        out_shape=[jax.ShapeDtypeStruct((B * GROUP, QK), bf16),
                   jax.ShapeDtypeStruct((B, HEADS, DK, DK), f32),
                   jax.ShapeDtypeStruct((B * GROUP, QK), f32),
                   jax.ShapeDtypeStruct((B * GROUP, QK), f32)],
        scratch_shapes=[pltpu.VMEM((R, DK), f32)],
        compiler_params=_cparams(("parallel", "parallel")),
        name="mlstm_sample",
    )(U8, U8, U8, U8, U8, hc, norm_w.reshape(HEADS, 1, DK), C0, n0g, m0g)


def _lru_sample_kernel(x_ref, y_ref, h0_ref, cw_ref, cb_ref, wr_ref, br_ref, wi_ref, bi_ref, lam_ref,
                       o_ref, h_ref):
    c = _conv_groups(x_ref[...], cw_ref) + cb_ref[...]
    a, u = _lru_gates(c, wr_ref, br_ref, wi_ref, bi_ref, lam_ref)
    hs = h0_ref[...]
    pos = _group_pos(hs.shape)
    for t in range(GROUP - TOK0):
        hs = jnp.where(pos == TOK0 + t, a * pltpu.roll(hs, 1, axis=0) + u, hs)
    h_ref[...] = hs
    o_ref[...] = (hs * jax.nn.gelu(y_ref[...])).astype(o_ref.dtype)


def _lru_sample(U8, h0g, cw, cb, wr, br, wi, bi, lam, *, R=128):
    W = LRU_W
    rows = U8.shape[0]
    vec = lambda: pl.BlockSpec((1, W), lambda i: (0, 0))
    return pl.pallas_call(
        _lru_sample_kernel,
        grid=(rows // R,),
        in_specs=[pl.BlockSpec((R, W), lambda i: (i, CB_LX)),
                  pl.BlockSpec((R, W), lambda i: (i, CB_LY)),
                  pl.BlockSpec((R, W), lambda i: (i, 0)),
                  pl.BlockSpec((CONV_W, W), lambda i: (0, 0)), vec(),
                  pl.BlockSpec((W, W), lambda i: (0, 0)), vec(),
                  pl.BlockSpec((W, W), lambda i: (0, 0)), vec(), vec()],
        out_specs=[pl.BlockSpec((R, W), lambda i: (i, 0)), pl.BlockSpec((R, W), lambda i: (i, 0))],
        out_shape=[jax.ShapeDtypeStruct((rows, W), bf16), jax.ShapeDtypeStruct((rows, W), f32)],
        compiler_params=_cparams(("parallel",)),
        name="lru_sample",
    )(U8, U8, h0g, cw, cb.reshape(1, W), wr, br.reshape(1, W), wi, bi.reshape(1, W), lam.reshape(1, W))


def _permute_w_in(w_in):
    s = {}
    off = 0
    for name, size in (("qkv", CONV_CH), ("z", QK), ("gb", HEADS), ("ga", HEADS), ("mq", QK), ("mk", QK),
                       ("mv", QK), ("mo", QK), ("mi", HEADS), ("mf", HEADS), ("lx", LRU_W), ("ly", LRU_W)):
        s[name] = w_in[:, off:off + size]
        off += size
    gates = jnp.concatenate([s["gb"], s["ga"], s["mi"], s["mf"]], axis=1)
    gates = jnp.pad(gates, ((0, 0), (0, DK - 4 * HEADS)))
    return jnp.concatenate([s["qkv"], s["z"], s["mq"], s["mk"], s["mv"], s["mo"], s["lx"], s["ly"], gates], axis=1)


def _block_diag(w):
    nb, d, e = w.shape
    eye = jnp.eye(nb, dtype=w.dtype)
    return (w[:, :, None, :] * eye[:, None, :, None]).reshape(nb * d, nb * e)


def _head_consts(*vecs):
    rows = [jnp.broadcast_to(v[:, None, None], (HEADS, 1, DK)) for v in vecs]
    rows.append(jnp.zeros((HEADS, 8 - len(vecs), DK), f32))
    return jnp.concatenate(rows, axis=1)


def _state_row(state):
    B, W = state.shape
    return jnp.pad(state[:, None, :], ((0, 0), (TOK0 - 1, GROUP - TOK0), (0, 0))).reshape(B * GROUP, W)


def _tail_rows(U, B, T, col, width):
    return jnp.stack([lax.slice(U, ((b + 1) * T - (CONV_W - 1), col), ((b + 1) * T, col + width))
                      for b in range(B)])


def kernel(x_prompt, x_sample, state_gdn_S, state_gdn_conv, state_mlstm_C, state_mlstm_n, state_mlstm_m, state_lru_h, state_lru_conv, ffn1_wg, ffn1_wu, ffn1_wd, ln_g, ln_b, w_in, gdn_conv_w, gdn_A_log, gdn_dt_bias, gdn_norm_w, mlstm_i_bias, mlstm_f_bias, mlstm_norm_w, lru_conv_w, lru_conv_b, lru_wr, lru_br, lru_wi, lru_bi, lru_lambda, w_out, ffn2_wg, ffn2_wu, ffn2_wd):
    BP, TP, D = x_prompt.shape
    BS, TS, _ = x_sample.shape
    MP, MS = BP * TP, BS * TS
    x = jnp.concatenate([x_prompt.reshape(MP, D), x_sample.reshape(MS, D)], axis=0)

    outs_p = [[] for _ in range(7)]
    outs_s = [[] for _ in range(7)]
    for d in range(DEPTH):
        x1 = _ffn_ln(x, ffn1_wg[d].astype(bf16), ffn1_wu[d].astype(bf16), ffn1_wd[d].astype(bf16),
                     ln_g[d, 0], ln_b[d, 0])
        U = _inproj(x1, _permute_w_in(w_in[d]).astype(bf16))
        hc = _head_consts(gdn_A_log[d], gdn_dt_bias[d], mlstm_i_bias[d], mlstm_f_bias[d])
        wr = _block_diag(lru_wr[d]).astype(bf16)
        wi = _block_diag(lru_wi[d]).astype(bf16)
        lru_w = (lru_conv_w[d], lru_conv_b[d], wr, lru_br[d], wi, lru_bi[d], lru_lambda[d])

        og_p, gS_p = _gdn_prompt(U, BP, TP, gdn_conv_w[d], hc, gdn_norm_w[d])
        hm_p, mC_p, mn_p, mm_p = _mlstm_prompt(U, BP, TP, hc, mlstm_norm_w[d])
        ol_p, lh_p = _lru_prompt(U, BP, TP, *lru_w)
        outs_p[0].append(gS_p)
        outs_p[1].append(_tail_rows(U, BP, TP, 0, CONV_CH))
        outs_p[2].append(mC_p)
        outs_p[3].append(mn_p.reshape(BP, HEADS, DK))
        outs_p[4].append(mm_p[:, :, 0, 0])
        outs_p[5].append(lh_p.reshape(BP, LRU_W))
        outs_p[6].append(_tail_rows(U, BP, TP, COL_LX, LRU_W))

        Us = lax.slice(U, (MP, 0), (MP + MS, D_INP)).reshape(BS, TS, D_INP)
        hist = jnp.concatenate([
            state_gdn_conv[d], jnp.zeros((BS, 3, COL_LX - CONV_CH), f32),
            state_lru_conv[d], jnp.zeros((BS, 3, D_INP - COL_LX - LRU_W), f32)], axis=2)
        U8 = jnp.concatenate([jnp.zeros((BS, 1, D_INP), f32), hist, Us], axis=1).reshape(BS * GROUP, D_INP)
        og_s, gS_s = _gdn_sample(U8, state_gdn_S[d], gdn_conv_w[d], hc, gdn_norm_w[d])
        hm_s, mC_s, n8, m8 = _mlstm_sample(
            U8, state_mlstm_C[d], _state_row(state_mlstm_n[d].reshape(BS, QK)),
            _state_row(jnp.repeat(state_mlstm_m[d], DK, axis=1)), hc, mlstm_norm_w[d])
        ol_s, h8 = _lru_sample(U8, _state_row(state_lru_h[d]), *lru_w)
        toks = lambda a: a.reshape(BS, GROUP, -1)[:, TOK0:].reshape(MS, -1)
        last = lambda a: a.reshape(BS, GROUP, -1)[:, GROUP - 1]
        outs_s[0].append(gS_s)
        outs_s[1].append(Us[:, TS - 3:, 0:CONV_CH])
        outs_s[2].append(mC_s)
        outs_s[3].append(last(n8).reshape(BS, HEADS, DK))
        outs_s[4].append(last(m8).reshape(BS, HEADS, DK)[:, :, 0])
        outs_s[5].append(last(h8))
        outs_s[6].append(Us[:, TS - 3:, COL_LX:COL_LX + LRU_W])

        mix = jnp.concatenate([jnp.concatenate([og_p, hm_p, ol_p], axis=1),
                               jnp.concatenate([toks(og_s), toks(hm_s), toks(ol_s)], axis=1)], axis=0)
        x2 = _outproj_ln(mix, x1, w_out[d].astype(bf16), ln_g[d, 1], ln_b[d, 1])
        x = _ffn_ln(x2, ffn2_wg[d].astype(bf16), ffn2_wu[d].astype(bf16), ffn2_wd[d].astype(bf16),
                    ln_g[d, 2], ln_b[d, 2])

    y_prompt = x[:MP].reshape(BP, TP, D)
    y_sample = x[MP:].reshape(BS, TS, D)
    return (y_prompt, y_sample, *[jnp.stack(o) for o in outs_p], *[jnp.stack(o) for o in outs_s])
```

```python
import functools
import math

import jax
import jax.numpy as jnp
from jax import lax
from jax.experimental import pallas as pl
from jax.experimental.pallas import tpu as pltpu

f32 = jnp.float32
bf16 = jnp.bfloat16

DEPTH = 2
D_MODEL = 2048
D_FF = 5632
HEADS = 6
DK = 128
LRU_W = 512
LRU_BLOCKS = 8
LRU_C = 8.0
CONV_W = 4
CHUNK = 64
ALPHA = (2 * DEPTH) ** 0.25
NORM_EPS = 1e-6
NEG = -1e30

QK = HEADS * DK
CONV_CH = 3 * QK
MIX_W = 4 * QK
CB_Q, CB_K, CB_V, CB_Z = 0, 6, 12, 18
CB_MQ, CB_MK, CB_MV, CB_MO = 24, 30, 36, 42
CB_GDN, CB_MLSTM = 0, 1
CB_LX, CB_LY = 12, 13
CB_GATES = 56
COL_LX = 6144
D_INP = 7296
GROUP = 8
TOK0 = 4
ROW_TILE = 512

VMEM_LIMIT = 60 * 1024 * 1024


def _cparams(sem):
    return pltpu.CompilerParams(dimension_semantics=sem, vmem_limit_bytes=VMEM_LIMIT)


def _silu(x):
    return x * jax.nn.sigmoid(x)


def _softplus(x):
    return jnp.maximum(x, 0.0) + jnp.log1p(jnp.exp(-jnp.abs(x)))


def _expm1(x):
    u = jnp.exp(x)
    um1 = u - 1.0
    lg = jnp.where(u == 1.0, 1.0, jnp.log(jnp.where(u == 0.0, 1.0, u)))
    return jnp.where(u == 1.0, x, jnp.where(u == 0.0, -1.0, um1 * x / lg))


def _layernorm(y, g, b):
    mu = jnp.mean(y, -1, keepdims=True)
    d = y - mu
    var = jnp.mean(d * d, -1, keepdims=True)
    return d * lax.rsqrt(var + NORM_EPS) * g + b


def _rms(x):
    return x * lax.rsqrt(jnp.mean(x * x, -1, keepdims=True) + NORM_EPS)


def _l2(x):
    return x * lax.rsqrt(jnp.sum(x * x, -1, keepdims=True) + NORM_EPS)


def _colsel(gates, idx):
    lane = lax.broadcasted_iota(jnp.int32, gates.shape, 1)
    return jnp.sum(jnp.where(lane == idx, gates, 0.0), axis=1, keepdims=True)


def _dot(a, b):
    return jnp.dot(a, b, preferred_element_type=f32)


def _bmm(a, b):
    return jnp.einsum('nij,njk->nik', a, b, preferred_element_type=f32)


def _bmm_nt(a, b):
    return jnp.einsum('nid,njd->nij', a, b, preferred_element_type=f32)


def _bmm_tn(a, b):
    return jnp.einsum('nck,ncv->nkv', a, b, preferred_element_type=f32)


def _ij(C):
    return (lax.broadcasted_iota(jnp.int32, (C, C), 0), lax.broadcasted_iota(jnp.int32, (C, C), 1))


def _cumsum_row(col3, ii, jj):
    return jnp.sum(jnp.where((ii <= jj)[None], col3, 0.0), axis=1, keepdims=True)


def _row2col(row3, ii, jj):
    return jnp.sum(jnp.where((ii == jj)[None], row3, 0.0), axis=2, keepdims=True)


def _col2row(col3, ii, jj):
    return jnp.sum(jnp.where((ii == jj)[None], col3, 0.0), axis=1, keepdims=True)


def _conv_carry(x, w_ref, xp_ref, tail):
    T = x.shape[0]
    xp_ref[0:8, :] = tail
    xp_ref[8:T + 8, :] = x
    y = w_ref[3:4, :] * x
    for k in range(CONV_W - 1):
        y = y + w_ref[k:k + 1, :] * xp_ref[pl.ds(5 + k, T), :]
    return y


def _conv_groups(x, w_ref):
    y = w_ref[3:4, :] * x
    for s in range(1, CONV_W):
        y = y + w_ref[3 - s:4 - s, :] * pltpu.roll(x, s, axis=0)
    return y


def _group_pos(shape):
    return lax.broadcasted_iota(jnp.int32, shape, 0) & (GROUP - 1)


def _ffn_ln_kernel(*refs, n_x, n_o, npb):
    x_refs = refs[:n_x]
    wg_ref, wu_ref, wd_ref, g_ref, b_ref = refs[n_x:n_x + 5]
    o_refs = refs[n_x + 5:n_x + 5 + n_o]
    acc_ref, xb_ref, xf_ref = refs[n_x + 5 + n_o:]
    i = pl.program_id(0)
    j = pl.program_id(1)

    def load(x_ref):
        xf_ref[...] = x_ref[...]
        xb_ref[...] = x_ref[...].astype(bf16)

    @pl.when(j == 0)
    def _():
        acc_ref[...] = jnp.zeros_like(acc_ref)

    if n_x == 1:
        pl.when(j == 0)(lambda: load(x_refs[0]))
    else:
        pl.when((j == 0) & (i < npb))(lambda: load(x_refs[0]))
        pl.when((j == 0) & (i >= npb))(lambda: load(x_refs[1]))

    xb = xb_ref[...]
    hg = _dot(xb, wg_ref[...])
    hu = _dot(xb, wu_ref[...])
    h = (_silu(hg) * hu).astype(bf16)
    acc_ref[...] += _dot(h, wd_ref[...])

    last = j == pl.num_programs(1) - 1

    def store(o_ref):
        y = ALPHA * xf_ref[...] + 0.5 * acc_ref[...]
        o_ref[...] = _layernorm(y, g_ref[...], b_ref[...])

    if n_o == 1:
        pl.when(last)(lambda: store(o_refs[0]))
    else:
        pl.when(last & (i < npb))(lambda: store(o_refs[0]))
        pl.when(last & (i >= npb))(lambda: store(o_refs[1]))


def _ffn_ln(xs, wg, wu, wd, d, g, b, *, split_out=False, tm=ROW_TILE, tf=512):
    xs = xs if isinstance(xs, (tuple, list)) else (xs,)
    D = xs[0].shape[1]
    M = sum(x.shape[0] for x in xs)
    F = wg.shape[2]
    npb = M // tm - 1
    if len(xs) == 2:
        assert xs[0].shape[0] == npb * tm and xs[1].shape[0] == tm
    rows_p = lambda i, j: (jnp.minimum(i, npb - 1), 0)
    rows_s = lambda i, j: (0, 0)
    rows = lambda i, j: (i, 0)
    x_specs = ([pl.BlockSpec((tm, D), rows)] if len(xs) == 1 else
               [pl.BlockSpec((tm, D), rows_p), pl.BlockSpec((tm, D), rows_s)])
    if split_out:
        out_specs = [pl.BlockSpec((tm, D), rows_p), pl.BlockSpec((tm, D), rows_s)]
        out_shape = [jax.ShapeDtypeStruct((npb * tm, D), f32), jax.ShapeDtypeStruct((tm, D), f32)]
    else:
        out_specs = pl.BlockSpec((tm, D), rows)
        out_shape = jax.ShapeDtypeStruct((M, D), f32)
    return pl.pallas_call(
        functools.partial(_ffn_ln_kernel, n_x=len(xs), n_o=2 if split_out else 1, npb=npb),
        grid=(M // tm, F // tf),
        in_specs=x_specs + [
            pl.BlockSpec((None, D, tf), lambda i, j: (d, 0, j)),
            pl.BlockSpec((None, D, tf), lambda i, j: (d, 0, j)),
            pl.BlockSpec((None, tf, D), lambda i, j: (d, j, 0)),
            pl.BlockSpec((1, D), lambda i, j: (0, 0)),
            pl.BlockSpec((1, D), lambda i, j: (0, 0)),
        ],
        out_specs=out_specs,
        out_shape=out_shape,
        scratch_shapes=[pltpu.VMEM((tm, D), f32), pltpu.VMEM((tm, D), bf16), pltpu.VMEM((tm, D), f32)],
        compiler_params=_cparams(("parallel", "arbitrary")),
        name="ffn_ln",
    )(*xs, wg, wu, wd, g.reshape(1, D), b.reshape(1, D))


def _inproj_kernel(x_ref, w_ref, o_ref):
    o_ref[...] = _dot(x_ref[...].astype(bf16), w_ref[...])


def _inproj(x, w, d, *, tm=ROW_TILE, tn=2432):
    M, D = x.shape
    N = w.shape[2]
    return pl.pallas_call(
        _inproj_kernel,
        grid=(N // tn, M // tm),
        in_specs=[pl.BlockSpec((tm, D), lambda n, i: (i, 0)),
                  pl.BlockSpec((None, D, tn), lambda n, i: (d, 0, n))],
        out_specs=pl.BlockSpec((tm, tn), lambda n, i: (i, n)),
        out_shape=jax.ShapeDtypeStruct((M, N), f32),
        compiler_params=_cparams(("parallel", "arbitrary")),
        name="in_proj",
    )(x, w)


def _outproj_ln_kernel(gp_ref, mp_ref, lp_ref, gs_ref, ms_ref, ls_ref, x_ref, w_ref, g_ref, b_ref, o_ref, *, npb):
    i = pl.program_id(0)

    def run(og_ref, hm_ref, ol_ref):
        mix = (_dot(og_ref[...], w_ref[0:QK, :]) + _dot(hm_ref[...], w_ref[QK:2 * QK, :])
               + _dot(ol_ref[...], w_ref[2 * QK:, :]))
        o_ref[...] = _layernorm(ALPHA * x_ref[...] + mix, g_ref[...], b_ref[...])

    pl.when(i < npb)(lambda: run(gp_ref, mp_ref, lp_ref))
    pl.when(i >= npb)(lambda: run(gs_ref, ms_ref, ls_ref))


def _outproj_ln(mix_p, mix_s, x, w, d, g, b, *, tm=ROW_TILE):
    M, D = x.shape
    npb = mix_p[0].shape[0] // tm
    assert mix_s[0].shape[0] == tm and M == (npb + 1) * tm
    rows_p = lambda i: (jnp.minimum(i, npb - 1), 0)
    rows_s = lambda i: (0, 0)
    widths = (QK, QK, LRU_W)
    return pl.pallas_call(
        functools.partial(_outproj_ln_kernel, npb=npb),
        grid=(M // tm,),
        in_specs=[pl.BlockSpec((tm, wd), rows_p) for wd in widths]
        + [pl.BlockSpec((tm, wd), rows_s) for wd in widths]
        + [pl.BlockSpec((tm, D), lambda i: (i, 0)),
           pl.BlockSpec((None, D, D), lambda i: (d, 0, 0)),
           pl.BlockSpec((1, D), lambda i: (0, 0)),
           pl.BlockSpec((1, D), lambda i: (0, 0))],
        out_specs=pl.BlockSpec((tm, D), lambda i: (i, 0)),
        out_shape=jax.ShapeDtypeStruct((M, D), f32),
        compiler_params=_cparams(("parallel",)),
        name="out_proj_ln",
    )(*mix_p, *mix_s, x, w, g.reshape(1, D), b.reshape(1, D))


def _gdn_gates(gates, gc_ref):
    beta = jax.nn.sigmoid(gates)
    gl = -jnp.exp(gc_ref[0:1, :]) * _softplus(gates + gc_ref[1:2, :])
    return beta, gl


def _mlstm_gates(gates, gc_ref):
    i_pre = gates + gc_ref[2:3, :]
    logf = -_softplus(-(gates + gc_ref[3:4, :]))
    return i_pre, logf


def _gdn_prompt_kernel(*refs, C, B):
    mains, gts = refs[0:2 * B:2], refs[1:2 * B:2]
    cw_ref, gc_ref, nw_ref, o_ref, s_ref, xp_ref, tail_ref = refs[2 * B:]
    n = pl.program_id(0)
    NC = B * HEADS

    @pl.when(n == 0)
    def _():
        tail_ref[...] = jnp.zeros_like(tail_ref)
        s_ref[...] = jnp.zeros_like(s_ref)

    qs, ks, vs, zs, bs, gs = [], [], [], [], [], []
    for b in range(B):
        x = mains[b][:, 0:CONV_CH]
        y = _silu(_conv_carry(x, cw_ref, xp_ref, tail_ref[b]))
        tail_ref[b] = x[C - 8:C]
        beta, gl = _gdn_gates(gts[b][...], gc_ref)
        for h in range(HEADS):
            qs.append(y[:, h * DK:(h + 1) * DK])
            ks.append(y[:, QK + h * DK:QK + (h + 1) * DK])
            vs.append(y[:, 2 * QK + h * DK:2 * QK + (h + 1) * DK])
            zs.append(mains[b][:, CONV_CH + h * DK:CONV_CH + (h + 1) * DK])
            bs.append(beta[:, h:h + 1])
            gs.append(gl[:, HEADS + h:HEADS + h + 1])
    q3 = _l2(jnp.stack(qs)) * (DK ** -0.5)
    k3 = _l2(jnp.stack(ks))
    v3 = jnp.stack(vs)
    b3 = jnp.stack(bs)
    g3 = jnp.stack(gs)

    ii, jj = _ij(C)
    g_row = _cumsum_row(g3, ii, jj)
    g_col = _row2col(g_row, ii, jj)
    decay = jnp.exp(jnp.where((ii >= jj)[None], g_col - g_row, NEG))
    kb = k3.astype(bf16)
    kk = _bmm_nt(kb, kb)
    p = jnp.where((ii > jj)[None], -(b3 * kk * decay), 0.0)
    e = p
    for _ in range(int(math.log2(C)) - 1):
        pb = p.astype(bf16)
        p = _bmm(pb, pb)
        e = e + p + _bmm(e.astype(bf16), p.astype(bf16))
    eb = e.astype(bf16)
    e_g = jnp.exp(g_col)
    rv = b3 * v3
    rk = (b3 * e_g) * k3
    uv = rv + _bmm(eb, rv.astype(bf16))
    wks = (rk + _bmm(eb, rk.astype(bf16))).astype(bf16)
    qk = (_bmm_nt(q3.astype(bf16), kb) * decay).astype(bf16)
    g_last = g_row[:, :, C - 1:C]
    qd = (q3 * e_g).astype(bf16)
    kt = (k3 * jnp.exp(g_last - g_col)).astype(bf16)

    S = s_ref[...].reshape(NC, DK, DK)
    Sb = S.astype(bf16)
    Ub = (uv - _bmm(wks, Sb)).astype(bf16)
    o = _bmm(qd, Sb) + _bmm(qk, Ub)
    s_ref[...] = (jnp.exp(g_last) * S + _bmm_tn(kt, Ub)).reshape(B, HEADS, DK, DK)
    o = _rms(o) * nw_ref[...] * _silu(jnp.stack(zs))
    for b in range(B):
        for h in range(HEADS):
            o_ref[b, :, h * DK:(h + 1) * DK] = o[b * HEADS + h].astype(o_ref.dtype)


def _gdn_prompt(U, B, T, conv_w, hc, norm_w, *, C=CHUNK):
    N = T // C
    in_specs = []
    for b in range(B):
        in_specs.append(pl.BlockSpec((C, MIX_W), lambda n, b=b: (b * N + n, CB_GDN)))
        in_specs.append(pl.BlockSpec((C, DK), lambda n, b=b: (b * N + n, CB_GATES)))
    in_specs += [pl.BlockSpec((CONV_W, CONV_CH), lambda n: (0, 0)),
                 pl.BlockSpec((8, DK), lambda n: (0, 0)),
                 pl.BlockSpec((1, DK), lambda n: (0, 0))]
    o, S = pl.pallas_call(
        functools.partial(_gdn_prompt_kernel, C=C, B=B),
        grid=(N,),
        in_specs=in_specs,
        out_specs=[pl.BlockSpec((B, None, C, QK), lambda n: (0, n, 0, 0)),
                   pl.BlockSpec((B, HEADS, DK, DK), lambda n: (0, 0, 0, 0))],
        out_shape=[jax.ShapeDtypeStruct((B, N, C, QK), bf16),
                   jax.ShapeDtypeStruct((B, HEADS, DK, DK), f32)],
        scratch_shapes=[pltpu.VMEM((C + 8, CONV_CH), f32), pltpu.VMEM((B, 8, CONV_CH), f32)],
        compiler_params=_cparams(("arbitrary",)),
        name="gdn_prompt",
    )(*([U, U] * B), conv_w, hc, norm_w.reshape(1, DK))
    return o.reshape(B * T, QK), S


def _mlstm_prompt_kernel(*refs, C, B):
    mains, gts = refs[0:2 * B:2], refs[1:2 * B:2]
    gc_ref, nw_ref, h_ref, c_ref, n_ref, m_ref = refs[2 * B:]
    n = pl.program_id(0)
    NC = B * HEADS

    @pl.when(n == 0)
    def _():
        c_ref[...] = jnp.zeros_like(c_ref)
        n_ref[...] = jnp.zeros_like(n_ref)
        m_ref[...] = jnp.zeros_like(m_ref)

    qs, ks, vs, os_, is_, fs = [], [], [], [], [], []
    for b in range(B):
        i_pre, logf = _mlstm_gates(gts[b][...], gc_ref)
        for h in range(HEADS):
            qs.append(mains[b][:, h * DK:(h + 1) * DK])
            ks.append(mains[b][:, QK + h * DK:QK + (h + 1) * DK])
            vs.append(mains[b][:, 2 * QK + h * DK:2 * QK + (h + 1) * DK])
            os_.append(mains[b][:, 3 * QK + h * DK:3 * QK + (h + 1) * DK])
            is_.append(i_pre[:, 2 * HEADS + h:2 * HEADS + h + 1])
            fs.append(logf[:, 3 * HEADS + h:3 * HEADS + h + 1])
    q3 = jnp.stack(qs)
    k3 = jnp.stack(ks) * (DK ** -0.5)
    vb = jnp.stack(vs).astype(bf16)
    i3 = jnp.stack(is_)
    f3 = jnp.stack(fs)

    ii, jj = _ij(C)
    b_row = _cumsum_row(f3, ii, jj)
    b_col = _row2col(b_row, ii, jj)
    i_row = _col2row(i3, ii, jj)
    D = jnp.where((ii >= jj)[None], b_col - b_row + i_row, NEG)
    d_max = jnp.max(D, axis=2, keepdims=True)
    qb = q3.astype(bf16)
    qk = _bmm_nt(qb, k3.astype(bf16))

    Cm = c_ref[...].reshape(NC, DK, DK)
    nr = n_ref[...].reshape(NC, 1, DK)
    m = m_ref[...].reshape(NC, 1, DK)[:, :, 0:1]
    m_t = jnp.maximum(b_col + m, d_max)
    inter = jnp.exp(b_col + m - m_t)
    Sw = jnp.exp(D - m_t) * qk
    num = inter * _bmm(qb, Cm.astype(bf16)) + _bmm(Sw.astype(bf16), vb)
    den = inter * jnp.sum(q3 * nr, -1, keepdims=True) + jnp.sum(Sw, -1, keepdims=True)
    hh = num / jnp.maximum(jnp.abs(den), jnp.exp(-m_t))
    nw = jnp.stack([nw_ref[h] for _ in range(B) for h in range(HEADS)])
    hh = _rms(hh) * nw * jax.nn.sigmoid(jnp.stack(os_))
    m_new = m_t[:, C - 1:C]
    b_last = b_col[:, C - 1:C]
    kw = jnp.exp(b_last - b_col + i3 - m_new) * k3
    dec = jnp.exp(b_last + m - m_new)
    c_ref[...] = (dec * Cm + _bmm_tn(kw.astype(bf16), vb)).reshape(B, HEADS, DK, DK)
    n_ref[...] = (dec * nr + jnp.sum(kw, axis=1, keepdims=True)).reshape(B, HEADS, 1, DK)
    m_ref[...] = jnp.broadcast_to(m_new, (NC, 1, DK)).reshape(B, HEADS, 1, DK)
    for b in range(B):
        for h in range(HEADS):
            h_ref[b, :, h * DK:(h + 1) * DK] = hh[b * HEADS + h].astype(h_ref.dtype)


def _mlstm_prompt(U, B, T, hc, norm_w, *, C=CHUNK):
    N = T // C
    in_specs = []
    for b in range(B):
        in_specs.append(pl.BlockSpec((C, MIX_W), lambda n, b=b: (b * N + n, CB_MLSTM)))
        in_specs.append(pl.BlockSpec((C, DK), lambda n, b=b: (b * N + n, CB_GATES)))
    in_specs += [pl.BlockSpec((8, DK), lambda n: (0, 0)),
                 pl.BlockSpec((HEADS, 1, DK), lambda n: (0, 0, 0))]
    st = lambda r: pl.BlockSpec((B, HEADS, r, DK), lambda n: (0, 0, 0, 0))
    h, Cm, nr, m = pl.pallas_call(
        functools.partial(_mlstm_prompt_kernel, C=C, B=B),
        grid=(N,),
        in_specs=in_specs,
        out_specs=[pl.BlockSpec((B, None, C, QK), lambda n: (0, n, 0, 0)), st(DK), st(1), st(1)],
        out_shape=[jax.ShapeDtypeStruct((B, N, C, QK), bf16),
                   jax.ShapeDtypeStruct((B, HEADS, DK, DK), f32),
                   jax.ShapeDtypeStruct((B, HEADS, 1, DK), f32),
                   jax.ShapeDtypeStruct((B, HEADS, 1, DK), f32)],
        compiler_params=_cparams(("arbitrary",)),
        name="mlstm_prompt",
    )(*([U, U] * B), hc, norm_w.reshape(HEADS, 1, DK))
    return h.reshape(B * T, QK), Cm, nr, m


def _lru_gates(c, wr_ref, br_ref, wi_ref, bi_ref, lam_ref):
    cb = c.astype(bf16)
    r = jax.nn.sigmoid(_dot(cb, wr_ref[...]) + br_ref[...])
    ig = jax.nn.sigmoid(_dot(cb, wi_ref[...]) + bi_ref[...])
    log_a = -LRU_C * r * _softplus(-lam_ref[...])
    a = jnp.exp(log_a)
    u = jnp.sqrt(-_expm1(2.0 * log_a)) * (ig * c)
    return a, u


def _lru_prompt_kernel(x_ref, y_ref, cw_ref, cb_ref, wr_ref, br_ref, wi_ref, bi_ref, lam_ref,
                       o_ref, hl_ref, xp_ref, tail_ref, hc_ref):
    tc = pl.program_id(1)
    Tc, W = x_ref.shape

    @pl.when(tc == 0)
    def _():
        tail_ref[...] = jnp.zeros_like(tail_ref)
        hc_ref[...] = jnp.zeros_like(hc_ref)

    x = x_ref[...]
    c = _conv_carry(x, cw_ref, xp_ref, tail_ref[...]) + cb_ref[...]
    tail_ref[...] = x[Tc - 8:Tc]
    a, u = _lru_gates(c, wr_ref, br_ref, wi_ref, bi_ref, lam_ref)
    row = lax.broadcasted_iota(jnp.int32, (Tc, W), 0)
    d = 1
    while d < Tc:
        keep = row >= d
        a_sh = jnp.where(keep, pltpu.roll(a, d, axis=0), 1.0)
        u_sh = jnp.where(keep, pltpu.roll(u, d, axis=0), 0.0)
        u = a * u_sh + u
        a = a * a_sh
        d *= 2
    hs = u + a * hc_ref[0:1, :]
    last = hs[Tc - 1:Tc]
    hc_ref[...] = jnp.broadcast_to(last, hc_ref.shape)
    hl_ref[...] = last
    o_ref[...] = (hs * jax.nn.gelu(y_ref[...])).astype(o_ref.dtype)


def _lru_prompt(U, B, T, cw, cb, wr, br, wi, bi, lam, *, Tc=512):
    W = LRU_W
    nt = T // Tc
    vec = lambda: pl.BlockSpec((1, W), lambda b, t: (0, 0))
    return pl.pallas_call(
        _lru_prompt_kernel,
        grid=(B, nt),
        in_specs=[pl.BlockSpec((Tc, W), lambda b, t: (b * nt + t, CB_LX)),
                  pl.BlockSpec((Tc, W), lambda b, t: (b * nt + t, CB_LY)),
                  pl.BlockSpec((CONV_W, W), lambda b, t: (0, 0)), vec(),
                  pl.BlockSpec((W, W), lambda b, t: (0, 0)), vec(),
                  pl.BlockSpec((W, W), lambda b, t: (0, 0)), vec(), vec()],
        out_specs=[pl.BlockSpec((Tc, W), lambda b, t: (b * nt + t, 0)),
                   pl.BlockSpec((None, 1, W), lambda b, t: (b, 0, 0))],
        out_shape=[jax.ShapeDtypeStruct((B * T, W), bf16),
                   jax.ShapeDtypeStruct((B, 1, W), f32)],
        scratch_shapes=[pltpu.VMEM((Tc + 8, W), f32), pltpu.VMEM((8, W), f32), pltpu.VMEM((8, W), f32)],
        compiler_params=_cparams(("parallel", "arbitrary")),
        name="lru_prompt",
    )(U, U, cw, cb.reshape(1, W), wr, br.reshape(1, W), wi, bi.reshape(1, W), lam.reshape(1, W))


def _gdn_sample_kernel(q_ref, k_ref, v_ref, z_ref, gt_ref, wq_ref, wk_ref, wv_ref, gc_ref, nw_ref, s0_ref,
                       *rest):
    o_ref, s_ref, raw_ref = rest[-3:]
    h = pl.program_id(0)
    BB = s0_ref.shape[0]
    q = _l2(_silu(_conv_groups(q_ref[...], wq_ref))) * (DK ** -0.5)
    k = _l2(_silu(_conv_groups(k_ref[...], wk_ref)))
    v = _silu(_conv_groups(v_ref[...], wv_ref))
    beta, gl = _gdn_gates(gt_ref[...], gc_ref)
    beta = _colsel(beta, h)
    gl = _colsel(gl, HEADS + h)
    eg = jnp.broadcast_to(jnp.exp(gl), v.shape)
    be = beta * eg
    vb = beta * v
    qT, kT = q.T, k.T
    raw_ref[...] = jnp.zeros_like(raw_ref)
    for b in range(BB):
        S = s0_ref[b]
        for t in range(GROUP - TOK0):
            r = b * GROUP + TOK0 + t
            kc = kT[:, r:r + 1]
            kS = jnp.sum(kc * S, axis=0, keepdims=True)
            S = eg[r:r + 1] * S + kc * (vb[r:r + 1] - be[r:r + 1] * kS)
            raw_ref[r:r + 1, :] = jnp.sum(qT[:, r:r + 1] * S, axis=0, keepdims=True)
        s_ref[b] = S
    o_ref[...] = (_rms(raw_ref[...]) * nw_ref[...] * _silu(z_ref[...])).astype(o_ref.dtype)


def _stacked_out(prev):
    if prev is None:
        return [], [], {}
    return [pl.BlockSpec(memory_space=pl.ANY)], [prev], None


def _gdn_sample(U8, S_all, d, S_prev, conv_w, hc, norm_w, *, BB=16):
    B = S_all.shape[1]
    R = BB * GROUP
    tok = lambda cb: pl.BlockSpec((R, DK), lambda h, b: (b, cb + h))
    cw = lambda cb: pl.BlockSpec((CONV_W, DK), lambda h, b: (0, cb + h))
    st = pl.BlockSpec((None, BB, None, DK, DK), lambda h, b: (d, b, h, 0, 0))
    extra_specs, extra_ops, _ = _stacked_out(S_prev)
    n_in = 11
    return pl.pallas_call(
        _gdn_sample_kernel,
        grid=(HEADS, B // BB),
        in_specs=[tok(CB_Q), tok(CB_K), tok(CB_V), tok(CB_Z),
                  pl.BlockSpec((R, DK), lambda h, b: (b, CB_GATES)),
                  cw(CB_Q), cw(CB_K), cw(CB_V),
                  pl.BlockSpec((8, DK), lambda h, b: (0, 0)),
                  pl.BlockSpec((1, DK), lambda h, b: (0, 0)), st] + extra_specs,
        out_specs=[pl.BlockSpec((R, DK), lambda h, b: (b, h)), st],
        out_shape=[jax.ShapeDtypeStruct((B * GROUP, QK), bf16),
                   jax.ShapeDtypeStruct(S_all.shape, f32)],
        input_output_aliases={n_in: 1} if S_prev is not None else {},
        scratch_shapes=[pltpu.VMEM((R, DK), f32)],
        compiler_params=_cparams(("parallel", "parallel")),
        name="gdn_sample",
    )(U8, U8, U8, U8, U8, conv_w, conv_w, conv_w, hc, norm_w.reshape(1, DK), S_all, *extra_ops)


def _mlstm_sample_kernel(q_ref, k_ref, v_ref, og_ref, gt_ref, gc_ref, nw_ref, c0_ref, n0_ref, m0_ref,
                         *rest):
    h_ref, c_ref, n_ref, m_ref, raw_ref = rest[-5:]
    h = pl.program_id(0)
    BB = c0_ref.shape[0]
    q = q_ref[...]
    k = k_ref[...] * (DK ** -0.5)
    v = v_ref[...]
    i_pre, logf = _mlstm_gates(gt_ref[...], gc_ref)
    i_pre = _colsel(i_pre, 2 * HEADS + h)
    logf = _colsel(logf, 3 * HEADS + h)
    pos = _group_pos(q.shape)
    m = m0_ref[...]
    for t in range(GROUP - TOK0):
        m = jnp.where(pos == TOK0 + t, jnp.maximum(logf + pltpu.roll(m, 1, axis=0), i_pre), m)
    fd = jnp.exp(logf + pltpu.roll(m, 1, axis=0) - m)
    ig = jnp.exp(i_pre - m)
    ks = ig * k
    n = n0_ref[...]
    for t in range(GROUP - TOK0):
        n = jnp.where(pos == TOK0 + t, fd * pltpu.roll(n, 1, axis=0) + ks, n)
    den = jnp.sum(q * n, axis=1, keepdims=True)
    inv = 1.0 / jnp.maximum(jnp.abs(den), jnp.exp(-m))
    qT, kT = q.T, ks.T
    raw_ref[...] = jnp.zeros_like(raw_ref)
    for b in range(BB):
        Cm = c0_ref[b]
        for t in range(GROUP - TOK0):
            r = b * GROUP + TOK0 + t
            Cm = fd[r:r + 1] * Cm + kT[:, r:r + 1] * v[r:r + 1]
            raw_ref[r:r + 1, :] = jnp.sum(qT[:, r:r + 1] * Cm, axis=0, keepdims=True)
        c_ref[b] = Cm
    n_ref[...] = n
    m_ref[...] = m
    h_ref[...] = (_rms(raw_ref[...] * inv) * nw_ref[h] * jax.nn.sigmoid(og_ref[...])).astype(h_ref.dtype)


def _mlstm_sample(U8, C_all, d, C_prev, n0g, m0g, hc, norm_w, *, BB=16):
    B = C_all.shape[1]
    R = BB * GROUP
    tok = lambda cb: pl.BlockSpec((R, DK), lambda h, b: (b, cb + h))
    row = pl.BlockSpec((R, DK), lambda h, b: (b, h))
    st = pl.BlockSpec((None, BB, None, DK, DK), lambda h, b: (d, b, h, 0, 0))
    extra_specs, extra_ops, _ = _stacked_out(C_prev)
    n_in = 10
    return pl.pallas_call(
        _mlstm_sample_kernel,
        grid=(HEADS, B // BB),
        in_specs=[tok(CB_MQ), tok(CB_MK), tok(CB_MV), tok(CB_MO),
                  pl.BlockSpec((R, DK), lambda h, b: (b, CB_GATES)),
                  pl.BlockSpec((8, DK), lambda h, b: (0, 0)),
                  pl.BlockSpec((HEADS, 1, DK), lambda h, b: (0, 0, 0)),
                  st, row, row] + extra_specs,
        out_specs=[row, st, row, row],
        out_shape=[jax.ShapeDtypeStruct((B * GROUP, QK), bf16),
                   jax.ShapeDtypeStruct(C_all.shape, f32),
                   jax.ShapeDtypeStruct((B * GROUP, QK), f32),
                   jax.ShapeDtypeStruct((B * GROUP, QK), f32)],
        input_output_aliases={n_in: 1} if C_prev is not None else {},
        scratch_shapes=[pltpu.VMEM((R, DK), f32)],
        compiler_params=_cparams(("parallel", "parallel")),
        name="mlstm_sample",
    )(U8, U8, U8, U8, U8, hc, norm_w.reshape(HEADS, 1, DK), C_all, n0g, m0g, *extra_ops)


def _lru_sample_kernel(x_ref, y_ref, h0_ref, cw_ref, cb_ref, wr_ref, br_ref, wi_ref, bi_ref, lam_ref,
                       o_ref, h_ref):
    c = _conv_groups(x_ref[...], cw_ref) + cb_ref[...]
    a, u = _lru_gates(c, wr_ref, br_ref, wi_ref, bi_ref, lam_ref)
    hs = h0_ref[...]
    pos = _group_pos(hs.shape)
    for t in range(GROUP - TOK0):
        hs = jnp.where(pos == TOK0 + t, a * pltpu.roll(hs, 1, axis=0) + u, hs)
    h_ref[...] = hs
    o_ref[...] = (hs * jax.nn.gelu(y_ref[...])).astype(o_ref.dtype)


def _lru_sample(U8, h0g, cw, cb, wr, br, wi, bi, lam, *, R=128):
    W = LRU_W
    rows = U8.shape[0]
    vec = lambda: pl.BlockSpec((1, W), lambda i: (0, 0))
    return pl.pallas_call(
        _lru_sample_kernel,
        grid=(rows // R,),
        in_specs=[pl.BlockSpec((R, W), lambda i: (i, CB_LX)),
                  pl.BlockSpec((R, W), lambda i: (i, CB_LY)),
                  pl.BlockSpec((R, W), lambda i: (i, 0)),
                  pl.BlockSpec((CONV_W, W), lambda i: (0, 0)), vec(),
                  pl.BlockSpec((W, W), lambda i: (0, 0)), vec(),
                  pl.BlockSpec((W, W), lambda i: (0, 0)), vec(), vec()],
        out_specs=[pl.BlockSpec((R, W), lambda i: (i, 0)), pl.BlockSpec((R, W), lambda i: (i, 0))],
        out_shape=[jax.ShapeDtypeStruct((rows, W), bf16), jax.ShapeDtypeStruct((rows, W), f32)],
        compiler_params=_cparams(("parallel",)),
        name="lru_sample",
    )(U8, U8, h0g, cw, cb.reshape(1, W), wr, br.reshape(1, W), wi, bi.reshape(1, W), lam.reshape(1, W))


def _permute_w_in(w_in):
    s = {}
    off = 0
    for name, size in (("qkv", CONV_CH), ("z", QK), ("gb", HEADS), ("ga", HEADS), ("mq", QK), ("mk", QK),
                       ("mv", QK), ("mo", QK), ("mi", HEADS), ("mf", HEADS), ("lx", LRU_W), ("ly", LRU_W)):
        s[name] = w_in[..., off:off + size]
        off += size
    pad = jnp.zeros(w_in.shape[:-1] + (DK - 4 * HEADS,), w_in.dtype)
    return jnp.concatenate([s["qkv"], s["z"], s["mq"], s["mk"], s["mv"], s["mo"], s["lx"], s["ly"],
                            s["gb"], s["ga"], s["mi"], s["mf"], pad], axis=-1)


def _block_diag(w):
    nb, d, e = w.shape
    eye = jnp.eye(nb, dtype=w.dtype)
    return (w[:, :, None, :] * eye[:, None, :, None]).reshape(nb * d, nb * e)


def _gate_consts(a_log, dt_bias, i_bias, f_bias):
    row = lambda v, k: jnp.pad(v, (k * HEADS, DK - (k + 1) * HEADS))
    rows = [row(a_log, 1), row(dt_bias, 1), row(i_bias, 2), row(f_bias, 3)]
    return jnp.stack(rows + [jnp.zeros((DK,), f32)] * (8 - len(rows)))


def _state_row(state):
    B, W = state.shape
    return jnp.pad(state[:, None, :], ((0, 0), (TOK0 - 1, GROUP - TOK0), (0, 0))).reshape(B * GROUP, W)


def _tail_rows(U, B, T, col, width):
    return jnp.stack([lax.slice(U, ((b + 1) * T - (CONV_W - 1), col), ((b + 1) * T, col + width))
                      for b in range(B)])


def kernel(x_prompt, x_sample, state_gdn_S, state_gdn_conv, state_mlstm_C, state_mlstm_n, state_mlstm_m, state_lru_h, state_lru_conv, ffn1_wg, ffn1_wu, ffn1_wd, ln_g, ln_b, w_in, gdn_conv_w, gdn_A_log, gdn_dt_bias, gdn_norm_w, mlstm_i_bias, mlstm_f_bias, mlstm_norm_w, lru_conv_w, lru_conv_b, lru_wr, lru_br, lru_wi, lru_bi, lru_lambda, w_out, ffn2_wg, ffn2_wu, ffn2_wd):
    BP, TP, D = x_prompt.shape
    BS, TS, _ = x_sample.shape
    MP, MS = BP * TP, BS * TS
    ffn1 = tuple(w.astype(bf16) for w in (ffn1_wg, ffn1_wu, ffn1_wd))
    ffn2 = tuple(w.astype(bf16) for w in (ffn2_wg, ffn2_wu, ffn2_wd))
    w_in_p = _permute_w_in(w_in.astype(bf16))
    w_out_b = w_out.astype(bf16)

    x = (x_prompt.reshape(MP, D), x_sample.reshape(MS, D))
    outs_p = [[] for _ in range(7)]
    outs_s = [[] for _ in range(7)]
    gS_s = mC_s = None
    for d in range(DEPTH):
        x1 = _ffn_ln(x, *ffn1, d, ln_g[d, 0], ln_b[d, 0])
        U = _inproj(x1, w_in_p, d)
        hc = _gate_consts(gdn_A_log[d], gdn_dt_bias[d], mlstm_i_bias[d], mlstm_f_bias[d])
        wr = _block_diag(lru_wr[d]).astype(bf16)
        wi = _block_diag(lru_wi[d]).astype(bf16)
        lru_w = (lru_conv_w[d], lru_conv_b[d], wr, lru_br[d], wi, lru_bi[d], lru_lambda[d])

        og_p, gS_p = _gdn_prompt(U, BP, TP, gdn_conv_w[d], hc, gdn_norm_w[d])
        hm_p, mC_p, mn_p, mm_p = _mlstm_prompt(U, BP, TP, hc, mlstm_norm_w[d])
        ol_p, lh_p = _lru_prompt(U, BP, TP, *lru_w)
        outs_p[0].append(gS_p)
        outs_p[1].append(_tail_rows(U, BP, TP, 0, CONV_CH))
        outs_p[2].append(mC_p)
        outs_p[3].append(mn_p.reshape(BP, HEADS, DK))
        outs_p[4].append(mm_p[:, :, 0, 0])
        outs_p[5].append(lh_p.reshape(BP, LRU_W))
        outs_p[6].append(_tail_rows(U, BP, TP, COL_LX, LRU_W))

        Us = lax.slice(U, (MP, 0), (MP + MS, D_INP)).reshape(BS, TS, D_INP)
        hist = jnp.concatenate([
            state_gdn_conv[d], jnp.zeros((BS, 3, COL_LX - CONV_CH), f32),
            state_lru_conv[d], jnp.zeros((BS, 3, D_INP - COL_LX - LRU_W), f32)], axis=2)
        U8 = jnp.concatenate([jnp.zeros((BS, 1, D_INP), f32), hist, Us], axis=1).reshape(BS * GROUP, D_INP)
        og_s, gS_s = _gdn_sample(U8, state_gdn_S, d, gS_s, gdn_conv_w[d], hc, gdn_norm_w[d])
        hm_s, mC_s, n8, m8 = _mlstm_sample(
            U8, state_mlstm_C, d, mC_s, _state_row(state_mlstm_n[d].reshape(BS, QK)),
            _state_row(jnp.repeat(state_mlstm_m[d], DK, axis=1)), hc, mlstm_norm_w[d])
        ol_s, h8 = _lru_sample(U8, _state_row(state_lru_h[d]), *lru_w)
        toks = lambda a: a.reshape(BS, GROUP, -1)[:, TOK0:].reshape(MS, -1)
        last = lambda a: a.reshape(BS, GROUP, -1)[:, GROUP - 1]
        outs_s[1].append(Us[:, TS - 3:, 0:CONV_CH])
        outs_s[3].append(last(n8).reshape(BS, HEADS, DK))
        outs_s[4].append(last(m8).reshape(BS, HEADS, DK)[:, :, 0])
        outs_s[5].append(last(h8))
        outs_s[6].append(Us[:, TS - 3:, COL_LX:COL_LX + LRU_W])

        x2 = _outproj_ln((og_p, hm_p, ol_p), (toks(og_s), toks(hm_s), toks(ol_s)), x1, w_out_b, d,
                         ln_g[d, 1], ln_b[d, 1])
        x = _ffn_ln(x2, *ffn2, d, ln_g[d, 2], ln_b[d, 2], split_out=(d == DEPTH - 1))

    y_prompt = x[0].reshape(BP, TP, D)
    y_sample = x[1].reshape(BS, TS, D)
    stack = lambda o: jnp.stack(o)
    return (y_prompt, y_sample,
            *[stack(o) for o in outs_p],
            gS_s, stack(outs_s[1]), mC_s, *[stack(outs_s[i]) for i in range(3, 7)])
```

```python
import functools
import math

import jax
import jax.numpy as jnp
from jax import lax
from jax.experimental import pallas as pl
from jax.experimental.pallas import tpu as pltpu

f32 = jnp.float32
bf16 = jnp.bfloat16

DEPTH = 2
D_MODEL = 2048
D_FF = 5632
HEADS = 6
DK = 128
LRU_W = 512
LRU_BLOCKS = 8
LRU_C = 8.0
CONV_W = 4
CHUNK = 64
ALPHA = (2 * DEPTH) ** 0.25
NORM_EPS = 1e-6
NEG = -1e30

QK = HEADS * DK
CONV_CH = 3 * QK
MIX_W = 4 * QK
CB_Q, CB_K, CB_V, CB_Z = 0, 6, 12, 18
CB_MQ, CB_MK, CB_MV, CB_MO = 24, 30, 36, 42
CB_GDN, CB_MLSTM = 0, 1
CB_LX, CB_LY = 12, 13
CB_GATES = 56
COL_LX = 6144
D_INP = 7296
GROUP = 8
TOK0 = 4
ROW_TILE = 512
PROMPT_ROW_TILE = 1024

VMEM_LIMIT = 60 * 1024 * 1024


def _cparams(sem):
    return pltpu.CompilerParams(dimension_semantics=sem, vmem_limit_bytes=VMEM_LIMIT)


def _silu(x):
    return x * jax.nn.sigmoid(x)


def _softplus(x):
    return jnp.maximum(x, 0.0) + jnp.log1p(jnp.exp(-jnp.abs(x)))


def _expm1(x):
    u = jnp.exp(x)
    um1 = u - 1.0
    lg = jnp.where(u == 1.0, 1.0, jnp.log(jnp.where(u == 0.0, 1.0, u)))
    return jnp.where(u == 1.0, x, jnp.where(u == 0.0, -1.0, um1 * x / lg))


def _layernorm(y, g, b):
    mu = jnp.mean(y, -1, keepdims=True)
    d = y - mu
    var = jnp.mean(d * d, -1, keepdims=True)
    return d * lax.rsqrt(var + NORM_EPS) * g + b


def _rms(x):
    return x * lax.rsqrt(jnp.mean(x * x, -1, keepdims=True) + NORM_EPS)


def _l2(x):
    return x * lax.rsqrt(jnp.sum(x * x, -1, keepdims=True) + NORM_EPS)


def _colsel(gates, idx):
    lane = lax.broadcasted_iota(jnp.int32, gates.shape, 1)
    return jnp.sum(jnp.where(lane == idx, gates, 0.0), axis=1, keepdims=True)


def _dot(a, b):
    return jnp.dot(a, b, preferred_element_type=f32)


def _bmm(a, b):
    return jnp.einsum('nij,njk->nik', a, b, preferred_element_type=f32)


def _bmm_nt(a, b):
    return jnp.einsum('nid,njd->nij', a, b, preferred_element_type=f32)


def _bmm_tn(a, b):
    return jnp.einsum('nck,ncv->nkv', a, b, preferred_element_type=f32)


def _ij(C):
    return (lax.broadcasted_iota(jnp.int32, (C, C), 0), lax.broadcasted_iota(jnp.int32, (C, C), 1))


def _cumsum_row(col3, ii, jj):
    return jnp.sum(jnp.where((ii <= jj)[None], col3, 0.0), axis=1, keepdims=True)


def _row2col(row3, ii, jj):
    return jnp.sum(jnp.where((ii == jj)[None], row3, 0.0), axis=2, keepdims=True)


def _col2row(col3, ii, jj):
    return jnp.sum(jnp.where((ii == jj)[None], col3, 0.0), axis=1, keepdims=True)


def _conv_carry(x, w_ref, xp_ref, tail):
    T = x.shape[0]
    xp_ref[0:8, :] = tail
    xp_ref[8:T + 8, :] = x
    y = w_ref[3:4, :] * x
    for k in range(CONV_W - 1):
        y = y + w_ref[k:k + 1, :] * xp_ref[pl.ds(5 + k, T), :]
    return y


def _conv_groups(x, w_ref):
    y = w_ref[3:4, :] * x
    for s in range(1, CONV_W):
        y = y + w_ref[3 - s:4 - s, :] * pltpu.roll(x, s, axis=0)
    return y


def _group_pos(shape):
    return lax.broadcasted_iota(jnp.int32, shape, 0) & (GROUP - 1)


def _ffn_ln_kernel(x_ref, wg_ref, wu_ref, wd_ref, g_ref, b_ref, o_ref, xb_ref):
    j = pl.program_id(1)

    @pl.when(j == 0)
    def _():
        o_ref[...] = jnp.zeros_like(o_ref)
        xb_ref[...] = x_ref[...].astype(bf16)

    xb = xb_ref[...]
    hg = _dot(xb, wg_ref[...].astype(bf16))
    hu = _dot(xb, wu_ref[...].astype(bf16))
    h = (_silu(hg) * hu).astype(bf16)
    o_ref[...] += _dot(h, wd_ref[...].astype(bf16))

    @pl.when(j == pl.num_programs(1) - 1)
    def _():
        y = ALPHA * x_ref[...] + 0.5 * o_ref[...]
        o_ref[...] = _layernorm(y, g_ref[...], b_ref[...])


def _ffn_ln(x, wg, wu, wd, d, g, b, *, tm, tf=512):
    M, D = x.shape
    F = wg.shape[2]
    return pl.pallas_call(
        _ffn_ln_kernel,
        grid=(M // tm, F // tf),
        in_specs=[
            pl.BlockSpec((tm, D), lambda i, j: (i, 0), pipeline_mode=pl.Buffered(1)),
            pl.BlockSpec((None, D, tf), lambda i, j: (d, 0, j)),
            pl.BlockSpec((None, D, tf), lambda i, j: (d, 0, j)),
            pl.BlockSpec((None, tf, D), lambda i, j: (d, j, 0)),
            pl.BlockSpec((1, D), lambda i, j: (0, 0)),
            pl.BlockSpec((1, D), lambda i, j: (0, 0)),
        ],
        out_specs=pl.BlockSpec((tm, D), lambda i, j: (i, 0), pipeline_mode=pl.Buffered(1)),
        out_shape=jax.ShapeDtypeStruct((M, D), f32),
        scratch_shapes=[pltpu.VMEM((tm, D), bf16)],
        compiler_params=_cparams(("parallel", "arbitrary")),
        name="ffn_ln",
    )(x, wg, wu, wd, g.reshape(1, D), b.reshape(1, D))


def _inproj_kernel(x_ref, w_ref, o_ref):
    o_ref[...] = _dot(x_ref[...].astype(bf16), w_ref[...])


def _inproj(x, w, d, *, tm=ROW_TILE, tn=2432):
    M, D = x.shape
    N = w.shape[2]
    return pl.pallas_call(
        _inproj_kernel,
        grid=(N // tn, M // tm),
        in_specs=[pl.BlockSpec((tm, D), lambda n, i: (i, 0)),
                  pl.BlockSpec((None, D, tn), lambda n, i: (d, 0, n))],
        out_specs=pl.BlockSpec((tm, tn), lambda n, i: (i, n)),
        out_shape=jax.ShapeDtypeStruct((M, N), f32),
        compiler_params=_cparams(("parallel", "arbitrary")),
        name="in_proj",
    )(x, w)


def _outproj_ln_kernel(og_ref, hm_ref, ol_ref, x_ref, w_ref, g_ref, b_ref, o_ref):
    mix = (_dot(og_ref[...], w_ref[0:QK, :]) + _dot(hm_ref[...], w_ref[QK:2 * QK, :])
           + _dot(ol_ref[...], w_ref[2 * QK:, :]))
    o_ref[...] = _layernorm(ALPHA * x_ref[...] + mix, g_ref[...], b_ref[...])


def _outproj_ln(mix, x, w, d, g, b, *, tm=ROW_TILE):
    M, D = x.shape
    return pl.pallas_call(
        _outproj_ln_kernel,
        grid=(M // tm,),
        in_specs=[pl.BlockSpec((tm, wd), lambda i: (i, 0)) for wd in (QK, QK, LRU_W)]
        + [pl.BlockSpec((tm, D), lambda i: (i, 0)),
           pl.BlockSpec((None, D, D), lambda i: (d, 0, 0)),
           pl.BlockSpec((1, D), lambda i: (0, 0)),
           pl.BlockSpec((1, D), lambda i: (0, 0))],
        out_specs=pl.BlockSpec((tm, D), lambda i: (i, 0)),
        out_shape=jax.ShapeDtypeStruct((M, D), f32),
        compiler_params=_cparams(("parallel",)),
        name="out_proj_ln",
    )(*mix, x, w, g.reshape(1, D), b.reshape(1, D))


def _gdn_gates(gates, gc_ref):
    beta = jax.nn.sigmoid(gates)
    gl = -jnp.exp(gc_ref[0:1, :]) * _softplus(gates + gc_ref[1:2, :])
    return beta, gl


def _mlstm_gates(gates, gc_ref):
    i_pre = gates + gc_ref[2:3, :]
    logf = -_softplus(-(gates + gc_ref[3:4, :]))
    return i_pre, logf


def _gdn_prompt_kernel(*refs, C, B):
    mains, gts = refs[0:2 * B:2], refs[1:2 * B:2]
    cw_ref, gc_ref, nw_ref, o_ref, s_ref, xp_ref, tail_ref = refs[2 * B:]
    n = pl.program_id(0)
    NC = B * HEADS

    @pl.when(n == 0)
    def _():
        tail_ref[...] = jnp.zeros_like(tail_ref)
        s_ref[...] = jnp.zeros_like(s_ref)

    qs, ks, vs, zs, bs, gs = [], [], [], [], [], []
    for b in range(B):
        x = mains[b][:, 0:CONV_CH]
        y = _silu(_conv_carry(x, cw_ref, xp_ref, tail_ref[b]))
        tail_ref[b] = x[C - 8:C]
        beta, gl = _gdn_gates(gts[b][...], gc_ref)
        for h in range(HEADS):
            qs.append(y[:, h * DK:(h + 1) * DK])
            ks.append(y[:, QK + h * DK:QK + (h + 1) * DK])
            vs.append(y[:, 2 * QK + h * DK:2 * QK + (h + 1) * DK])
            zs.append(mains[b][:, CONV_CH + h * DK:CONV_CH + (h + 1) * DK])
            bs.append(beta[:, h:h + 1])
            gs.append(gl[:, HEADS + h:HEADS + h + 1])
    q3 = _l2(jnp.stack(qs)) * (DK ** -0.5)
    k3 = _l2(jnp.stack(ks))
    v3 = jnp.stack(vs)
    b3 = jnp.stack(bs)
    g3 = jnp.stack(gs)

    ii, jj = _ij(C)
    g_row = _cumsum_row(g3, ii, jj)
    g_col = _row2col(g_row, ii, jj)
    decay = jnp.exp(jnp.where((ii >= jj)[None], g_col - g_row, NEG))
    kb = k3.astype(bf16)
    kk = _bmm_nt(kb, kb)
    p = jnp.where((ii > jj)[None], -(b3 * kk * decay), 0.0)
    e = p
    for _ in range(int(math.log2(C)) - 1):
        pb = p.astype(bf16)
        p = _bmm(pb, pb)
        e = e + p + _bmm(e.astype(bf16), p.astype(bf16))
    eb = e.astype(bf16)
    e_g = jnp.exp(g_col)
    rv = b3 * v3
    rk = (b3 * e_g) * k3
    uv = rv + _bmm(eb, rv.astype(bf16))
    wks = (rk + _bmm(eb, rk.astype(bf16))).astype(bf16)
    qk = (_bmm_nt(q3.astype(bf16), kb) * decay).astype(bf16)
    g_last = g_row[:, :, C - 1:C]
    qd = (q3 * e_g).astype(bf16)
    kt = (k3 * jnp.exp(g_last - g_col)).astype(bf16)

    S = s_ref[...].reshape(NC, DK, DK)
    Sb = S.astype(bf16)
    Ub = (uv - _bmm(wks, Sb)).astype(bf16)
    o = _bmm(qd, Sb) + _bmm(qk, Ub)
    s_ref[...] = (jnp.exp(g_last) * S + _bmm_tn(kt, Ub)).reshape(B, HEADS, DK, DK)
    o = _rms(o) * nw_ref[...] * _silu(jnp.stack(zs))
    for b in range(B):
        for h in range(HEADS):
            o_ref[b, :, h * DK:(h + 1) * DK] = o[b * HEADS + h].astype(o_ref.dtype)


def _gdn_prompt(U, B, T, conv_w, hc, norm_w, *, C=CHUNK):
    N = T // C
    in_specs = []
    for b in range(B):
        in_specs.append(pl.BlockSpec((C, MIX_W), lambda n, b=b: (b * N + n, CB_GDN)))
        in_specs.append(pl.BlockSpec((C, DK), lambda n, b=b: (b * N + n, CB_GATES)))
    in_specs += [pl.BlockSpec((CONV_W, CONV_CH), lambda n: (0, 0)),
                 pl.BlockSpec((8, DK), lambda n: (0, 0)),
                 pl.BlockSpec((1, DK), lambda n: (0, 0))]
    o, S = pl.pallas_call(
        functools.partial(_gdn_prompt_kernel, C=C, B=B),
        grid=(N,),
        in_specs=in_specs,
        out_specs=[pl.BlockSpec((B, None, C, QK), lambda n: (0, n, 0, 0)),
                   pl.BlockSpec((B, HEADS, DK, DK), lambda n: (0, 0, 0, 0))],
        out_shape=[jax.ShapeDtypeStruct((B, N, C, QK), bf16),
                   jax.ShapeDtypeStruct((B, HEADS, DK, DK), f32)],
        scratch_shapes=[pltpu.VMEM((C + 8, CONV_CH), f32), pltpu.VMEM((B, 8, CONV_CH), f32)],
        compiler_params=_cparams(("arbitrary",)),
        name="gdn_prompt",
    )(*([U, U] * B), conv_w, hc, norm_w.reshape(1, DK))
    return o.reshape(B * T, QK), S


def _mlstm_prompt_kernel(*refs, C, B):
    mains, gts = refs[0:2 * B:2], refs[1:2 * B:2]
    gc_ref, nw_ref, h_ref, c_ref, n_ref, m_ref = refs[2 * B:]
    n = pl.program_id(0)
    NC = B * HEADS

    @pl.when(n == 0)
    def _():
        c_ref[...] = jnp.zeros_like(c_ref)
        n_ref[...] = jnp.zeros_like(n_ref)
        m_ref[...] = jnp.zeros_like(m_ref)

    qs, ks, vs, os_, is_, fs = [], [], [], [], [], []
    for b in range(B):
        i_pre, logf = _mlstm_gates(gts[b][...], gc_ref)
        for h in range(HEADS):
            qs.append(mains[b][:, h * DK:(h + 1) * DK])
            ks.append(mains[b][:, QK + h * DK:QK + (h + 1) * DK])
            vs.append(mains[b][:, 2 * QK + h * DK:2 * QK + (h + 1) * DK])
            os_.append(mains[b][:, 3 * QK + h * DK:3 * QK + (h + 1) * DK])
            is_.append(i_pre[:, 2 * HEADS + h:2 * HEADS + h + 1])
            fs.append(logf[:, 3 * HEADS + h:3 * HEADS + h + 1])
    q3 = jnp.stack(qs)
    k3 = jnp.stack(ks) * (DK ** -0.5)
    vb = jnp.stack(vs).astype(bf16)
    i3 = jnp.stack(is_)
    f3 = jnp.stack(fs)

    ii, jj = _ij(C)
    b_row = _cumsum_row(f3, ii, jj)
    b_col = _row2col(b_row, ii, jj)
    i_row = _col2row(i3, ii, jj)
    D = jnp.where((ii >= jj)[None], b_col - b_row + i_row, NEG)
    d_max = jnp.max(D, axis=2, keepdims=True)
    qb = q3.astype(bf16)
    qk = _bmm_nt(qb, k3.astype(bf16))

    Cm = c_ref[...].reshape(NC, DK, DK)
    nr = n_ref[...].reshape(NC, 1, DK)
    m = m_ref[...].reshape(NC, 1, DK)[:, :, 0:1]
    m_t = jnp.maximum(b_col + m, d_max)
    inter = jnp.exp(b_col + m - m_t)
    Sw = jnp.exp(D - m_t) * qk
    num = inter * _bmm(qb, Cm.astype(bf16)) + _bmm(Sw.astype(bf16), vb)
    den = inter * jnp.sum(q3 * nr, -1, keepdims=True) + jnp.sum(Sw, -1, keepdims=True)
    hh = num / jnp.maximum(jnp.abs(den), jnp.exp(-m_t))
    nw = jnp.stack([nw_ref[h] for _ in range(B) for h in range(HEADS)])
    hh = _rms(hh) * nw * jax.nn.sigmoid(jnp.stack(os_))
    m_new = m_t[:, C - 1:C]
    b_last = b_col[:, C - 1:C]
    kw = jnp.exp(b_last - b_col + i3 - m_new) * k3
    dec = jnp.exp(b_last + m - m_new)
    c_ref[...] = (dec * Cm + _bmm_tn(kw.astype(bf16), vb)).reshape(B, HEADS, DK, DK)
    n_ref[...] = (dec * nr + jnp.sum(kw, axis=1, keepdims=True)).reshape(B, HEADS, 1, DK)
    m_ref[...] = jnp.broadcast_to(m_new, (NC, 1, DK)).reshape(B, HEADS, 1, DK)
    for b in range(B):
        for h in range(HEADS):
            h_ref[b, :, h * DK:(h + 1) * DK] = hh[b * HEADS + h].astype(h_ref.dtype)


def _mlstm_prompt(U, B, T, hc, norm_w, *, C=CHUNK):
    N = T // C
    in_specs = []
    for b in range(B):
        in_specs.append(pl.BlockSpec((C, MIX_W), lambda n, b=b: (b * N + n, CB_MLSTM)))
        in_specs.append(pl.BlockSpec((C, DK), lambda n, b=b: (b * N + n, CB_GATES)))
    in_specs += [pl.BlockSpec((8, DK), lambda n: (0, 0)),
                 pl.BlockSpec((HEADS, 1, DK), lambda n: (0, 0, 0))]
    st = lambda r: pl.BlockSpec((B, HEADS, r, DK), lambda n: (0, 0, 0, 0))
    h, Cm, nr, m = pl.pallas_call(
        functools.partial(_mlstm_prompt_kernel, C=C, B=B),
        grid=(N,),
        in_specs=in_specs,
        out_specs=[pl.BlockSpec((B, None, C, QK), lambda n: (0, n, 0, 0)), st(DK), st(1), st(1)],
        out_shape=[jax.ShapeDtypeStruct((B, N, C, QK), bf16),
                   jax.ShapeDtypeStruct((B, HEADS, DK, DK), f32),
                   jax.ShapeDtypeStruct((B, HEADS, 1, DK), f32),
                   jax.ShapeDtypeStruct((B, HEADS, 1, DK), f32)],
        compiler_params=_cparams(("arbitrary",)),
        name="mlstm_prompt",
    )(*([U, U] * B), hc, norm_w.reshape(HEADS, 1, DK))
    return h.reshape(B * T, QK), Cm, nr, m


def _lru_gates(c, wr_ref, br_ref, wi_ref, bi_ref, lam_ref):
    cb = c.astype(bf16)
    r = jax.nn.sigmoid(_dot(cb, wr_ref[...]) + br_ref[...])
    ig = jax.nn.sigmoid(_dot(cb, wi_ref[...]) + bi_ref[...])
    log_a = -LRU_C * r * _softplus(-lam_ref[...])
    a = jnp.exp(log_a)
    u = jnp.sqrt(-_expm1(2.0 * log_a)) * (ig * c)
    return a, u


def _lru_prompt_kernel(x_ref, y_ref, cw_ref, cb_ref, wr_ref, br_ref, wi_ref, bi_ref, lam_ref,
                       o_ref, hl_ref, xp_ref, tail_ref, hc_ref):
    tc = pl.program_id(1)
    Tc, W = x_ref.shape

    @pl.when(tc == 0)
    def _():
        tail_ref[...] = jnp.zeros_like(tail_ref)
        hc_ref[...] = jnp.zeros_like(hc_ref)

    x = x_ref[...]
    c = _conv_carry(x, cw_ref, xp_ref, tail_ref[...]) + cb_ref[...]
    tail_ref[...] = x[Tc - 8:Tc]
    a, u = _lru_gates(c, wr_ref, br_ref, wi_ref, bi_ref, lam_ref)
    row = lax.broadcasted_iota(jnp.int32, (Tc, W), 0)
    d = 1
    while d < Tc:
        keep = row >= d
        a_sh = jnp.where(keep, pltpu.roll(a, d, axis=0), 1.0)
        u_sh = jnp.where(keep, pltpu.roll(u, d, axis=0), 0.0)
        u = a * u_sh + u
        a = a * a_sh
        d *= 2
    hs = u + a * hc_ref[0:1, :]
    last = hs[Tc - 1:Tc]
    hc_ref[...] = jnp.broadcast_to(last, hc_ref.shape)
    hl_ref[...] = last
    o_ref[...] = (hs * jax.nn.gelu(y_ref[...])).astype(o_ref.dtype)


def _lru_prompt(U, B, T, cw, cb, wr, br, wi, bi, lam, *, Tc=512):
    W = LRU_W
    nt = T // Tc
    vec = lambda: pl.BlockSpec((1, W), lambda b, t: (0, 0))
    return pl.pallas_call(
        _lru_prompt_kernel,
        grid=(B, nt),
        in_specs=[pl.BlockSpec((Tc, W), lambda b, t: (b * nt + t, CB_LX)),
                  pl.BlockSpec((Tc, W), lambda b, t: (b * nt + t, CB_LY)),
                  pl.BlockSpec((CONV_W, W), lambda b, t: (0, 0)), vec(),
                  pl.BlockSpec((W, W), lambda b, t: (0, 0)), vec(),
                  pl.BlockSpec((W, W), lambda b, t: (0, 0)), vec(), vec()],
        out_specs=[pl.BlockSpec((Tc, W), lambda b, t: (b * nt + t, 0)),
                   pl.BlockSpec((None, 1, W), lambda b, t: (b, 0, 0))],
        out_shape=[jax.ShapeDtypeStruct((B * T, W), bf16),
                   jax.ShapeDtypeStruct((B, 1, W), f32)],
        scratch_shapes=[pltpu.VMEM((Tc + 8, W), f32), pltpu.VMEM((8, W), f32), pltpu.VMEM((8, W), f32)],
        compiler_params=_cparams(("parallel", "arbitrary")),
        name="lru_prompt",
    )(U, U, cw, cb.reshape(1, W), wr, br.reshape(1, W), wi, bi.reshape(1, W), lam.reshape(1, W))


def _gdn_sample_kernel(q_ref, k_ref, v_ref, z_ref, gt_ref, wq_ref, wk_ref, wv_ref, gc_ref, nw_ref, s0_ref,
                       *rest):
    o_ref, s_ref, raw_ref = rest[-3:]
    h = pl.program_id(0)
    BB = s0_ref.shape[0]
    q = _l2(_silu(_conv_groups(q_ref[...], wq_ref))) * (DK ** -0.5)
    k = _l2(_silu(_conv_groups(k_ref[...], wk_ref)))
    v = _silu(_conv_groups(v_ref[...], wv_ref))
    beta, gl = _gdn_gates(gt_ref[...], gc_ref)
    beta = _colsel(beta, h)
    gl = _colsel(gl, HEADS + h)
    eg = jnp.broadcast_to(jnp.exp(gl), v.shape)
    be = beta * eg
    vb = beta * v
    qT, kT = q.T, k.T
    raw_ref[...] = jnp.zeros_like(raw_ref)
    for b in range(BB):
        S = s0_ref[b]
        for t in range(GROUP - TOK0):
            r = b * GROUP + TOK0 + t
            kc = kT[:, r:r + 1]
            kS = jnp.sum(kc * S, axis=0, keepdims=True)
            S = eg[r:r + 1] * S + kc * (vb[r:r + 1] - be[r:r + 1] * kS)
            raw_ref[r:r + 1, :] = jnp.sum(qT[:, r:r + 1] * S, axis=0, keepdims=True)
        s_ref[b] = S
    o_ref[...] = (_rms(raw_ref[...]) * nw_ref[...] * _silu(z_ref[...])).astype(o_ref.dtype)


def _stacked_out(prev):
    if prev is None:
        return [], [], {}
    return [pl.BlockSpec(memory_space=pl.ANY)], [prev], None


def _gdn_sample(U8, S_all, d, S_prev, conv_w, hc, norm_w, *, BB=16):
    B = S_all.shape[1]
    R = BB * GROUP
    tok = lambda cb: pl.BlockSpec((R, DK), lambda h, b: (b, cb + h))
    cw = lambda cb: pl.BlockSpec((CONV_W, DK), lambda h, b: (0, cb + h))
    st = pl.BlockSpec((None, BB, None, DK, DK), lambda h, b: (d, b, h, 0, 0))
    extra_specs, extra_ops, _ = _stacked_out(S_prev)
    n_in = 11
    return pl.pallas_call(
        _gdn_sample_kernel,
        grid=(HEADS, B // BB),
        in_specs=[tok(CB_Q), tok(CB_K), tok(CB_V), tok(CB_Z),
                  pl.BlockSpec((R, DK), lambda h, b: (b, CB_GATES)),
                  cw(CB_Q), cw(CB_K), cw(CB_V),
                  pl.BlockSpec((8, DK), lambda h, b: (0, 0)),
                  pl.BlockSpec((1, DK), lambda h, b: (0, 0)), st] + extra_specs,
        out_specs=[pl.BlockSpec((R, DK), lambda h, b: (b, h)), st],
        out_shape=[jax.ShapeDtypeStruct((B * GROUP, QK), bf16),
                   jax.ShapeDtypeStruct(S_all.shape, f32)],
        input_output_aliases={n_in: 1} if S_prev is not None else {},
        scratch_shapes=[pltpu.VMEM((R, DK), f32)],
        compiler_params=_cparams(("parallel", "parallel")),
        name="gdn_sample",
    )(U8, U8, U8, U8, U8, conv_w, conv_w, conv_w, hc, norm_w.reshape(1, DK), S_all, *extra_ops)


def _mlstm_sample_kernel(q_ref, k_ref, v_ref, og_ref, gt_ref, gc_ref, nw_ref, c0_ref, n0_ref, m0_ref,
                         *rest):
    h_ref, c_ref, n_ref, m_ref, raw_ref = rest[-5:]
    h = pl.program_id(0)
    BB = c0_ref.shape[0]
    q = q_ref[...]
    k = k_ref[...] * (DK ** -0.5)
    v = v_ref[...]
    i_pre, logf = _mlstm_gates(gt_ref[...], gc_ref)
    i_pre = _colsel(i_pre, 2 * HEADS + h)
    logf = _colsel(logf, 3 * HEADS + h)
    pos = _group_pos(q.shape)
    m = m0_ref[...]
    for t in range(GROUP - TOK0):
        m = jnp.where(pos == TOK0 + t, jnp.maximum(logf + pltpu.roll(m, 1, axis=0), i_pre), m)
    fd = jnp.exp(logf + pltpu.roll(m, 1, axis=0) - m)
    ig = jnp.exp(i_pre - m)
    ks = ig * k
    n = n0_ref[...]
    for t in range(GROUP - TOK0):
        n = jnp.where(pos == TOK0 + t, fd * pltpu.roll(n, 1, axis=0) + ks, n)
    den = jnp.sum(q * n, axis=1, keepdims=True)
    inv = 1.0 / jnp.maximum(jnp.abs(den), jnp.exp(-m))
    qT, kT = q.T, ks.T
    raw_ref[...] = jnp.zeros_like(raw_ref)
    for b in range(BB):
        Cm = c0_ref[b]
        for t in range(GROUP - TOK0):
            r = b * GROUP + TOK0 + t
            Cm = fd[r:r + 1] * Cm + kT[:, r:r + 1] * v[r:r + 1]
            raw_ref[r:r + 1, :] = jnp.sum(qT[:, r:r + 1] * Cm, axis=0, keepdims=True)
        c_ref[b] = Cm
    n_ref[...] = n
    m_ref[...] = m
    h_ref[...] = (_rms(raw_ref[...] * inv) * nw_ref[h] * jax.nn.sigmoid(og_ref[...])).astype(h_ref.dtype)


def _mlstm_sample(U8, C_all, d, C_prev, n0g, m0g, hc, norm_w, *, BB=16):
    B = C_all.shape[1]
    R = BB * GROUP
    tok = lambda cb: pl.BlockSpec((R, DK), lambda h, b: (b, cb + h))
    row = pl.BlockSpec((R, DK), lambda h, b: (b, h))
    st = pl.BlockSpec((None, BB, None, DK, DK), lambda h, b: (d, b, h, 0, 0))
    extra_specs, extra_ops, _ = _stacked_out(C_prev)
    n_in = 10
    return pl.pallas_call(
        _mlstm_sample_kernel,
        grid=(HEADS, B // BB),
        in_specs=[tok(CB_MQ), tok(CB_MK), tok(CB_MV), tok(CB_MO),
                  pl.BlockSpec((R, DK), lambda h, b: (b, CB_GATES)),
                  pl.BlockSpec((8, DK), lambda h, b: (0, 0)),
                  pl.BlockSpec((HEADS, 1, DK), lambda h, b: (0, 0, 0)),
                  st, row, row] + extra_specs,
        out_specs=[row, st, row, row],
        out_shape=[jax.ShapeDtypeStruct((B * GROUP, QK), bf16),
                   jax.ShapeDtypeStruct(C_all.shape, f32),
                   jax.ShapeDtypeStruct((B * GROUP, QK), f32),
                   jax.ShapeDtypeStruct((B * GROUP, QK), f32)],
        input_output_aliases={n_in: 1} if C_prev is not None else {},
        scratch_shapes=[pltpu.VMEM((R, DK), f32)],
        compiler_params=_cparams(("parallel", "parallel")),
        name="mlstm_sample",
    )(U8, U8, U8, U8, U8, hc, norm_w.reshape(HEADS, 1, DK), C_all, n0g, m0g, *extra_ops)


def _lru_sample_kernel(x_ref, y_ref, h0_ref, cw_ref, cb_ref, wr_ref, br_ref, wi_ref, bi_ref, lam_ref,
                       o_ref, h_ref):
    c = _conv_groups(x_ref[...], cw_ref) + cb_ref[...]
    a, u = _lru_gates(c, wr_ref, br_ref, wi_ref, bi_ref, lam_ref)
    hs = h0_ref[...]
    pos = _group_pos(hs.shape)
    for t in range(GROUP - TOK0):
        hs = jnp.where(pos == TOK0 + t, a * pltpu.roll(hs, 1, axis=0) + u, hs)
    h_ref[...] = hs
    o_ref[...] = (hs * jax.nn.gelu(y_ref[...])).astype(o_ref.dtype)


def _lru_sample(U8, h0g, cw, cb, wr, br, wi, bi, lam, *, R=128):
    W = LRU_W
    rows = U8.shape[0]
    vec = lambda: pl.BlockSpec((1, W), lambda i: (0, 0))
    return pl.pallas_call(
        _lru_sample_kernel,
        grid=(rows // R,),
        in_specs=[pl.BlockSpec((R, W), lambda i: (i, CB_LX)),
                  pl.BlockSpec((R, W), lambda i: (i, CB_LY)),
                  pl.BlockSpec((R, W), lambda i: (i, 0)),
                  pl.BlockSpec((CONV_W, W), lambda i: (0, 0)), vec(),
                  pl.BlockSpec((W, W), lambda i: (0, 0)), vec(),
                  pl.BlockSpec((W, W), lambda i: (0, 0)), vec(), vec()],
        out_specs=[pl.BlockSpec((R, W), lambda i: (i, 0)), pl.BlockSpec((R, W), lambda i: (i, 0))],
        out_shape=[jax.ShapeDtypeStruct((rows, W), bf16), jax.ShapeDtypeStruct((rows, W), f32)],
        compiler_params=_cparams(("parallel",)),
        name="lru_sample",
    )(U8, U8, h0g, cw, cb.reshape(1, W), wr, br.reshape(1, W), wi, bi.reshape(1, W), lam.reshape(1, W))


def _permute_w_in(w_in):
    s = {}
    off = 0
    for name, size in (("qkv", CONV_CH), ("z", QK), ("gb", HEADS), ("ga", HEADS), ("mq", QK), ("mk", QK),
                       ("mv", QK), ("mo", QK), ("mi", HEADS), ("mf", HEADS), ("lx", LRU_W), ("ly", LRU_W)):
        s[name] = w_in[..., off:off + size]
        off += size
    pad = jnp.zeros(w_in.shape[:-1] + (DK - 4 * HEADS,), w_in.dtype)
    return jnp.concatenate([s["qkv"], s["z"], s["mq"], s["mk"], s["mv"], s["mo"], s["lx"], s["ly"],
                            s["gb"], s["ga"], s["mi"], s["mf"], pad], axis=-1)


def _block_diag(w):
    nb, d, e = w.shape
    eye = jnp.eye(nb, dtype=w.dtype)
    return (w[:, :, None, :] * eye[:, None, :, None]).reshape(nb * d, nb * e)


def _gate_consts(a_log, dt_bias, i_bias, f_bias):
    row = lambda v, k: jnp.pad(v, (k * HEADS, DK - (k + 1) * HEADS))
    rows = [row(a_log, 1), row(dt_bias, 1), row(i_bias, 2), row(f_bias, 3)]
    return jnp.stack(rows + [jnp.zeros((DK,), f32)] * (8 - len(rows)))


def _state_row(state):
    B, W = state.shape
    return jnp.pad(state[:, None, :], ((0, 0), (TOK0 - 1, GROUP - TOK0), (0, 0))).reshape(B * GROUP, W)


def _tail_rows(U, B, T, col, width):
    return jnp.stack([lax.slice(U, ((b + 1) * T - (CONV_W - 1), col), ((b + 1) * T, col + width))
                      for b in range(B)])


def kernel(x_prompt, x_sample, state_gdn_S, state_gdn_conv, state_mlstm_C, state_mlstm_n, state_mlstm_m, state_lru_h, state_lru_conv, ffn1_wg, ffn1_wu, ffn1_wd, ln_g, ln_b, w_in, gdn_conv_w, gdn_A_log, gdn_dt_bias, gdn_norm_w, mlstm_i_bias, mlstm_f_bias, mlstm_norm_w, lru_conv_w, lru_conv_b, lru_wr, lru_br, lru_wi, lru_bi, lru_lambda, w_out, ffn2_wg, ffn2_wu, ffn2_wd):
    BP, TP, D = x_prompt.shape
    BS, TS, _ = x_sample.shape
    MP, MS = BP * TP, BS * TS
    ffn1 = (ffn1_wg, ffn1_wu, ffn1_wd)
    ffn2 = (ffn2_wg, ffn2_wu, ffn2_wd)
    w_in_p = _permute_w_in(w_in.astype(bf16))
    w_out_b = w_out.astype(bf16)

    xp, xs = x_prompt.reshape(MP, D), x_sample.reshape(MS, D)
    outs_p = [[] for _ in range(7)]
    outs_s = [[] for _ in range(7)]
    gS_s = mC_s = None
    for d in range(DEPTH):
        x1p = _ffn_ln(xp, *ffn1, d, ln_g[d, 0], ln_b[d, 0], tm=PROMPT_ROW_TILE)
        x1s = _ffn_ln(xs, *ffn1, d, ln_g[d, 0], ln_b[d, 0], tm=ROW_TILE)
        U = _inproj(x1p, w_in_p, d)
        hc = _gate_consts(gdn_A_log[d], gdn_dt_bias[d], mlstm_i_bias[d], mlstm_f_bias[d])
        wr = _block_diag(lru_wr[d]).astype(bf16)
        wi = _block_diag(lru_wi[d]).astype(bf16)
        lru_w = (lru_conv_w[d], lru_conv_b[d], wr, lru_br[d], wi, lru_bi[d], lru_lambda[d])

        og_p, gS_p = _gdn_prompt(U, BP, TP, gdn_conv_w[d], hc, gdn_norm_w[d])
        hm_p, mC_p, mn_p, mm_p = _mlstm_prompt(U, BP, TP, hc, mlstm_norm_w[d])
        ol_p, lh_p = _lru_prompt(U, BP, TP, *lru_w)
        outs_p[0].append(gS_p)
        outs_p[1].append(_tail_rows(U, BP, TP, 0, CONV_CH))
        outs_p[2].append(mC_p)
        outs_p[3].append(mn_p.reshape(BP, HEADS, DK))
        outs_p[4].append(mm_p[:, :, 0, 0])
        outs_p[5].append(lh_p.reshape(BP, LRU_W))
        outs_p[6].append(_tail_rows(U, BP, TP, COL_LX, LRU_W))

        Us = _inproj(x1s, w_in_p, d).reshape(BS, TS, D_INP)
        hist = jnp.concatenate([
            state_gdn_conv[d], jnp.zeros((BS, 3, COL_LX - CONV_CH), f32),
            state_lru_conv[d], jnp.zeros((BS, 3, D_INP - COL_LX - LRU_W), f32)], axis=2)
        U8 = jnp.concatenate([jnp.zeros((BS, 1, D_INP), f32), hist, Us], axis=1).reshape(BS * GROUP, D_INP)
        og_s, gS_s = _gdn_sample(U8, state_gdn_S, d, gS_s, gdn_conv_w[d], hc, gdn_norm_w[d])
        hm_s, mC_s, n8, m8 = _mlstm_sample(
            U8, state_mlstm_C, d, mC_s, _state_row(state_mlstm_n[d].reshape(BS, QK)),
            _state_row(jnp.repeat(state_mlstm_m[d], DK, axis=1)), hc, mlstm_norm_w[d])
        ol_s, h8 = _lru_sample(U8, _state_row(state_lru_h[d]), *lru_w)
        toks = lambda a: a.reshape(BS, GROUP, -1)[:, TOK0:].reshape(MS, -1)
        last = lambda a: a.reshape(BS, GROUP, -1)[:, GROUP - 1]
        outs_s[1].append(Us[:, TS - 3:, 0:CONV_CH])
        outs_s[3].append(last(n8).reshape(BS, HEADS, DK))
        outs_s[4].append(last(m8).reshape(BS, HEADS, DK)[:, :, 0])
        outs_s[5].append(last(h8))
        outs_s[6].append(Us[:, TS - 3:, COL_LX:COL_LX + LRU_W])

        x2p = _outproj_ln((og_p, hm_p, ol_p), x1p, w_out_b, d, ln_g[d, 1], ln_b[d, 1])
        x2s = _outproj_ln((toks(og_s), toks(hm_s), toks(ol_s)), x1s, w_out_b, d, ln_g[d, 1], ln_b[d, 1])
        xp = _ffn_ln(x2p, *ffn2, d, ln_g[d, 2], ln_b[d, 2], tm=PROMPT_ROW_TILE)
        xs = _ffn_ln(x2s, *ffn2, d, ln_g[d, 2], ln_b[d, 2], tm=ROW_TILE)

    y_prompt = xp.reshape(BP, TP, D)
    y_sample = xs.reshape(BS, TS, D)
    stack = lambda o: jnp.stack(o)
    return (y_prompt, y_sample,
            *[stack(o) for o in outs_p],
            gS_s, stack(outs_s[1]), mC_s, *[stack(outs_s[i]) for i in range(3, 7)])
```

```python
import functools
import math

import jax
import jax.numpy as jnp
from jax import lax
from jax.experimental import pallas as pl
from jax.experimental.pallas import tpu as pltpu

f32 = jnp.float32
bf16 = jnp.bfloat16

DEPTH = 2
D_MODEL = 2048
D_FF = 5632
HEADS = 6
DK = 128
LRU_W = 512
LRU_BLOCKS = 8
LRU_C = 8.0
CONV_W = 4
CHUNK = 64
ALPHA = (2 * DEPTH) ** 0.25
NORM_EPS = 1e-6
NEG = -1e30

QK = HEADS * DK
CONV_CH = 3 * QK
MIX_W = 4 * QK
CB_Q, CB_K, CB_V, CB_Z = 0, 6, 12, 18
CB_MQ, CB_MK, CB_MV, CB_MO = 24, 30, 36, 42
CB_GDN, CB_MLSTM = 0, 1
CB_LX, CB_LY = 12, 13
CB_GATES = 56
COL_LX = 6144
D_INP = 7296
GROUP = 8
TOK0 = 4
ROW_TILE = 512
PROMPT_ROW_TILE = 1024
SAMPLE_SEQS_PER_STEP = 32

VMEM_LIMIT = 60 * 1024 * 1024


def _cparams(sem):
    return pltpu.CompilerParams(dimension_semantics=sem, vmem_limit_bytes=VMEM_LIMIT)


def _silu(x):
    return x * jax.nn.sigmoid(x)


def _softplus(x):
    return jnp.maximum(x, 0.0) + jnp.log1p(jnp.exp(-jnp.abs(x)))


def _expm1(x):
    u = jnp.exp(x)
    um1 = u - 1.0
    lg = jnp.where(u == 1.0, 1.0, jnp.log(jnp.where(u == 0.0, 1.0, u)))
    return jnp.where(u == 1.0, x, jnp.where(u == 0.0, -1.0, um1 * x / lg))


def _layernorm(y, g, b):
    mu = jnp.mean(y, -1, keepdims=True)
    d = y - mu
    var = jnp.mean(d * d, -1, keepdims=True)
    return d * lax.rsqrt(var + NORM_EPS) * g + b


def _rms(x):
    return x * lax.rsqrt(jnp.mean(x * x, -1, keepdims=True) + NORM_EPS)


def _l2(x):
    return x * lax.rsqrt(jnp.sum(x * x, -1, keepdims=True) + NORM_EPS)


def _colsel(gates, idx):
    lane = lax.broadcasted_iota(jnp.int32, gates.shape, 1)
    return jnp.sum(jnp.where(lane == idx, gates, 0.0), axis=1, keepdims=True)


def _dot(a, b):
    return jnp.dot(a, b, preferred_element_type=f32)


def _bmm(a, b):
    return jnp.einsum('nij,njk->nik', a, b, preferred_element_type=f32)


def _bmm_nt(a, b):
    return jnp.einsum('nid,njd->nij', a, b, preferred_element_type=f32)


def _bmm_tn(a, b):
    return jnp.einsum('nck,ncv->nkv', a, b, preferred_element_type=f32)


def _ij(C):
    return (lax.broadcasted_iota(jnp.int32, (C, C), 0), lax.broadcasted_iota(jnp.int32, (C, C), 1))


def _cumsum_row(col3, ii, jj):
    return jnp.sum(jnp.where((ii <= jj)[None], col3, 0.0), axis=1, keepdims=True)


def _row2col(row3, ii, jj):
    return jnp.sum(jnp.where((ii == jj)[None], row3, 0.0), axis=2, keepdims=True)


def _col2row(col3, ii, jj):
    return jnp.sum(jnp.where((ii == jj)[None], col3, 0.0), axis=1, keepdims=True)


def _conv_carry(x, w_ref, xp_ref, tail):
    T = x.shape[0]
    xp_ref[0:8, :] = tail
    xp_ref[8:T + 8, :] = x
    y = w_ref[3:4, :] * x
    for k in range(CONV_W - 1):
        y = y + w_ref[k:k + 1, :] * xp_ref[pl.ds(5 + k, T), :]
    return y


def _conv_groups(x, w_ref):
    y = w_ref[3:4, :] * x
    for s in range(1, CONV_W):
        y = y + w_ref[3 - s:4 - s, :] * pltpu.roll(x, s, axis=0)
    return y


def _group_pos(shape):
    return lax.broadcasted_iota(jnp.int32, shape, 0) & (GROUP - 1)


def _ffn_ln_kernel(x_ref, wg_ref, wu_ref, wd_ref, g_ref, b_ref, o_ref, xb_ref):
    j = pl.program_id(1)

    @pl.when(j == 0)
    def _():
        o_ref[...] = jnp.zeros_like(o_ref)
        xb_ref[...] = x_ref[...].astype(bf16)

    xb = xb_ref[...]
    hg = _dot(xb, wg_ref[...])
    hu = _dot(xb, wu_ref[...])
    h = (_silu(hg) * hu).astype(bf16)
    o_ref[...] += _dot(h, wd_ref[...])

    @pl.when(j == pl.num_programs(1) - 1)
    def _():
        y = ALPHA * x_ref[...] + 0.5 * o_ref[...]
        o_ref[...] = _layernorm(y, g_ref[...], b_ref[...])


def _ffn_ln(x, wg, wu, wd, d, g, b, *, tm, tf=512):
    M, D = x.shape
    F = wg.shape[2]
    return pl.pallas_call(
        _ffn_ln_kernel,
        grid=(M // tm, F // tf),
        in_specs=[
            pl.BlockSpec((tm, D), lambda i, j: (i, 0), pipeline_mode=pl.Buffered(1)),
            pl.BlockSpec((None, D, tf), lambda i, j: (d, 0, j)),
            pl.BlockSpec((None, D, tf), lambda i, j: (d, 0, j)),
            pl.BlockSpec((None, tf, D), lambda i, j: (d, j, 0)),
            pl.BlockSpec((1, D), lambda i, j: (0, 0)),
            pl.BlockSpec((1, D), lambda i, j: (0, 0)),
        ],
        out_specs=pl.BlockSpec((tm, D), lambda i, j: (i, 0)),
        out_shape=jax.ShapeDtypeStruct((M, D), f32),
        scratch_shapes=[pltpu.VMEM((tm, D), bf16)],
        compiler_params=_cparams(("parallel", "arbitrary")),
        name="ffn_ln",
    )(x, wg, wu, wd, g.reshape(1, D), b.reshape(1, D))


def _inproj_kernel(x_ref, w_ref, o_ref):
    o_ref[...] = _dot(x_ref[...].astype(bf16), w_ref[...])


def _inproj(x, w, d, *, tm=ROW_TILE, tn=2432):
    M, D = x.shape
    N = w.shape[2]
    return pl.pallas_call(
        _inproj_kernel,
        grid=(N // tn, M // tm),
        in_specs=[pl.BlockSpec((tm, D), lambda n, i: (i, 0)),
                  pl.BlockSpec((None, D, tn), lambda n, i: (d, 0, n))],
        out_specs=pl.BlockSpec((tm, tn), lambda n, i: (i, n)),
        out_shape=jax.ShapeDtypeStruct((M, N), f32),
        compiler_params=_cparams(("parallel", "arbitrary")),
        name="in_proj",
    )(x, w)


def _outproj_ln_kernel(og_ref, hm_ref, ol_ref, x_ref, w_ref, g_ref, b_ref, o_ref):
    mix = (_dot(og_ref[...], w_ref[0:QK, :]) + _dot(hm_ref[...], w_ref[QK:2 * QK, :])
           + _dot(ol_ref[...], w_ref[2 * QK:, :]))
    o_ref[...] = _layernorm(ALPHA * x_ref[...] + mix, g_ref[...], b_ref[...])


def _outproj_ln(mix, x, w, d, g, b, *, tm=ROW_TILE):
    M, D = x.shape
    return pl.pallas_call(
        _outproj_ln_kernel,
        grid=(M // tm,),
        in_specs=[pl.BlockSpec((tm, wd), lambda i: (i, 0)) for wd in (QK, QK, LRU_W)]
        + [pl.BlockSpec((tm, D), lambda i: (i, 0)),
           pl.BlockSpec((None, D, D), lambda i: (d, 0, 0)),
           pl.BlockSpec((1, D), lambda i: (0, 0)),
           pl.BlockSpec((1, D), lambda i: (0, 0))],
        out_specs=pl.BlockSpec((tm, D), lambda i: (i, 0)),
        out_shape=jax.ShapeDtypeStruct((M, D), f32),
        compiler_params=_cparams(("parallel",)),
        name="out_proj_ln",
    )(*mix, x, w, g.reshape(1, D), b.reshape(1, D))


def _gdn_gates(gates, gc_ref):
    beta = jax.nn.sigmoid(gates)
    gl = -jnp.exp(gc_ref[0:1, :]) * _softplus(gates + gc_ref[1:2, :])
    return beta, gl


def _mlstm_gates(gates, gc_ref):
    i_pre = gates + gc_ref[2:3, :]
    logf = -_softplus(-(gates + gc_ref[3:4, :]))
    return i_pre, logf


def _gdn_chunk(q3, k3, v3, b3, g3, S):
    C = q3.shape[1]
    ii, jj = _ij(C)
    g_row = _cumsum_row(g3, ii, jj)
    g_col = _row2col(g_row, ii, jj)
    decay = jnp.exp(jnp.where((ii >= jj)[None], g_col - g_row, NEG))
    kb = k3.astype(bf16)
    kk = _bmm_nt(kb, kb)
    p = jnp.where((ii > jj)[None], -(b3 * kk * decay), 0.0)
    e = p
    for _ in range(int(math.log2(C)) - 1):
        pb = p.astype(bf16)
        p = _bmm(pb, pb)
        e = e + p + _bmm(e.astype(bf16), p.astype(bf16))
    eb = e.astype(bf16)
    e_g = jnp.exp(g_col)
    rv = b3 * v3
    rk = (b3 * e_g) * k3
    uv = rv + _bmm(eb, rv.astype(bf16))
    wks = (rk + _bmm(eb, rk.astype(bf16))).astype(bf16)
    qk = (_bmm_nt(q3.astype(bf16), kb) * decay).astype(bf16)
    g_last = g_row[:, :, C - 1:C]
    qd = (q3 * e_g).astype(bf16)
    kt = (k3 * jnp.exp(g_last - g_col)).astype(bf16)
    Sb = S.astype(bf16)
    Ub = (uv - _bmm(wks, Sb)).astype(bf16)
    o = _bmm(qd, Sb) + _bmm(qk, Ub)
    return o, jnp.exp(g_last) * S + _bmm_tn(kt, Ub)


def _mlstm_chunk(q3, k3, vb, i3, f3, Cm, nr, m):
    C = q3.shape[1]
    ii, jj = _ij(C)
    b_row = _cumsum_row(f3, ii, jj)
    b_col = _row2col(b_row, ii, jj)
    i_row = _col2row(i3, ii, jj)
    D = jnp.where((ii >= jj)[None], b_col - b_row + i_row, NEG)
    d_max = jnp.max(D, axis=2, keepdims=True)
    qb = q3.astype(bf16)
    qk = _bmm_nt(qb, k3.astype(bf16))
    m_t = jnp.maximum(b_col + m, d_max)
    inter = jnp.exp(b_col + m - m_t)
    Sw = jnp.exp(D - m_t) * qk
    num = inter * _bmm(qb, Cm.astype(bf16)) + _bmm(Sw.astype(bf16), vb)
    den = inter * jnp.sum(q3 * nr, -1, keepdims=True) + jnp.sum(Sw, -1, keepdims=True)
    hh = num / jnp.maximum(jnp.abs(den), jnp.exp(-m_t))
    m_new = m_t[:, C - 1:C]
    b_last = b_col[:, C - 1:C]
    kw = jnp.exp(b_last - b_col + i3 - m_new) * k3
    dec = jnp.exp(b_last + m - m_new)
    return (hh, dec * Cm + _bmm_tn(kw.astype(bf16), vb), dec * nr + jnp.sum(kw, axis=1, keepdims=True), m_new)


def _gdn_prompt_kernel(*refs, C, B):
    mains, gts = refs[0:2 * B:2], refs[1:2 * B:2]
    cw_ref, gc_ref, nw_ref, o_ref, s_ref, xp_ref, tail_ref = refs[2 * B:]
    n = pl.program_id(0)
    NC = B * HEADS

    @pl.when(n == 0)
    def _():
        tail_ref[...] = jnp.zeros_like(tail_ref)
        s_ref[...] = jnp.zeros_like(s_ref)

    qs, ks, vs, zs, bs, gs = [], [], [], [], [], []
    for b in range(B):
        x = mains[b][:, 0:CONV_CH]
        y = _silu(_conv_carry(x, cw_ref, xp_ref, tail_ref[b]))
        tail_ref[b] = x[C - 8:C]
        beta, gl = _gdn_gates(gts[b][...], gc_ref)
        for h in range(HEADS):
            qs.append(y[:, h * DK:(h + 1) * DK])
            ks.append(y[:, QK + h * DK:QK + (h + 1) * DK])
            vs.append(y[:, 2 * QK + h * DK:2 * QK + (h + 1) * DK])
            zs.append(mains[b][:, CONV_CH + h * DK:CONV_CH + (h + 1) * DK])
            bs.append(beta[:, h:h + 1])
            gs.append(gl[:, HEADS + h:HEADS + h + 1])
    q3 = _l2(jnp.stack(qs)) * (DK ** -0.5)
    k3 = _l2(jnp.stack(ks))
    v3 = jnp.stack(vs)
    b3 = jnp.stack(bs)
    g3 = jnp.stack(gs)
    o, S = _gdn_chunk(q3, k3, v3, b3, g3, s_ref[...].reshape(NC, DK, DK))
    s_ref[...] = S.reshape(B, HEADS, DK, DK)
    o = _rms(o) * nw_ref[...] * _silu(jnp.stack(zs))
    for b in range(B):
        for h in range(HEADS):
            o_ref[b, :, h * DK:(h + 1) * DK] = o[b * HEADS + h].astype(o_ref.dtype)


def _gdn_prompt(U, B, T, conv_w, hc, norm_w, *, C=CHUNK):
    N = T // C
    in_specs = []
    for b in range(B):
        in_specs.append(pl.BlockSpec((C, MIX_W), lambda n, b=b: (b * N + n, CB_GDN)))
        in_specs.append(pl.BlockSpec((C, DK), lambda n, b=b: (b * N + n, CB_GATES)))
    in_specs += [pl.BlockSpec((CONV_W, CONV_CH), lambda n: (0, 0)),
                 pl.BlockSpec((8, DK), lambda n: (0, 0)),
                 pl.BlockSpec((1, DK), lambda n: (0, 0))]
    o, S = pl.pallas_call(
        functools.partial(_gdn_prompt_kernel, C=C, B=B),
        grid=(N,),
        in_specs=in_specs,
        out_specs=[pl.BlockSpec((B, None, C, QK), lambda n: (0, n, 0, 0)),
                   pl.BlockSpec((B, HEADS, DK, DK), lambda n: (0, 0, 0, 0))],
        out_shape=[jax.ShapeDtypeStruct((B, N, C, QK), bf16),
                   jax.ShapeDtypeStruct((B, HEADS, DK, DK), f32)],
        scratch_shapes=[pltpu.VMEM((C + 8, CONV_CH), f32), pltpu.VMEM((B, 8, CONV_CH), f32)],
        compiler_params=_cparams(("arbitrary",)),
        name="gdn_prompt",
    )(*([U, U] * B), conv_w, hc, norm_w.reshape(1, DK))
    return o.reshape(B * T, QK), S


def _mlstm_prompt_kernel(*refs, C, B):
    mains, gts = refs[0:2 * B:2], refs[1:2 * B:2]
    gc_ref, nw_ref, h_ref, c_ref, n_ref, m_ref = refs[2 * B:]
    n = pl.program_id(0)
    NC = B * HEADS

    @pl.when(n == 0)
    def _():
        c_ref[...] = jnp.zeros_like(c_ref)
        n_ref[...] = jnp.zeros_like(n_ref)
        m_ref[...] = jnp.zeros_like(m_ref)

    qs, ks, vs, os_, is_, fs = [], [], [], [], [], []
    for b in range(B):
        i_pre, logf = _mlstm_gates(gts[b][...], gc_ref)
        for h in range(HEADS):
            qs.append(mains[b][:, h * DK:(h + 1) * DK])
            ks.append(mains[b][:, QK + h * DK:QK + (h + 1) * DK])
            vs.append(mains[b][:, 2 * QK + h * DK:2 * QK + (h + 1) * DK])
            os_.append(mains[b][:, 3 * QK + h * DK:3 * QK + (h + 1) * DK])
            is_.append(i_pre[:, 2 * HEADS + h:2 * HEADS + h + 1])
            fs.append(logf[:, 3 * HEADS + h:3 * HEADS + h + 1])
    q3 = jnp.stack(qs)
    k3 = jnp.stack(ks) * (DK ** -0.5)
    vb = jnp.stack(vs).astype(bf16)
    i3 = jnp.stack(is_)
    f3 = jnp.stack(fs)
    hh, Cm, nr, m = _mlstm_chunk(q3, k3, vb, i3, f3, c_ref[...].reshape(NC, DK, DK),
                                 n_ref[...].reshape(NC, 1, DK), m_ref[...].reshape(NC, 1, DK)[:, :, 0:1])
    nw = jnp.stack([nw_ref[h] for _ in range(B) for h in range(HEADS)])
    hh = _rms(hh) * nw * jax.nn.sigmoid(jnp.stack(os_))
    c_ref[...] = Cm.reshape(B, HEADS, DK, DK)
    n_ref[...] = nr.reshape(B, HEADS, 1, DK)
    m_ref[...] = jnp.broadcast_to(m, (NC, 1, DK)).reshape(B, HEADS, 1, DK)
    for b in range(B):
        for h in range(HEADS):
            h_ref[b, :, h * DK:(h + 1) * DK] = hh[b * HEADS + h].astype(h_ref.dtype)


def _mlstm_prompt(U, B, T, hc, norm_w, *, C=CHUNK):
    N = T // C
    in_specs = []
    for b in range(B):
        in_specs.append(pl.BlockSpec((C, MIX_W), lambda n, b=b: (b * N + n, CB_MLSTM)))
        in_specs.append(pl.BlockSpec((C, DK), lambda n, b=b: (b * N + n, CB_GATES)))
    in_specs += [pl.BlockSpec((8, DK), lambda n: (0, 0)),
                 pl.BlockSpec((HEADS, 1, DK), lambda n: (0, 0, 0))]
    st = lambda r: pl.BlockSpec((B, HEADS, r, DK), lambda n: (0, 0, 0, 0))
    h, Cm, nr, m = pl.pallas_call(
        functools.partial(_mlstm_prompt_kernel, C=C, B=B),
        grid=(N,),
        in_specs=in_specs,
        out_specs=[pl.BlockSpec((B, None, C, QK), lambda n: (0, n, 0, 0)), st(DK), st(1), st(1)],
        out_shape=[jax.ShapeDtypeStruct((B, N, C, QK), bf16),
                   jax.ShapeDtypeStruct((B, HEADS, DK, DK), f32),
                   jax.ShapeDtypeStruct((B, HEADS, 1, DK), f32),
                   jax.ShapeDtypeStruct((B, HEADS, 1, DK), f32)],
        compiler_params=_cparams(("arbitrary",)),
        name="mlstm_prompt",
    )(*([U, U] * B), hc, norm_w.reshape(HEADS, 1, DK))
    return h.reshape(B * T, QK), Cm, nr, m


def _lru_gates(c, wr_ref, br_ref, wi_ref, bi_ref, lam_ref):
    cb = c.astype(bf16)
    r = jax.nn.sigmoid(_dot(cb, wr_ref[...]) + br_ref[...])
    ig = jax.nn.sigmoid(_dot(cb, wi_ref[...]) + bi_ref[...])
    log_a = -LRU_C * r * _softplus(-lam_ref[...])
    a = jnp.exp(log_a)
    u = jnp.sqrt(-_expm1(2.0 * log_a)) * (ig * c)
    return a, u


def _lru_prompt_kernel(x_ref, y_ref, cw_ref, cb_ref, wr_ref, br_ref, wi_ref, bi_ref, lam_ref,
                       o_ref, hl_ref, xp_ref, tail_ref, hc_ref):
    tc = pl.program_id(1)
    Tc, W = x_ref.shape

    @pl.when(tc == 0)
    def _():
        tail_ref[...] = jnp.zeros_like(tail_ref)
        hc_ref[...] = jnp.zeros_like(hc_ref)

    x = x_ref[...]
    c = _conv_carry(x, cw_ref, xp_ref, tail_ref[...]) + cb_ref[...]
    tail_ref[...] = x[Tc - 8:Tc]
    a, u = _lru_gates(c, wr_ref, br_ref, wi_ref, bi_ref, lam_ref)
    row = lax.broadcasted_iota(jnp.int32, (Tc, W), 0)
    d = 1
    while d < Tc:
        keep = row >= d
        a_sh = jnp.where(keep, pltpu.roll(a, d, axis=0), 1.0)
        u_sh = jnp.where(keep, pltpu.roll(u, d, axis=0), 0.0)
        u = a * u_sh + u
        a = a * a_sh
        d *= 2
    hs = u + a * hc_ref[0:1, :]
    last = hs[Tc - 1:Tc]
    hc_ref[...] = jnp.broadcast_to(last, hc_ref.shape)
    hl_ref[...] = last
    o_ref[...] = (hs * jax.nn.gelu(y_ref[...])).astype(o_ref.dtype)


def _lru_prompt(U, B, T, cw, cb, wr, br, wi, bi, lam, *, Tc=512):
    W = LRU_W
    nt = T // Tc
    vec = lambda: pl.BlockSpec((1, W), lambda b, t: (0, 0))
    return pl.pallas_call(
        _lru_prompt_kernel,
        grid=(B, nt),
        in_specs=[pl.BlockSpec((Tc, W), lambda b, t: (b * nt + t, CB_LX)),
                  pl.BlockSpec((Tc, W), lambda b, t: (b * nt + t, CB_LY)),
                  pl.BlockSpec((CONV_W, W), lambda b, t: (0, 0)), vec(),
                  pl.BlockSpec((W, W), lambda b, t: (0, 0)), vec(),
                  pl.BlockSpec((W, W), lambda b, t: (0, 0)), vec(), vec()],
        out_specs=[pl.BlockSpec((Tc, W), lambda b, t: (b * nt + t, 0)),
                   pl.BlockSpec((None, 1, W), lambda b, t: (b, 0, 0))],
        out_shape=[jax.ShapeDtypeStruct((B * T, W), bf16),
                   jax.ShapeDtypeStruct((B, 1, W), f32)],
        scratch_shapes=[pltpu.VMEM((Tc + 8, W), f32), pltpu.VMEM((8, W), f32), pltpu.VMEM((8, W), f32)],
        compiler_params=_cparams(("parallel", "arbitrary")),
        name="lru_prompt",
    )(U, U, cw, cb.reshape(1, W), wr, br.reshape(1, W), wi, bi.reshape(1, W), lam.reshape(1, W))


def _gdn_sample_kernel(q_ref, k_ref, v_ref, z_ref, gt_ref, wq_ref, wk_ref, wv_ref, gc_ref, nw_ref, s0_ref,
                       *rest):
    o_ref, s_ref = rest[-2:]
    h = pl.program_id(0)
    BB = s0_ref.shape[0]
    R = BB * GROUP
    q = _l2(_silu(_conv_groups(q_ref[...], wq_ref))) * (DK ** -0.5)
    k = _l2(_silu(_conv_groups(k_ref[...], wk_ref)))
    v = _silu(_conv_groups(v_ref[...], wv_ref))
    beta, gl = _gdn_gates(gt_ref[...], gc_ref)
    is_tok = _group_pos((R, 1)) >= TOK0
    beta = jnp.where(is_tok, _colsel(beta, h), 0.0)
    gl = jnp.where(is_tok, _colsel(gl, HEADS + h), 0.0)
    grp = lambda a: a.reshape(BB, GROUP, a.shape[-1])
    o, S = _gdn_chunk(grp(q), grp(k), grp(v), grp(beta), grp(gl), s0_ref[...])
    s_ref[...] = S
    o_ref[...] = (_rms(o.reshape(R, DK)) * nw_ref[...] * _silu(z_ref[...])).astype(o_ref.dtype)


def _stacked_out(prev):
    if prev is None:
        return [], [], {}
    return [pl.BlockSpec(memory_space=pl.ANY)], [prev], None


def _gdn_sample(U8, S_all, d, S_prev, conv_w, hc, norm_w, *, BB=SAMPLE_SEQS_PER_STEP):
    B = S_all.shape[1]
    R = BB * GROUP
    tok = lambda cb: pl.BlockSpec((R, DK), lambda h, b: (b, cb + h))
    cw = lambda cb: pl.BlockSpec((CONV_W, DK), lambda h, b: (0, cb + h))
    st = pl.BlockSpec((None, BB, None, DK, DK), lambda h, b: (d, b, h, 0, 0))
    extra_specs, extra_ops, _ = _stacked_out(S_prev)
    n_in = 11
    return pl.pallas_call(
        _gdn_sample_kernel,
        grid=(HEADS, B // BB),
        in_specs=[tok(CB_Q), tok(CB_K), tok(CB_V), tok(CB_Z),
                  pl.BlockSpec((R, DK), lambda h, b: (b, CB_GATES)),
                  cw(CB_Q), cw(CB_K), cw(CB_V),
                  pl.BlockSpec((8, DK), lambda h, b: (0, 0)),
                  pl.BlockSpec((1, DK), lambda h, b: (0, 0)), st] + extra_specs,
        out_specs=[pl.BlockSpec((R, DK), lambda h, b: (b, h)), st],
        out_shape=[jax.ShapeDtypeStruct((B * GROUP, QK), bf16),
                   jax.ShapeDtypeStruct(S_all.shape, f32)],
        input_output_aliases={n_in: 1} if S_prev is not None else {},
        compiler_params=_cparams(("parallel", "parallel")),
        name="gdn_sample",
    )(U8, U8, U8, U8, U8, conv_w, conv_w, conv_w, hc, norm_w.reshape(1, DK), S_all, *extra_ops)


def _mlstm_sample_kernel(q_ref, k_ref, v_ref, og_ref, gt_ref, gc_ref, nw_ref, c0_ref, n0_ref, m0_ref,
                         *rest):
    h_ref, c_ref, n_ref, m_ref = rest[-4:]
    h = pl.program_id(0)
    BB = c0_ref.shape[0]
    R = BB * GROUP
    i_pre, logf = _mlstm_gates(gt_ref[...], gc_ref)
    is_tok = _group_pos((R, 1)) >= TOK0
    i_pre = jnp.where(is_tok, _colsel(i_pre, 2 * HEADS + h), NEG)
    logf = jnp.where(is_tok, _colsel(logf, 3 * HEADS + h), 0.0)
    grp = lambda a: a.reshape(BB, GROUP, a.shape[-1])
    old = lambda ref: grp(ref[...])[:, TOK0 - 1:TOK0, :]
    hh, Cm, nr, m = _mlstm_chunk(grp(q_ref[...]), grp(k_ref[...] * (DK ** -0.5)), grp(v_ref[...]).astype(bf16),
                                 grp(i_pre), grp(logf), c0_ref[...], old(n0_ref), old(m0_ref)[:, :, 0:1])
    c_ref[...] = Cm
    n_ref[...] = jnp.broadcast_to(nr, (BB, GROUP, DK)).reshape(R, DK)
    m_ref[...] = jnp.broadcast_to(m, (BB, GROUP, DK)).reshape(R, DK)
    h_ref[...] = (_rms(hh.reshape(R, DK)) * nw_ref[h] * jax.nn.sigmoid(og_ref[...])).astype(h_ref.dtype)


def _mlstm_sample(U8, C_all, d, C_prev, n0g, m0g, hc, norm_w, *, BB=SAMPLE_SEQS_PER_STEP):
    B = C_all.shape[1]
    R = BB * GROUP
    tok = lambda cb: pl.BlockSpec((R, DK), lambda h, b: (b, cb + h))
    row = pl.BlockSpec((R, DK), lambda h, b: (b, h))
    st = pl.BlockSpec((None, BB, None, DK, DK), lambda h, b: (d, b, h, 0, 0))
    extra_specs, extra_ops, _ = _stacked_out(C_prev)
    n_in = 10
    return pl.pallas_call(
        _mlstm_sample_kernel,
        grid=(HEADS, B // BB),
        in_specs=[tok(CB_MQ), tok(CB_MK), tok(CB_MV), tok(CB_MO),
                  pl.BlockSpec((R, DK), lambda h, b: (b, CB_GATES)),
                  pl.BlockSpec((8, DK), lambda h, b: (0, 0)),
                  pl.BlockSpec((HEADS, 1, DK), lambda h, b: (0, 0, 0)),
                  st, row, row] + extra_specs,
        out_specs=[row, st, row, row],
        out_shape=[jax.ShapeDtypeStruct((B * GROUP, QK), bf16),
                   jax.ShapeDtypeStruct(C_all.shape, f32),
                   jax.ShapeDtypeStruct((B * GROUP, QK), f32),
                   jax.ShapeDtypeStruct((B * GROUP, QK), f32)],
        input_output_aliases={n_in: 1} if C_prev is not None else {},
        compiler_params=_cparams(("parallel", "parallel")),
        name="mlstm_sample",
    )(U8, U8, U8, U8, U8, hc, norm_w.reshape(HEADS, 1, DK), C_all, n0g, m0g, *extra_ops)


def _lru_sample_kernel(x_ref, y_ref, h0_ref, cw_ref, cb_ref, wr_ref, br_ref, wi_ref, bi_ref, lam_ref,
                       o_ref, h_ref):
    c = _conv_groups(x_ref[...], cw_ref) + cb_ref[...]
    a, u = _lru_gates(c, wr_ref, br_ref, wi_ref, bi_ref, lam_ref)
    hs = h0_ref[...]
    pos = _group_pos(hs.shape)
    for t in range(GROUP - TOK0):
        hs = jnp.where(pos == TOK0 + t, a * pltpu.roll(hs, 1, axis=0) + u, hs)
    h_ref[...] = hs
    o_ref[...] = (hs * jax.nn.gelu(y_ref[...])).astype(o_ref.dtype)


def _lru_sample(U8, h0g, cw, cb, wr, br, wi, bi, lam, *, R=128):
    W = LRU_W
    rows = U8.shape[0]
    vec = lambda: pl.BlockSpec((1, W), lambda i: (0, 0))
    return pl.pallas_call(
        _lru_sample_kernel,
        grid=(rows // R,),
        in_specs=[pl.BlockSpec((R, W), lambda i: (i, CB_LX)),
                  pl.BlockSpec((R, W), lambda i: (i, CB_LY)),
                  pl.BlockSpec((R, W), lambda i: (i, 0)),
                  pl.BlockSpec((CONV_W, W), lambda i: (0, 0)), vec(),
                  pl.BlockSpec((W, W), lambda i: (0, 0)), vec(),
                  pl.BlockSpec((W, W), lambda i: (0, 0)), vec(), vec()],
        out_specs=[pl.BlockSpec((R, W), lambda i: (i, 0)), pl.BlockSpec((R, W), lambda i: (i, 0))],
        out_shape=[jax.ShapeDtypeStruct((rows, W), bf16), jax.ShapeDtypeStruct((rows, W), f32)],
        compiler_params=_cparams(("parallel",)),
        name="lru_sample",
    )(U8, U8, h0g, cw, cb.reshape(1, W), wr, br.reshape(1, W), wi, bi.reshape(1, W), lam.reshape(1, W))


def _permute_w_in(w_in):
    s = {}
    off = 0
    for name, size in (("qkv", CONV_CH), ("z", QK), ("gb", HEADS), ("ga", HEADS), ("mq", QK), ("mk", QK),
                       ("mv", QK), ("mo", QK), ("mi", HEADS), ("mf", HEADS), ("lx", LRU_W), ("ly", LRU_W)):
        s[name] = w_in[..., off:off + size]
        off += size
    pad = jnp.zeros(w_in.shape[:-1] + (DK - 4 * HEADS,), w_in.dtype)
    return jnp.concatenate([s["qkv"], s["z"], s["mq"], s["mk"], s["mv"], s["mo"], s["lx"], s["ly"],
                            s["gb"], s["ga"], s["mi"], s["mf"], pad], axis=-1)


def _block_diag(w):
    nb, d, e = w.shape
    eye = jnp.eye(nb, dtype=w.dtype)
    return (w[:, :, None, :] * eye[:, None, :, None]).reshape(nb * d, nb * e)


def _gate_consts(a_log, dt_bias, i_bias, f_bias):
    row = lambda v, k: jnp.pad(v, (k * HEADS, DK - (k + 1) * HEADS))
    rows = [row(a_log, 1), row(dt_bias, 1), row(i_bias, 2), row(f_bias, 3)]
    return jnp.stack(rows + [jnp.zeros((DK,), f32)] * (8 - len(rows)))


def _state_row(state):
    B, W = state.shape
    return jnp.pad(state[:, None, :], ((0, 0), (TOK0 - 1, GROUP - TOK0), (0, 0))).reshape(B * GROUP, W)


def _tail_rows(U, B, T, col, width):
    return jnp.stack([lax.slice(U, ((b + 1) * T - (CONV_W - 1), col), ((b + 1) * T, col + width))
                      for b in range(B)])


def kernel(x_prompt, x_sample, state_gdn_S, state_gdn_conv, state_mlstm_C, state_mlstm_n, state_mlstm_m, state_lru_h, state_lru_conv, ffn1_wg, ffn1_wu, ffn1_wd, ln_g, ln_b, w_in, gdn_conv_w, gdn_A_log, gdn_dt_bias, gdn_norm_w, mlstm_i_bias, mlstm_f_bias, mlstm_norm_w, lru_conv_w, lru_conv_b, lru_wr, lru_br, lru_wi, lru_bi, lru_lambda, w_out, ffn2_wg, ffn2_wu, ffn2_wd):
    BP, TP, D = x_prompt.shape
    BS, TS, _ = x_sample.shape
    MP, MS = BP * TP, BS * TS
    ffn1 = tuple(w.astype(bf16) for w in (ffn1_wg, ffn1_wu, ffn1_wd))
    ffn2 = tuple(w.astype(bf16) for w in (ffn2_wg, ffn2_wu, ffn2_wd))
    w_in_p = _permute_w_in(w_in.astype(bf16))
    w_out_b = w_out.astype(bf16)

    xp, xs = x_prompt.reshape(MP, D), x_sample.reshape(MS, D)
    outs_p = [[] for _ in range(7)]
    outs_s = [[] for _ in range(7)]
    gS_s = mC_s = None
    for d in range(DEPTH):
        x1p = _ffn_ln(xp, *ffn1, d, ln_g[d, 0], ln_b[d, 0], tm=PROMPT_ROW_TILE)
        x1s = _ffn_ln(xs, *ffn1, d, ln_g[d, 0], ln_b[d, 0], tm=ROW_TILE)
        U = _inproj(x1p, w_in_p, d)
        hc = _gate_consts(gdn_A_log[d], gdn_dt_bias[d], mlstm_i_bias[d], mlstm_f_bias[d])
        wr = _block_diag(lru_wr[d]).astype(bf16)
        wi = _block_diag(lru_wi[d]).astype(bf16)
        lru_w = (lru_conv_w[d], lru_conv_b[d], wr, lru_br[d], wi, lru_bi[d], lru_lambda[d])

        og_p, gS_p = _gdn_prompt(U, BP, TP, gdn_conv_w[d], hc, gdn_norm_w[d])
        hm_p, mC_p, mn_p, mm_p = _mlstm_prompt(U, BP, TP, hc, mlstm_norm_w[d])
        ol_p, lh_p = _lru_prompt(U, BP, TP, *lru_w)
        outs_p[0].append(gS_p)
        outs_p[1].append(_tail_rows(U, BP, TP, 0, CONV_CH))
        outs_p[2].append(mC_p)
        outs_p[3].append(mn_p.reshape(BP, HEADS, DK))
        outs_p[4].append(mm_p[:, :, 0, 0])
        outs_p[5].append(lh_p.reshape(BP, LRU_W))
        outs_p[6].append(_tail_rows(U, BP, TP, COL_LX, LRU_W))

        Us = _inproj(x1s, w_in_p, d).reshape(BS, TS, D_INP)
        hist = jnp.concatenate([
            state_gdn_conv[d], jnp.zeros((BS, 3, COL_LX - CONV_CH), f32),
            state_lru_conv[d], jnp.zeros((BS, 3, D_INP - COL_LX - LRU_W), f32)], axis=2)
        U8 = jnp.concatenate([jnp.zeros((BS, 1, D_INP), f32), hist, Us], axis=1).reshape(BS * GROUP, D_INP)
        og_s, gS_s = _gdn_sample(U8, state_gdn_S, d, gS_s, gdn_conv_w[d], hc, gdn_norm_w[d])
        hm_s, mC_s, n8, m8 = _mlstm_sample(
            U8, state_mlstm_C, d, mC_s, _state_row(state_mlstm_n[d].reshape(BS, QK)),
            _state_row(jnp.repeat(state_mlstm_m[d], DK, axis=1)), hc, mlstm_norm_w[d])
        ol_s, h8 = _lru_sample(U8, _state_row(state_lru_h[d]), *lru_w)
        toks = lambda a: a.reshape(BS, GROUP, -1)[:, TOK0:].reshape(MS, -1)
        last = lambda a: a.reshape(BS, GROUP, -1)[:, GROUP - 1]
        outs_s[1].append(Us[:, TS - 3:, 0:CONV_CH])
        outs_s[3].append(last(n8).reshape(BS, HEADS, DK))
        outs_s[4].append(last(m8).reshape(BS, HEADS, DK)[:, :, 0])
        outs_s[5].append(last(h8))
        outs_s[6].append(Us[:, TS - 3:, COL_LX:COL_LX + LRU_W])

        x2p = _outproj_ln((og_p, hm_p, ol_p), x1p, w_out_b, d, ln_g[d, 1], ln_b[d, 1])
        x2s = _outproj_ln((toks(og_s), toks(hm_s), toks(ol_s)), x1s, w_out_b, d, ln_g[d, 1], ln_b[d, 1])
        xp = _ffn_ln(x2p, *ffn2, d, ln_g[d, 2], ln_b[d, 2], tm=PROMPT_ROW_TILE)
        xs = _ffn_ln(x2s, *ffn2, d, ln_g[d, 2], ln_b[d, 2], tm=ROW_TILE)

    y_prompt = xp.reshape(BP, TP, D)
    y_sample = xs.reshape(BS, TS, D)
    stack = lambda o: jnp.stack(o)
    return (y_prompt, y_sample,
            *[stack(o) for o in outs_p],
            gS_s, stack(outs_s[1]), mC_s, *[stack(outs_s[i]) for i in range(3, 7)])
```

```python
import functools
import math

import jax
import jax.numpy as jnp
from jax import lax
from jax.experimental import pallas as pl
from jax.experimental.pallas import tpu as pltpu

f32 = jnp.float32
bf16 = jnp.bfloat16

DEPTH = 2
D_MODEL = 2048
D_FF = 5632
HEADS = 6
DK = 128
LRU_W = 512
LRU_BLOCKS = 8
LRU_C = 8.0
CONV_W = 4
CHUNK = 64
ALPHA = (2 * DEPTH) ** 0.25
NORM_EPS = 1e-6
NEG = -1e30

QK = HEADS * DK
CONV_CH = 3 * QK
MIX_W = 4 * QK
CB_Q, CB_K, CB_V, CB_Z = 0, 6, 12, 18
CB_MQ, CB_MK, CB_MV, CB_MO = 24, 30, 36, 42
CB_GDN, CB_MLSTM = 0, 1
CB_LX, CB_LY = 12, 13
CB_GATES = 56
COL_LX = 6144
D_INP = 7296
GROUP = 8
TOK0 = 4
ROW_TILE = 512
PROMPT_ROW_TILE = 1024
SAMPLE_SEQS_PER_STEP = 32

VMEM_LIMIT = 60 * 1024 * 1024


def _cparams(sem):
    return pltpu.CompilerParams(dimension_semantics=sem, vmem_limit_bytes=VMEM_LIMIT)


def _silu(x):
    return x * jax.nn.sigmoid(x)


def _softplus(x):
    return jnp.maximum(x, 0.0) + jnp.log1p(jnp.exp(-jnp.abs(x)))


def _expm1(x):
    u = jnp.exp(x)
    um1 = u - 1.0
    lg = jnp.where(u == 1.0, 1.0, jnp.log(jnp.where(u == 0.0, 1.0, u)))
    return jnp.where(u == 1.0, x, jnp.where(u == 0.0, -1.0, um1 * x / lg))


def _layernorm(y, g, b):
    mu = jnp.mean(y, -1, keepdims=True)
    d = y - mu
    var = jnp.mean(d * d, -1, keepdims=True)
    return d * lax.rsqrt(var + NORM_EPS) * g + b


def _rms(x):
    return x * lax.rsqrt(jnp.mean(x * x, -1, keepdims=True) + NORM_EPS)


def _l2(x):
    return x * lax.rsqrt(jnp.sum(x * x, -1, keepdims=True) + NORM_EPS)


def _colsel(gates, idx):
    lane = lax.broadcasted_iota(jnp.int32, gates.shape, 1)
    return jnp.sum(jnp.where(lane == idx, gates, 0.0), axis=1, keepdims=True)


def _dot(a, b):
    return jnp.dot(a, b, preferred_element_type=f32)


def _bmm(a, b):
    return jnp.einsum('nij,njk->nik', a, b, preferred_element_type=f32)


def _bmm_nt(a, b):
    return jnp.einsum('nid,njd->nij', a, b, preferred_element_type=f32)


def _bmm_tn(a, b):
    return jnp.einsum('nck,ncv->nkv', a, b, preferred_element_type=f32)


def _ij(C):
    return (lax.broadcasted_iota(jnp.int32, (C, C), 0), lax.broadcasted_iota(jnp.int32, (C, C), 1))


def _cumsum_row(col3, ii, jj):
    return jnp.sum(jnp.where((ii <= jj)[None], col3, 0.0), axis=1, keepdims=True)


def _row2col(row3, ii, jj):
    return jnp.sum(jnp.where((ii == jj)[None], row3, 0.0), axis=2, keepdims=True)


def _col2row(col3, ii, jj):
    return jnp.sum(jnp.where((ii == jj)[None], col3, 0.0), axis=1, keepdims=True)


def _conv_carry(x, w_ref, xp_ref, tail):
    T = x.shape[0]
    xp_ref[0:8, :] = tail
    xp_ref[8:T + 8, :] = x
    y = w_ref[3:4, :] * x
    for k in range(CONV_W - 1):
        y = y + w_ref[k:k + 1, :] * xp_ref[pl.ds(5 + k, T), :]
    return y


def _conv_groups(x, w_ref):
    y = w_ref[3:4, :] * x
    for s in range(1, CONV_W):
        y = y + w_ref[3 - s:4 - s, :] * pltpu.roll(x, s, axis=0)
    return y


def _group_pos(shape):
    return lax.broadcasted_iota(jnp.int32, shape, 0) & (GROUP - 1)


def _ffn_ln_kernel(*refs, cast_weights, aliased):
    x_ref, wg_ref, wu_ref, wd_ref, g_ref, b_ref = refs[:6]
    n_in = 7 if aliased else 6
    o_ref = refs[n_in]
    xb_ref = refs[-1]
    j = pl.program_id(1)

    @pl.when(j == 0)
    def _():
        o_ref[...] = jnp.zeros_like(o_ref)
        xb_ref[...] = x_ref[...].astype(bf16)

    if cast_weights:
        wg, wu, wd = (r[...].astype(bf16) for r in (wg_ref, wu_ref, wd_ref))
        for w_out_ref, w in zip(refs[n_in + 1:n_in + 4], (wg, wu, wd)):
            w_out_ref[...] = w
    else:
        wg, wu, wd = wg_ref[...], wu_ref[...], wd_ref[...]
    xb = xb_ref[...]
    h = (_silu(_dot(xb, wg)) * _dot(xb, wu)).astype(bf16)
    o_ref[...] += _dot(h, wd)

    @pl.when(j == pl.num_programs(1) - 1)
    def _():
        y = ALPHA * x_ref[...] + 0.5 * o_ref[...]
        o_ref[...] = _layernorm(y, g_ref[...], b_ref[...])


def _ffn_ln(x, wg, wu, wd, g, b, *, tm, tf=512, row0=0, prev=None):
    M, D = x.shape
    F = wg.shape[1]
    rows = lambda i, j: (i + row0, 0)
    in_specs = [
        pl.BlockSpec((tm, D), rows, pipeline_mode=pl.Buffered(1)),
        pl.BlockSpec((D, tf), lambda i, j: (0, j)),
        pl.BlockSpec((D, tf), lambda i, j: (0, j)),
        pl.BlockSpec((tf, D), lambda i, j: (j, 0)),
        pl.BlockSpec((1, D), lambda i, j: (0, 0)),
        pl.BlockSpec((1, D), lambda i, j: (0, 0)),
    ]
    operands = [x, wg, wu, wd, g.reshape(1, D), b.reshape(1, D)]
    if prev is not None:
        in_specs.append(pl.BlockSpec(memory_space=pl.ANY))
        operands.append(prev)
    return pl.pallas_call(
        functools.partial(_ffn_ln_kernel, cast_weights=False, aliased=prev is not None),
        grid=(M // tm - row0, F // tf),
        in_specs=in_specs,
        out_specs=pl.BlockSpec((tm, D), rows),
        out_shape=jax.ShapeDtypeStruct((M, D), f32),
        input_output_aliases={6: 0} if prev is not None else {},
        scratch_shapes=[pltpu.VMEM((tm, D), bf16)],
        compiler_params=_cparams(("parallel", "arbitrary")),
        name="ffn_ln",
    )(*operands)


def _ffn_ln_head(x, wg, wu, wd, d, g, b, *, tm, tf=256):
    M, D = x.shape
    F = wg.shape[2]
    return pl.pallas_call(
        functools.partial(_ffn_ln_kernel, cast_weights=True, aliased=False),
        grid=(1, F // tf),
        in_specs=[
            pl.BlockSpec((tm, D), lambda i, j: (0, 0), pipeline_mode=pl.Buffered(1)),
            pl.BlockSpec((None, D, tf), lambda i, j: (d, 0, j)),
            pl.BlockSpec((None, D, tf), lambda i, j: (d, 0, j)),
            pl.BlockSpec((None, tf, D), lambda i, j: (d, j, 0)),
            pl.BlockSpec((1, D), lambda i, j: (0, 0)),
            pl.BlockSpec((1, D), lambda i, j: (0, 0)),
        ],
        out_specs=[pl.BlockSpec((tm, D), lambda i, j: (0, 0)),
                   pl.BlockSpec((D, tf), lambda i, j: (0, j)),
                   pl.BlockSpec((D, tf), lambda i, j: (0, j)),
                   pl.BlockSpec((tf, D), lambda i, j: (j, 0))],
        out_shape=[jax.ShapeDtypeStruct((M, D), f32), jax.ShapeDtypeStruct((D, F), bf16),
                   jax.ShapeDtypeStruct((D, F), bf16), jax.ShapeDtypeStruct((F, D), bf16)],
        scratch_shapes=[pltpu.VMEM((tm, D), bf16)],
        compiler_params=_cparams(("arbitrary", "arbitrary")),
        name="ffn_ln_head",
    )(x, wg, wu, wd, g.reshape(1, D), b.reshape(1, D))


def _ffn_ln_both(xp, xs, w32, d, g, b):
    out, *wb = _ffn_ln_head(xp, *w32, d, g, b, tm=PROMPT_ROW_TILE)
    yp = _ffn_ln(xp, *wb, g, b, tm=PROMPT_ROW_TILE, row0=1, prev=out)
    ys = _ffn_ln(xs, *wb, g, b, tm=ROW_TILE)
    return yp, ys


def _inproj_kernel(x_ref, w_ref, o_ref):
    o_ref[...] = _dot(x_ref[...].astype(bf16), w_ref[...])


def _inproj(x, w, d, *, tm=ROW_TILE, tn=2432):
    M, D = x.shape
    N = w.shape[2]
    return pl.pallas_call(
        _inproj_kernel,
        grid=(N // tn, M // tm),
        in_specs=[pl.BlockSpec((tm, D), lambda n, i: (i, 0)),
                  pl.BlockSpec((None, D, tn), lambda n, i: (d, 0, n))],
        out_specs=pl.BlockSpec((tm, tn), lambda n, i: (i, n)),
        out_shape=jax.ShapeDtypeStruct((M, N), f32),
        compiler_params=_cparams(("parallel", "arbitrary")),
        name="in_proj",
    )(x, w)


def _outproj_ln_kernel(og_ref, hm_ref, ol_ref, x_ref, w_ref, g_ref, b_ref, o_ref):
    mix = (_dot(og_ref[...], w_ref[0:QK, :]) + _dot(hm_ref[...], w_ref[QK:2 * QK, :])
           + _dot(ol_ref[...], w_ref[2 * QK:, :]))
    o_ref[...] = _layernorm(ALPHA * x_ref[...] + mix, g_ref[...], b_ref[...])


def _outproj_ln(mix, x, w, d, g, b, *, tm=ROW_TILE):
    M, D = x.shape
    return pl.pallas_call(
        _outproj_ln_kernel,
        grid=(M // tm,),
        in_specs=[pl.BlockSpec((tm, wd), lambda i: (i, 0)) for wd in (QK, QK, LRU_W)]
        + [pl.BlockSpec((tm, D), lambda i: (i, 0)),
           pl.BlockSpec((None, D, D), lambda i: (d, 0, 0)),
           pl.BlockSpec((1, D), lambda i: (0, 0)),
           pl.BlockSpec((1, D), lambda i: (0, 0))],
        out_specs=pl.BlockSpec((tm, D), lambda i: (i, 0)),
        out_shape=jax.ShapeDtypeStruct((M, D), f32),
        compiler_params=_cparams(("parallel",)),
        name="out_proj_ln",
    )(*mix, x, w, g.reshape(1, D), b.reshape(1, D))


def _gdn_gates(gates, gc_ref):
    beta = jax.nn.sigmoid(gates)
    gl = -jnp.exp(gc_ref[0:1, :]) * _softplus(gates + gc_ref[1:2, :])
    return beta, gl


def _mlstm_gates(gates, gc_ref):
    i_pre = gates + gc_ref[2:3, :]
    logf = -_softplus(-(gates + gc_ref[3:4, :]))
    return i_pre, logf


def _gdn_chunk(q3, k3, v3, b3, g3, S):
    C = q3.shape[1]
    ii, jj = _ij(C)
    g_row = _cumsum_row(g3, ii, jj)
    g_col = _row2col(g_row, ii, jj)
    decay = jnp.exp(jnp.where((ii >= jj)[None], g_col - g_row, NEG))
    kb = k3.astype(bf16)
    kk = _bmm_nt(kb, kb)
    p = jnp.where((ii > jj)[None], -(b3 * kk * decay), 0.0)
    e = p
    for _ in range(int(math.log2(C)) - 1):
        pb = p.astype(bf16)
        p = _bmm(pb, pb)
        e = e + p + _bmm(e.astype(bf16), p.astype(bf16))
    eb = e.astype(bf16)
    e_g = jnp.exp(g_col)
    rv = b3 * v3
    rk = (b3 * e_g) * k3
    uv = rv + _bmm(eb, rv.astype(bf16))
    wks = (rk + _bmm(eb, rk.astype(bf16))).astype(bf16)
    qk = (_bmm_nt(q3.astype(bf16), kb) * decay).astype(bf16)
    g_last = g_row[:, :, C - 1:C]
    qd = (q3 * e_g).astype(bf16)
    kt = (k3 * jnp.exp(g_last - g_col)).astype(bf16)
    Sb = S.astype(bf16)
    Ub = (uv - _bmm(wks, Sb)).astype(bf16)
    o = _bmm(qd, Sb) + _bmm(qk, Ub)
    return o, jnp.exp(g_last) * S + _bmm_tn(kt, Ub)


def _mlstm_chunk(q3, k3, vb, i3, f3, Cm, nr, m):
    C = q3.shape[1]
    ii, jj = _ij(C)
    b_row = _cumsum_row(f3, ii, jj)
    b_col = _row2col(b_row, ii, jj)
    i_row = _col2row(i3, ii, jj)
    D = jnp.where((ii >= jj)[None], b_col - b_row + i_row, NEG)
    d_max = jnp.max(D, axis=2, keepdims=True)
    qb = q3.astype(bf16)
    qk = _bmm_nt(qb, k3.astype(bf16))
    m_t = jnp.maximum(b_col + m, d_max)
    inter = jnp.exp(b_col + m - m_t)
    Sw = jnp.exp(D - m_t) * qk
    num = inter * _bmm(qb, Cm.astype(bf16)) + _bmm(Sw.astype(bf16), vb)
    den = inter * jnp.sum(q3 * nr, -1, keepdims=True) + jnp.sum(Sw, -1, keepdims=True)
    hh = num / jnp.maximum(jnp.abs(den), jnp.exp(-m_t))
    m_new = m_t[:, C - 1:C]
    b_last = b_col[:, C - 1:C]
    kw = jnp.exp(b_last - b_col + i3 - m_new) * k3
    dec = jnp.exp(b_last + m - m_new)
    return (hh, dec * Cm + _bmm_tn(kw.astype(bf16), vb), dec * nr + jnp.sum(kw, axis=1, keepdims=True), m_new)


def _gdn_prompt_kernel(*refs, C, B):
    mains, gts = refs[0:2 * B:2], refs[1:2 * B:2]
    cw_ref, gc_ref, nw_ref, o_ref, s_ref, xp_ref, tail_ref = refs[2 * B:]
    n = pl.program_id(0)
    NC = B * HEADS

    @pl.when(n == 0)
    def _():
        tail_ref[...] = jnp.zeros_like(tail_ref)
        s_ref[...] = jnp.zeros_like(s_ref)

    qs, ks, vs, zs, bs, gs = [], [], [], [], [], []
    for b in range(B):
        x = mains[b][:, 0:CONV_CH]
        y = _silu(_conv_carry(x, cw_ref, xp_ref, tail_ref[b]))
        tail_ref[b] = x[C - 8:C]
        beta, gl = _gdn_gates(gts[b][...], gc_ref)
        for h in range(HEADS):
            qs.append(y[:, h * DK:(h + 1) * DK])
            ks.append(y[:, QK + h * DK:QK + (h + 1) * DK])
            vs.append(y[:, 2 * QK + h * DK:2 * QK + (h + 1) * DK])
            zs.append(mains[b][:, CONV_CH + h * DK:CONV_CH + (h + 1) * DK])
            bs.append(beta[:, h:h + 1])
            gs.append(gl[:, HEADS + h:HEADS + h + 1])
    q3 = _l2(jnp.stack(qs)) * (DK ** -0.5)
    k3 = _l2(jnp.stack(ks))
    v3 = jnp.stack(vs)
    b3 = jnp.stack(bs)
    g3 = jnp.stack(gs)
    o, S = _gdn_chunk(q3, k3, v3, b3, g3, s_ref[...].reshape(NC, DK, DK))
    s_ref[...] = S.reshape(B, HEADS, DK, DK)
    o = _rms(o) * nw_ref[...] * _silu(jnp.stack(zs))
    for b in range(B):
        for h in range(HEADS):
            o_ref[b, :, h * DK:(h + 1) * DK] = o[b * HEADS + h].astype(o_ref.dtype)


def _gdn_prompt(U, B, T, conv_w, hc, norm_w, *, C=CHUNK):
    N = T // C
    in_specs = []
    for b in range(B):
        in_specs.append(pl.BlockSpec((C, MIX_W), lambda n, b=b: (b * N + n, CB_GDN)))
        in_specs.append(pl.BlockSpec((C, DK), lambda n, b=b: (b * N + n, CB_GATES)))
    in_specs += [pl.BlockSpec((CONV_W, CONV_CH), lambda n: (0, 0)),
                 pl.BlockSpec((8, DK), lambda n: (0, 0)),
                 pl.BlockSpec((1, DK), lambda n: (0, 0))]
    o, S = pl.pallas_call(
        functools.partial(_gdn_prompt_kernel, C=C, B=B),
        grid=(N,),
        in_specs=in_specs,
        out_specs=[pl.BlockSpec((B, None, C, QK), lambda n: (0, n, 0, 0)),
                   pl.BlockSpec((B, HEADS, DK, DK), lambda n: (0, 0, 0, 0))],
        out_shape=[jax.ShapeDtypeStruct((B, N, C, QK), bf16),
                   jax.ShapeDtypeStruct((B, HEADS, DK, DK), f32)],
        scratch_shapes=[pltpu.VMEM((C + 8, CONV_CH), f32), pltpu.VMEM((B, 8, CONV_CH), f32)],
        compiler_params=_cparams(("arbitrary",)),
        name="gdn_prompt",
    )(*([U, U] * B), conv_w, hc, norm_w.reshape(1, DK))
    return o.reshape(B * T, QK), S


def _mlstm_prompt_kernel(*refs, C, B):
    mains, gts = refs[0:2 * B:2], refs[1:2 * B:2]
    gc_ref, nw_ref, h_ref, c_ref, n_ref, m_ref = refs[2 * B:]
    n = pl.program_id(0)
    NC = B * HEADS

    @pl.when(n == 0)
    def _():
        c_ref[...] = jnp.zeros_like(c_ref)
        n_ref[...] = jnp.zeros_like(n_ref)
        m_ref[...] = jnp.zeros_like(m_ref)

    qs, ks, vs, os_, is_, fs = [], [], [], [], [], []
    for b in range(B):
        i_pre, logf = _mlstm_gates(gts[b][...], gc_ref)
        for h in range(HEADS):
            qs.append(mains[b][:, h * DK:(h + 1) * DK])
            ks.append(mains[b][:, QK + h * DK:QK + (h + 1) * DK])
            vs.append(mains[b][:, 2 * QK + h * DK:2 * QK + (h + 1) * DK])
            os_.append(mains[b][:, 3 * QK + h * DK:3 * QK + (h + 1) * DK])
            is_.append(i_pre[:, 2 * HEADS + h:2 * HEADS + h + 1])
            fs.append(logf[:, 3 * HEADS + h:3 * HEADS + h + 1])
    q3 = jnp.stack(qs)
    k3 = jnp.stack(ks) * (DK ** -0.5)
    vb = jnp.stack(vs).astype(bf16)
    i3 = jnp.stack(is_)
    f3 = jnp.stack(fs)
    hh, Cm, nr, m = _mlstm_chunk(q3, k3, vb, i3, f3, c_ref[...].reshape(NC, DK, DK),
                                 n_ref[...].reshape(NC, 1, DK), m_ref[...].reshape(NC, 1, DK)[:, :, 0:1])
    nw = jnp.stack([nw_ref[h] for _ in range(B) for h in range(HEADS)])
    hh = _rms(hh) * nw * jax.nn.sigmoid(jnp.stack(os_))
    c_ref[...] = Cm.reshape(B, HEADS, DK, DK)
    n_ref[...] = nr.reshape(B, HEADS, 1, DK)
    m_ref[...] = jnp.broadcast_to(m, (NC, 1, DK)).reshape(B, HEADS, 1, DK)
    for b in range(B):
        for h in range(HEADS):
            h_ref[b, :, h * DK:(h + 1) * DK] = hh[b * HEADS + h].astype(h_ref.dtype)


def _mlstm_prompt(U, B, T, hc, norm_w, *, C=CHUNK):
    N = T // C
    in_specs = []
    for b in range(B):
        in_specs.append(pl.BlockSpec((C, MIX_W), lambda n, b=b: (b * N + n, CB_MLSTM)))
        in_specs.append(pl.BlockSpec((C, DK), lambda n, b=b: (b * N + n, CB_GATES)))
    in_specs += [pl.BlockSpec((8, DK), lambda n: (0, 0)),
                 pl.BlockSpec((HEADS, 1, DK), lambda n: (0, 0, 0))]
    st = lambda r: pl.BlockSpec((B, HEADS, r, DK), lambda n: (0, 0, 0, 0))
    h, Cm, nr, m = pl.pallas_call(
        functools.partial(_mlstm_prompt_kernel, C=C, B=B),
        grid=(N,),
        in_specs=in_specs,
        out_specs=[pl.BlockSpec((B, None, C, QK), lambda n: (0, n, 0, 0)), st(DK), st(1), st(1)],
        out_shape=[jax.ShapeDtypeStruct((B, N, C, QK), bf16),
                   jax.ShapeDtypeStruct((B, HEADS, DK, DK), f32),
                   jax.ShapeDtypeStruct((B, HEADS, 1, DK), f32),
                   jax.ShapeDtypeStruct((B, HEADS, 1, DK), f32)],
        compiler_params=_cparams(("arbitrary",)),
        name="mlstm_prompt",
    )(*([U, U] * B), hc, norm_w.reshape(HEADS, 1, DK))
    return h.reshape(B * T, QK), Cm, nr, m


def _lru_gates(c, wr_ref, br_ref, wi_ref, bi_ref, lam_ref):
    cb = c.astype(bf16)
    r = jax.nn.sigmoid(_dot(cb, wr_ref[...]) + br_ref[...])
    ig = jax.nn.sigmoid(_dot(cb, wi_ref[...]) + bi_ref[...])
    log_a = -LRU_C * r * _softplus(-lam_ref[...])
    a = jnp.exp(log_a)
    u = jnp.sqrt(-_expm1(2.0 * log_a)) * (ig * c)
    return a, u


def _lru_prompt_kernel(x_ref, y_ref, cw_ref, cb_ref, wr_ref, br_ref, wi_ref, bi_ref, lam_ref,
                       o_ref, hl_ref, xp_ref, tail_ref, hc_ref):
    tc = pl.program_id(1)
    Tc, W = x_ref.shape

    @pl.when(tc == 0)
    def _():
        tail_ref[...] = jnp.zeros_like(tail_ref)
        hc_ref[...] = jnp.zeros_like(hc_ref)

    x = x_ref[...]
    c = _conv_carry(x, cw_ref, xp_ref, tail_ref[...]) + cb_ref[...]
    tail_ref[...] = x[Tc - 8:Tc]
    a, u = _lru_gates(c, wr_ref, br_ref, wi_ref, bi_ref, lam_ref)
    row = lax.broadcasted_iota(jnp.int32, (Tc, W), 0)
    d = 1
    while d < Tc:
        keep = row >= d
        a_sh = jnp.where(keep, pltpu.roll(a, d, axis=0), 1.0)
        u_sh = jnp.where(keep, pltpu.roll(u, d, axis=0), 0.0)
        u = a * u_sh + u
        a = a * a_sh
        d *= 2
    hs = u + a * hc_ref[0:1, :]
    last = hs[Tc - 1:Tc]
    hc_ref[...] = jnp.broadcast_to(last, hc_ref.shape)
    hl_ref[...] = last
    o_ref[...] = (hs * jax.nn.gelu(y_ref[...])).astype(o_ref.dtype)


def _lru_prompt(U, B, T, cw, cb, wr, br, wi, bi, lam, *, Tc=512):
    W = LRU_W
    nt = T // Tc
    vec = lambda: pl.BlockSpec((1, W), lambda b, t: (0, 0))
    return pl.pallas_call(
        _lru_prompt_kernel,
        grid=(B, nt),
        in_specs=[pl.BlockSpec((Tc, W), lambda b, t: (b * nt + t, CB_LX)),
                  pl.BlockSpec((Tc, W), lambda b, t: (b * nt + t, CB_LY)),
                  pl.BlockSpec((CONV_W, W), lambda b, t: (0, 0)), vec(),
                  pl.BlockSpec((W, W), lambda b, t: (0, 0)), vec(),
                  pl.BlockSpec((W, W), lambda b, t: (0, 0)), vec(), vec()],
        out_specs=[pl.BlockSpec((Tc, W), lambda b, t: (b * nt + t, 0)),
                   pl.BlockSpec((None, 1, W), lambda b, t: (b, 0, 0))],
        out_shape=[jax.ShapeDtypeStruct((B * T, W), bf16),
                   jax.ShapeDtypeStruct((B, 1, W), f32)],
        scratch_shapes=[pltpu.VMEM((Tc + 8, W), f32), pltpu.VMEM((8, W), f32), pltpu.VMEM((8, W), f32)],
        compiler_params=_cparams(("parallel", "arbitrary")),
        name="lru_prompt",
    )(U, U, cw, cb.reshape(1, W), wr, br.reshape(1, W), wi, bi.reshape(1, W), lam.reshape(1, W))


def _gdn_sample_kernel(q_ref, k_ref, v_ref, z_ref, gt_ref, wq_ref, wk_ref, wv_ref, gc_ref, nw_ref, s0_ref,
                       *rest):
    o_ref, s_ref = rest[-2:]
    h = pl.program_id(0)
    BB = s0_ref.shape[0]
    R = BB * GROUP
    q = _l2(_silu(_conv_groups(q_ref[...], wq_ref))) * (DK ** -0.5)
    k = _l2(_silu(_conv_groups(k_ref[...], wk_ref)))
    v = _silu(_conv_groups(v_ref[...], wv_ref))
    beta, gl = _gdn_gates(gt_ref[...], gc_ref)
    is_tok = _group_pos((R, 1)) >= TOK0
    beta = jnp.where(is_tok, _colsel(beta, h), 0.0)
    gl = jnp.where(is_tok, _colsel(gl, HEADS + h), 0.0)
    grp = lambda a: a.reshape(BB, GROUP, a.shape[-1])
    o, S = _gdn_chunk(grp(q), grp(k), grp(v), grp(beta), grp(gl), s0_ref[...])
    s_ref[...] = S
    o_ref[...] = (_rms(o.reshape(R, DK)) * nw_ref[...] * _silu(z_ref[...])).astype(o_ref.dtype)


def _stacked_out(prev):
    if prev is None:
        return [], [], {}
    return [pl.BlockSpec(memory_space=pl.ANY)], [prev], None


def _gdn_sample(U8, S_all, d, S_prev, conv_w, hc, norm_w, *, BB=SAMPLE_SEQS_PER_STEP):
    B = S_all.shape[1]
    R = BB * GROUP
    tok = lambda cb: pl.BlockSpec((R, DK), lambda h, b: (b, cb + h))
    cw = lambda cb: pl.BlockSpec((CONV_W, DK), lambda h, b: (0, cb + h))
    st = pl.BlockSpec((None, BB, None, DK, DK), lambda h, b: (d, b, h, 0, 0))
    extra_specs, extra_ops, _ = _stacked_out(S_prev)
    n_in = 11
    return pl.pallas_call(
        _gdn_sample_kernel,
        grid=(HEADS, B // BB),
        in_specs=[tok(CB_Q), tok(CB_K), tok(CB_V), tok(CB_Z),
                  pl.BlockSpec((R, DK), lambda h, b: (b, CB_GATES)),
                  cw(CB_Q), cw(CB_K), cw(CB_V),
                  pl.BlockSpec((8, DK), lambda h, b: (0, 0)),
                  pl.BlockSpec((1, DK), lambda h, b: (0, 0)), st] + extra_specs,
        out_specs=[pl.BlockSpec((R, DK), lambda h, b: (b, h)), st],
        out_shape=[jax.ShapeDtypeStruct((B * GROUP, QK), bf16),
                   jax.ShapeDtypeStruct(S_all.shape, f32)],
        input_output_aliases={n_in: 1} if S_prev is not None else {},
        compiler_params=_cparams(("parallel", "parallel")),
        name="gdn_sample",
    )(U8, U8, U8, U8, U8, conv_w, conv_w, conv_w, hc, norm_w.reshape(1, DK), S_all, *extra_ops)


def _mlstm_sample_kernel(q_ref, k_ref, v_ref, og_ref, gt_ref, gc_ref, nw_ref, c0_ref, n0_ref, m0_ref,
                         *rest):
    h_ref, c_ref, n_ref, m_ref = rest[-4:]
    h = pl.program_id(0)
    BB = c0_ref.shape[0]
    R = BB * GROUP
    i_pre, logf = _mlstm_gates(gt_ref[...], gc_ref)
    is_tok = _group_pos((R, 1)) >= TOK0
    i_pre = jnp.where(is_tok, _colsel(i_pre, 2 * HEADS + h), NEG)
    logf = jnp.where(is_tok, _colsel(logf, 3 * HEADS + h), 0.0)
    grp = lambda a: a.reshape(BB, GROUP, a.shape[-1])
    old = lambda ref: grp(ref[...])[:, TOK0 - 1:TOK0, :]
    hh, Cm, nr, m = _mlstm_chunk(grp(q_ref[...]), grp(k_ref[...] * (DK ** -0.5)), grp(v_ref[...]).astype(bf16),
                                 grp(i_pre), grp(logf), c0_ref[...], old(n0_ref), old(m0_ref)[:, :, 0:1])
    c_ref[...] = Cm
    n_ref[...] = jnp.broadcast_to(nr, (BB, GROUP, DK)).reshape(R, DK)
    m_ref[...] = jnp.broadcast_to(m, (BB, GROUP, DK)).reshape(R, DK)
    h_ref[...] = (_rms(hh.reshape(R, DK)) * nw_ref[h] * jax.nn.sigmoid(og_ref[...])).astype(h_ref.dtype)


def _mlstm_sample(U8, C_all, d, C_prev, n0g, m0g, hc, norm_w, *, BB=SAMPLE_SEQS_PER_STEP):
    B = C_all.shape[1]
    R = BB * GROUP
    tok = lambda cb: pl.BlockSpec((R, DK), lambda h, b: (b, cb + h))
    row = pl.BlockSpec((R, DK), lambda h, b: (b, h))
    st = pl.BlockSpec((None, BB, None, DK, DK), lambda h, b: (d, b, h, 0, 0))
    extra_specs, extra_ops, _ = _stacked_out(C_prev)
    n_in = 10
    return pl.pallas_call(
        _mlstm_sample_kernel,
        grid=(HEADS, B // BB),
        in_specs=[tok(CB_MQ), tok(CB_MK), tok(CB_MV), tok(CB_MO),
                  pl.BlockSpec((R, DK), lambda h, b: (b, CB_GATES)),
                  pl.BlockSpec((8, DK), lambda h, b: (0, 0)),
                  pl.BlockSpec((HEADS, 1, DK), lambda h, b: (0, 0, 0)),
                  st, row, row] + extra_specs,
        out_specs=[row, st, row, row],
        out_shape=[jax.ShapeDtypeStruct((B * GROUP, QK), bf16),
                   jax.ShapeDtypeStruct(C_all.shape, f32),
                   jax.ShapeDtypeStruct((B * GROUP, QK), f32),
                   jax.ShapeDtypeStruct((B * GROUP, QK), f32)],
        input_output_aliases={n_in: 1} if C_prev is not None else {},
        compiler_params=_cparams(("parallel", "parallel")),
        name="mlstm_sample",
    )(U8, U8, U8, U8, U8, hc, norm_w.reshape(HEADS, 1, DK), C_all, n0g, m0g, *extra_ops)


def _lru_sample_kernel(x_ref, y_ref, h0_ref, cw_ref, cb_ref, wr_ref, br_ref, wi_ref, bi_ref, lam_ref,
                       o_ref, h_ref):
    c = _conv_groups(x_ref[...], cw_ref) + cb_ref[...]
    a, u = _lru_gates(c, wr_ref, br_ref, wi_ref, bi_ref, lam_ref)
    hs = h0_ref[...]
    pos = _group_pos(hs.shape)
    for t in range(GROUP - TOK0):
        hs = jnp.where(pos == TOK0 + t, a * pltpu.roll(hs, 1, axis=0) + u, hs)
    h_ref[...] = hs
    o_ref[...] = (hs * jax.nn.gelu(y_ref[...])).astype(o_ref.dtype)


def _lru_sample(U8, h0g, cw, cb, wr, br, wi, bi, lam, *, R=128):
    W = LRU_W
    rows = U8.shape[0]
    vec = lambda: pl.BlockSpec((1, W), lambda i: (0, 0))
    return pl.pallas_call(
        _lru_sample_kernel,
        grid=(rows // R,),
        in_specs=[pl.BlockSpec((R, W), lambda i: (i, CB_LX)),
                  pl.BlockSpec((R, W), lambda i: (i, CB_LY)),
                  pl.BlockSpec((R, W), lambda i: (i, 0)),
                  pl.BlockSpec((CONV_W, W), lambda i: (0, 0)), vec(),
                  pl.BlockSpec((W, W), lambda i: (0, 0)), vec(),
                  pl.BlockSpec((W, W), lambda i: (0, 0)), vec(), vec()],
        out_specs=[pl.BlockSpec((R, W), lambda i: (i, 0)), pl.BlockSpec((R, W), lambda i: (i, 0))],
        out_shape=[jax.ShapeDtypeStruct((rows, W), bf16), jax.ShapeDtypeStruct((rows, W), f32)],
        compiler_params=_cparams(("parallel",)),
        name="lru_sample",
    )(U8, U8, h0g, cw, cb.reshape(1, W), wr, br.reshape(1, W), wi, bi.reshape(1, W), lam.reshape(1, W))


def _permute_w_in(w_in):
    s = {}
    off = 0
    for name, size in (("qkv", CONV_CH), ("z", QK), ("gb", HEADS), ("ga", HEADS), ("mq", QK), ("mk", QK),
                       ("mv", QK), ("mo", QK), ("mi", HEADS), ("mf", HEADS), ("lx", LRU_W), ("ly", LRU_W)):
        s[name] = w_in[..., off:off + size]
        off += size
    pad = jnp.zeros(w_in.shape[:-1] + (DK - 4 * HEADS,), w_in.dtype)
    return jnp.concatenate([s["qkv"], s["z"], s["mq"], s["mk"], s["mv"], s["mo"], s["lx"], s["ly"],
                            s["gb"], s["ga"], s["mi"], s["mf"], pad], axis=-1)


def _block_diag(w):
    nb, d, e = w.shape
    eye = jnp.eye(nb, dtype=w.dtype)
    return (w[:, :, None, :] * eye[:, None, :, None]).reshape(nb * d, nb * e)


def _gate_consts(a_log, dt_bias, i_bias, f_bias):
    row = lambda v, k: jnp.pad(v, (k * HEADS, DK - (k + 1) * HEADS))
    rows = [row(a_log, 1), row(dt_bias, 1), row(i_bias, 2), row(f_bias, 3)]
    return jnp.stack(rows + [jnp.zeros((DK,), f32)] * (8 - len(rows)))


def _state_row(state):
    B, W = state.shape
    return jnp.pad(state[:, None, :], ((0, 0), (TOK0 - 1, GROUP - TOK0), (0, 0))).reshape(B * GROUP, W)


def _tail_rows(U, B, T, col, width):
    return jnp.stack([lax.slice(U, ((b + 1) * T - (CONV_W - 1), col), ((b + 1) * T, col + width))
                      for b in range(B)])


def kernel(x_prompt, x_sample, state_gdn_S, state_gdn_conv, state_mlstm_C, state_mlstm_n, state_mlstm_m, state_lru_h, state_lru_conv, ffn1_wg, ffn1_wu, ffn1_wd, ln_g, ln_b, w_in, gdn_conv_w, gdn_A_log, gdn_dt_bias, gdn_norm_w, mlstm_i_bias, mlstm_f_bias, mlstm_norm_w, lru_conv_w, lru_conv_b, lru_wr, lru_br, lru_wi, lru_bi, lru_lambda, w_out, ffn2_wg, ffn2_wu, ffn2_wd):
    BP, TP, D = x_prompt.shape
    BS, TS, _ = x_sample.shape
    MP, MS = BP * TP, BS * TS
    ffn1 = (ffn1_wg, ffn1_wu, ffn1_wd)
    ffn2 = (ffn2_wg, ffn2_wu, ffn2_wd)
    w_in_p = _permute_w_in(w_in.astype(bf16))
    w_out_b = w_out.astype(bf16)

    xp, xs = x_prompt.reshape(MP, D), x_sample.reshape(MS, D)
    outs_p = [[] for _ in range(7)]
    outs_s = [[] for _ in range(7)]
    gS_s = mC_s = None
    for d in range(DEPTH):
        x1p, x1s = _ffn_ln_both(xp, xs, ffn1, d, ln_g[d, 0], ln_b[d, 0])
        U = _inproj(x1p, w_in_p, d)
        hc = _gate_consts(gdn_A_log[d], gdn_dt_bias[d], mlstm_i_bias[d], mlstm_f_bias[d])
        wr = _block_diag(lru_wr[d]).astype(bf16)
        wi = _block_diag(lru_wi[d]).astype(bf16)
        lru_w = (lru_conv_w[d], lru_conv_b[d], wr, lru_br[d], wi, lru_bi[d], lru_lambda[d])

        og_p, gS_p = _gdn_prompt(U, BP, TP, gdn_conv_w[d], hc, gdn_norm_w[d])
        hm_p, mC_p, mn_p, mm_p = _mlstm_prompt(U, BP, TP, hc, mlstm_norm_w[d])
        ol_p, lh_p = _lru_prompt(U, BP, TP, *lru_w)
        outs_p[0].append(gS_p)
        outs_p[1].append(_tail_rows(U, BP, TP, 0, CONV_CH))
        outs_p[2].append(mC_p)
        outs_p[3].append(mn_p.reshape(BP, HEADS, DK))
        outs_p[4].append(mm_p[:, :, 0, 0])
        outs_p[5].append(lh_p.reshape(BP, LRU_W))
        outs_p[6].append(_tail_rows(U, BP, TP, COL_LX, LRU_W))

        Us = _inproj(x1s, w_in_p, d).reshape(BS, TS, D_INP)
        hist = jnp.concatenate([
            state_gdn_conv[d], jnp.zeros((BS, 3, COL_LX - CONV_CH), f32),
            state_lru_conv[d], jnp.zeros((BS, 3, D_INP - COL_LX - LRU_W), f32)], axis=2)
        U8 = jnp.concatenate([jnp.zeros((BS, 1, D_INP), f32), hist, Us], axis=1).reshape(BS * GROUP, D_INP)
        og_s, gS_s = _gdn_sample(U8, state_gdn_S, d, gS_s, gdn_conv_w[d], hc, gdn_norm_w[d])
        hm_s, mC_s, n8, m8 = _mlstm_sample(
            U8, state_mlstm_C, d, mC_s, _state_row(state_mlstm_n[d].reshape(BS, QK)),
            _state_row(jnp.repeat(state_mlstm_m[d], DK, axis=1)), hc, mlstm_norm_w[d])
        ol_s, h8 = _lru_sample(U8, _state_row(state_lru_h[d]), *lru_w)
        toks = lambda a: a.reshape(BS, GROUP, -1)[:, TOK0:].reshape(MS, -1)
        last = lambda a: a.reshape(BS, GROUP, -1)[:, GROUP - 1]
        outs_s[1].append(Us[:, TS - 3:, 0:CONV_CH])
        outs_s[3].append(last(n8).reshape(BS, HEADS, DK))
        outs_s[4].append(last(m8).reshape(BS, HEADS, DK)[:, :, 0])
        outs_s[5].append(last(h8))
        outs_s[6].append(Us[:, TS - 3:, COL_LX:COL_LX + LRU_W])

        x2p = _outproj_ln((og_p, hm_p, ol_p), x1p, w_out_b, d, ln_g[d, 1], ln_b[d, 1])
        x2s = _outproj_ln((toks(og_s), toks(hm_s), toks(ol_s)), x1s, w_out_b, d, ln_g[d, 1], ln_b[d, 1])
        xp, xs = _ffn_ln_both(x2p, x2s, ffn2, d, ln_g[d, 2], ln_b[d, 2])

    y_prompt = xp.reshape(BP, TP, D)
    y_sample = xs.reshape(BS, TS, D)
    stack = lambda o: jnp.stack(o)
    return (y_prompt, y_sample,
            *[stack(o) for o in outs_p],
            gS_s, stack(outs_s[1]), mC_s, *[stack(outs_s[i]) for i in range(3, 7)])
```

```python
import functools
import math

import jax
import jax.numpy as jnp
from jax import lax
from jax.experimental import pallas as pl
from jax.experimental.pallas import tpu as pltpu

f32 = jnp.float32
bf16 = jnp.bfloat16

DEPTH = 2
D_MODEL = 2048
D_FF = 5632
HEADS = 6
DK = 128
LRU_W = 512
LRU_BLOCKS = 8
LRU_C = 8.0
CONV_W = 4
CHUNK = 64
ALPHA = (2 * DEPTH) ** 0.25
NORM_EPS = 1e-6
NEG = -1e30

QK = HEADS * DK
CONV_CH = 3 * QK
MIX_W = 4 * QK
CB_Q, CB_K, CB_V, CB_Z = 0, 6, 12, 18
CB_MQ, CB_MK, CB_MV, CB_MO = 24, 30, 36, 42
CB_GDN, CB_MLSTM = 0, 1
CB_LX, CB_LY = 12, 13
CB_GATES = 56
COL_LX = 6144
D_INP = 7296
GROUP = 8
TOK0 = 4
ROW_TILE = 512
PROMPT_ROW_TILE = 1024
SAMPLE_SEQS_PER_STEP = 32

VMEM_LIMIT = 60 * 1024 * 1024


def _cparams(sem):
    return pltpu.CompilerParams(dimension_semantics=sem, vmem_limit_bytes=VMEM_LIMIT)


def _silu(x):
    return x * jax.nn.sigmoid(x)


def _softplus(x):
    return jnp.maximum(x, 0.0) + jnp.log1p(jnp.exp(-jnp.abs(x)))


def _expm1(x):
    u = jnp.exp(x)
    um1 = u - 1.0
    lg = jnp.where(u == 1.0, 1.0, jnp.log(jnp.where(u == 0.0, 1.0, u)))
    return jnp.where(u == 1.0, x, jnp.where(u == 0.0, -1.0, um1 * x / lg))


def _layernorm(y, g, b):
    mu = jnp.mean(y, -1, keepdims=True)
    d = y - mu
    var = jnp.mean(d * d, -1, keepdims=True)
    return d * lax.rsqrt(var + NORM_EPS) * g + b


def _rms(x):
    return x * lax.rsqrt(jnp.mean(x * x, -1, keepdims=True) + NORM_EPS)


def _l2(x):
    return x * lax.rsqrt(jnp.sum(x * x, -1, keepdims=True) + NORM_EPS)


def _colsel(gates, idx):
    lane = lax.broadcasted_iota(jnp.int32, gates.shape, 1)
    return jnp.sum(jnp.where(lane == idx, gates, 0.0), axis=1, keepdims=True)


def _dot(a, b):
    return jnp.dot(a, b, preferred_element_type=f32)


def _bmm(a, b):
    return jnp.einsum('nij,njk->nik', a, b, preferred_element_type=f32)


def _bmm_nt(a, b):
    return jnp.einsum('nid,njd->nij', a, b, preferred_element_type=f32)


def _bmm_tn(a, b):
    return jnp.einsum('nck,ncv->nkv', a, b, preferred_element_type=f32)


def _ij(C):
    return (lax.broadcasted_iota(jnp.int32, (C, C), 0), lax.broadcasted_iota(jnp.int32, (C, C), 1))


def _cumsum_row(col3, ii, jj):
    return jnp.sum(jnp.where((ii <= jj)[None], col3, 0.0), axis=1, keepdims=True)


def _row2col(row3, ii, jj):
    return jnp.sum(jnp.where((ii == jj)[None], row3, 0.0), axis=2, keepdims=True)


def _col2row(col3, ii, jj):
    return jnp.sum(jnp.where((ii == jj)[None], col3, 0.0), axis=1, keepdims=True)


def _conv_carry(x, w_ref, xp_ref, tail):
    T = x.shape[0]
    xp_ref[0:8, :] = tail
    xp_ref[8:T + 8, :] = x
    y = w_ref[3:4, :] * x
    for k in range(CONV_W - 1):
        y = y + w_ref[k:k + 1, :] * xp_ref[pl.ds(5 + k, T), :]
    return y


def _conv_groups(x, w_ref):
    y = w_ref[3:4, :] * x
    for s in range(1, CONV_W):
        y = y + w_ref[3 - s:4 - s, :] * pltpu.roll(x, s, axis=0)
    return y


def _group_pos(shape):
    return lax.broadcasted_iota(jnp.int32, shape, 0) & (GROUP - 1)


def _ffn_ln_kernel(*refs, cast_weights, aliased):
    x_ref, wg_ref, wu_ref, wd_ref, g_ref, b_ref = refs[:6]
    n_in = 7 if aliased else 6
    o_ref = refs[n_in]
    xb_ref = refs[-1]
    j = pl.program_id(1)

    @pl.when(j == 0)
    def _():
        o_ref[...] = jnp.zeros_like(o_ref)
        xb_ref[...] = x_ref[...].astype(bf16)

    if cast_weights:
        wg, wu, wd = (r[...].astype(bf16) for r in (wg_ref, wu_ref, wd_ref))
        for w_out_ref, w in zip(refs[n_in + 1:n_in + 4], (wg, wu, wd)):
            w_out_ref[...] = w
    else:
        wg, wu, wd = wg_ref[...], wu_ref[...], wd_ref[...]
    xb = xb_ref[...]
    h = (_silu(_dot(xb, wg)) * _dot(xb, wu)).astype(bf16)
    o_ref[...] += _dot(h, wd)

    @pl.when(j == pl.num_programs(1) - 1)
    def _():
        y = ALPHA * x_ref[...] + 0.5 * o_ref[...]
        o_ref[...] = _layernorm(y, g_ref[...], b_ref[...])


def _ffn_ln(x, wg, wu, wd, g, b, *, tm, tf=512, row0=0, prev=None):
    M, D = x.shape
    F = wg.shape[1]
    rows = lambda i, j: (i + row0, 0)
    in_specs = [
        pl.BlockSpec((tm, D), rows, pipeline_mode=pl.Buffered(1)),
        pl.BlockSpec((D, tf), lambda i, j: (0, j)),
        pl.BlockSpec((D, tf), lambda i, j: (0, j)),
        pl.BlockSpec((tf, D), lambda i, j: (j, 0)),
        pl.BlockSpec((1, D), lambda i, j: (0, 0)),
        pl.BlockSpec((1, D), lambda i, j: (0, 0)),
    ]
    operands = [x, wg, wu, wd, g.reshape(1, D), b.reshape(1, D)]
    if prev is not None:
        in_specs.append(pl.BlockSpec(memory_space=pl.ANY))
        operands.append(prev)
    return pl.pallas_call(
        functools.partial(_ffn_ln_kernel, cast_weights=False, aliased=prev is not None),
        grid=(M // tm - row0, F // tf),
        in_specs=in_specs,
        out_specs=pl.BlockSpec((tm, D), rows),
        out_shape=jax.ShapeDtypeStruct((M, D), f32),
        input_output_aliases={6: 0} if prev is not None else {},
        scratch_shapes=[pltpu.VMEM((tm, D), bf16)],
        compiler_params=_cparams(("parallel", "arbitrary")),
        name="ffn_ln",
    )(*operands)


def _ffn_ln_head(x, wg, wu, wd, d, g, b, *, tm, tf=256):
    M, D = x.shape
    F = wg.shape[2]
    return pl.pallas_call(
        functools.partial(_ffn_ln_kernel, cast_weights=True, aliased=False),
        grid=(1, F // tf),
        in_specs=[
            pl.BlockSpec((tm, D), lambda i, j: (0, 0), pipeline_mode=pl.Buffered(1)),
            pl.BlockSpec((None, D, tf), lambda i, j: (d, 0, j)),
            pl.BlockSpec((None, D, tf), lambda i, j: (d, 0, j)),
            pl.BlockSpec((None, tf, D), lambda i, j: (d, j, 0)),
            pl.BlockSpec((1, D), lambda i, j: (0, 0)),
            pl.BlockSpec((1, D), lambda i, j: (0, 0)),
        ],
        out_specs=[pl.BlockSpec((tm, D), lambda i, j: (0, 0)),
                   pl.BlockSpec((D, tf), lambda i, j: (0, j)),
                   pl.BlockSpec((D, tf), lambda i, j: (0, j)),
                   pl.BlockSpec((tf, D), lambda i, j: (j, 0))],
        out_shape=[jax.ShapeDtypeStruct((M, D), f32), jax.ShapeDtypeStruct((D, F), bf16),
                   jax.ShapeDtypeStruct((D, F), bf16), jax.ShapeDtypeStruct((F, D), bf16)],
        scratch_shapes=[pltpu.VMEM((tm, D), bf16)],
        compiler_params=_cparams(("arbitrary", "arbitrary")),
        name="ffn_ln_head",
    )(x, wg, wu, wd, g.reshape(1, D), b.reshape(1, D))


def _ffn_ln_both(xp, xs, w32, d, g, b):
    out, *wb = _ffn_ln_head(xp, *w32, d, g, b, tm=PROMPT_ROW_TILE)
    yp = _ffn_ln(xp, *wb, g, b, tm=PROMPT_ROW_TILE, row0=1, prev=out)
    ys = _ffn_ln(xs, *wb, g, b, tm=ROW_TILE)
    return yp, ys


def _inproj_kernel(x_ref, wt_ref, o_ref, w_scr):
    @pl.when(pl.program_id(1) == 0)
    def _():
        w_scr[...] = wt_ref[...].T

    o_ref[...] = _dot(x_ref[...].astype(bf16), w_scr[...])


def _inproj(x, wt, d, *, tm=ROW_TILE, tn=2432):
    M, D = x.shape
    N = wt.shape[1]
    return pl.pallas_call(
        _inproj_kernel,
        grid=(N // tn, M // tm),
        in_specs=[pl.BlockSpec((tm, D), lambda n, i: (i, 0)),
                  pl.BlockSpec((None, tn, D), lambda n, i: (d, n, 0))],
        out_specs=pl.BlockSpec((tm, tn), lambda n, i: (i, n)),
        out_shape=jax.ShapeDtypeStruct((M, N), f32),
        scratch_shapes=[pltpu.VMEM((D, tn), bf16)],
        compiler_params=_cparams(("arbitrary", "arbitrary")),
        name="in_proj",
    )(x, wt)


def _outproj_ln_kernel(og_ref, hm_ref, ol_ref, x_ref, w_ref, g_ref, b_ref, o_ref):
    mix = (_dot(og_ref[...], w_ref[0:QK, :]) + _dot(hm_ref[...], w_ref[QK:2 * QK, :])
           + _dot(ol_ref[...], w_ref[2 * QK:, :]))
    o_ref[...] = _layernorm(ALPHA * x_ref[...] + mix, g_ref[...], b_ref[...])


def _outproj_ln(mix, x, w, d, g, b, *, tm=ROW_TILE):
    M, D = x.shape
    return pl.pallas_call(
        _outproj_ln_kernel,
        grid=(M // tm,),
        in_specs=[pl.BlockSpec((tm, wd), lambda i: (i, 0)) for wd in (QK, QK, LRU_W)]
        + [pl.BlockSpec((tm, D), lambda i: (i, 0)),
           pl.BlockSpec((None, D, D), lambda i: (d, 0, 0)),
           pl.BlockSpec((1, D), lambda i: (0, 0)),
           pl.BlockSpec((1, D), lambda i: (0, 0))],
        out_specs=pl.BlockSpec((tm, D), lambda i: (i, 0)),
        out_shape=jax.ShapeDtypeStruct((M, D), f32),
        compiler_params=_cparams(("parallel",)),
        name="out_proj_ln",
    )(*mix, x, w, g.reshape(1, D), b.reshape(1, D))


def _gdn_gates(gates, gc_ref):
    beta = jax.nn.sigmoid(gates)
    gl = -jnp.exp(gc_ref[0:1, :]) * _softplus(gates + gc_ref[1:2, :])
    return beta, gl


def _mlstm_gates(gates, gc_ref):
    i_pre = gates + gc_ref[2:3, :]
    logf = -_softplus(-(gates + gc_ref[3:4, :]))
    return i_pre, logf


def _gdn_chunk(q3, k3, v3, b3, g3, S):
    C = q3.shape[1]
    ii, jj = _ij(C)
    g_row = _cumsum_row(g3, ii, jj)
    g_col = _row2col(g_row, ii, jj)
    decay = jnp.exp(jnp.where((ii >= jj)[None], g_col - g_row, NEG))
    kb = k3.astype(bf16)
    kk = _bmm_nt(kb, kb)
    p = jnp.where((ii > jj)[None], -(b3 * kk * decay), 0.0)
    e = p
    for _ in range(int(math.log2(C)) - 1):
        pb = p.astype(bf16)
        p = _bmm(pb, pb)
        e = e + p + _bmm(e.astype(bf16), p.astype(bf16))
    eb = e.astype(bf16)
    e_g = jnp.exp(g_col)
    rv = b3 * v3
    rk = (b3 * e_g) * k3
    uv = rv + _bmm(eb, rv.astype(bf16))
    wks = (rk + _bmm(eb, rk.astype(bf16))).astype(bf16)
    qk = (_bmm_nt(q3.astype(bf16), kb) * decay).astype(bf16)
    g_last = g_row[:, :, C - 1:C]
    qd = (q3 * e_g).astype(bf16)
    kt = (k3 * jnp.exp(g_last - g_col)).astype(bf16)
    Sb = S.astype(bf16)
    Ub = (uv - _bmm(wks, Sb)).astype(bf16)
    o = _bmm(qd, Sb) + _bmm(qk, Ub)
    return o, jnp.exp(g_last) * S + _bmm_tn(kt, Ub)


def _mlstm_chunk(q3, k3, vb, i3, f3, Cm, nr, m):
    C = q3.shape[1]
    ii, jj = _ij(C)
    b_row = _cumsum_row(f3, ii, jj)
    b_col = _row2col(b_row, ii, jj)
    i_row = _col2row(i3, ii, jj)
    D = jnp.where((ii >= jj)[None], b_col - b_row + i_row, NEG)
    d_max = jnp.max(D, axis=2, keepdims=True)
    qb = q3.astype(bf16)
    qk = _bmm_nt(qb, k3.astype(bf16))
    m_t = jnp.maximum(b_col + m, d_max)
    inter = jnp.exp(b_col + m - m_t)
    Sw = jnp.exp(D - m_t) * qk
    num = inter * _bmm(qb, Cm.astype(bf16)) + _bmm(Sw.astype(bf16), vb)
    den = inter * jnp.sum(q3 * nr, -1, keepdims=True) + jnp.sum(Sw, -1, keepdims=True)
    hh = num / jnp.maximum(jnp.abs(den), jnp.exp(-m_t))
    m_new = m_t[:, C - 1:C]
    b_last = b_col[:, C - 1:C]
    kw = jnp.exp(b_last - b_col + i3 - m_new) * k3
    dec = jnp.exp(b_last + m - m_new)
    return (hh, dec * Cm + _bmm_tn(kw.astype(bf16), vb), dec * nr + jnp.sum(kw, axis=1, keepdims=True), m_new)


def _gdn_prompt_kernel(*refs, C, B):
    mains, gts = refs[0:2 * B:2], refs[1:2 * B:2]
    cw_ref, gc_ref, nw_ref, o_ref, s_ref, xp_ref, tail_ref = refs[2 * B:]
    n = pl.program_id(0)
    NC = B * HEADS

    @pl.when(n == 0)
    def _():
        tail_ref[...] = jnp.zeros_like(tail_ref)
        s_ref[...] = jnp.zeros_like(s_ref)

    qs, ks, vs, zs, bs, gs = [], [], [], [], [], []
    for b in range(B):
        x = mains[b][:, 0:CONV_CH]
        y = _silu(_conv_carry(x, cw_ref, xp_ref, tail_ref[b]))
        tail_ref[b] = x[C - 8:C]
        beta, gl = _gdn_gates(gts[b][...], gc_ref)
        for h in range(HEADS):
            qs.append(y[:, h * DK:(h + 1) * DK])
            ks.append(y[:, QK + h * DK:QK + (h + 1) * DK])
            vs.append(y[:, 2 * QK + h * DK:2 * QK + (h + 1) * DK])
            zs.append(mains[b][:, CONV_CH + h * DK:CONV_CH + (h + 1) * DK])
            bs.append(beta[:, h:h + 1])
            gs.append(gl[:, HEADS + h:HEADS + h + 1])
    q3 = _l2(jnp.stack(qs)) * (DK ** -0.5)
    k3 = _l2(jnp.stack(ks))
    v3 = jnp.stack(vs)
    b3 = jnp.stack(bs)
    g3 = jnp.stack(gs)
    o, S = _gdn_chunk(q3, k3, v3, b3, g3, s_ref[...].reshape(NC, DK, DK))
    s_ref[...] = S.reshape(B, HEADS, DK, DK)
    o = _rms(o) * nw_ref[...] * _silu(jnp.stack(zs))
    for b in range(B):
        for h in range(HEADS):
            o_ref[b, :, h * DK:(h + 1) * DK] = o[b * HEADS + h].astype(o_ref.dtype)


def _gdn_prompt(U, B, T, conv_w, hc, norm_w, *, C=CHUNK):
    N = T // C
    in_specs = []
    for b in range(B):
        in_specs.append(pl.BlockSpec((C, MIX_W), lambda n, b=b: (b * N + n, CB_GDN)))
        in_specs.append(pl.BlockSpec((C, DK), lambda n, b=b: (b * N + n, CB_GATES)))
    in_specs += [pl.BlockSpec((CONV_W, CONV_CH), lambda n: (0, 0)),
                 pl.BlockSpec((8, DK), lambda n: (0, 0)),
                 pl.BlockSpec((1, DK), lambda n: (0, 0))]
    o, S = pl.pallas_call(
        functools.partial(_gdn_prompt_kernel, C=C, B=B),
        grid=(N,),
        in_specs=in_specs,
        out_specs=[pl.BlockSpec((B, None, C, QK), lambda n: (0, n, 0, 0)),
                   pl.BlockSpec((B, HEADS, DK, DK), lambda n: (0, 0, 0, 0))],
        out_shape=[jax.ShapeDtypeStruct((B, N, C, QK), bf16),
                   jax.ShapeDtypeStruct((B, HEADS, DK, DK), f32)],
        scratch_shapes=[pltpu.VMEM((C + 8, CONV_CH), f32), pltpu.VMEM((B, 8, CONV_CH), f32)],
        compiler_params=_cparams(("arbitrary",)),
        name="gdn_prompt",
    )(*([U, U] * B), conv_w, hc, norm_w.reshape(1, DK))
    return o.reshape(B * T, QK), S


def _mlstm_prompt_kernel(*refs, C, B):
    mains, gts = refs[0:2 * B:2], refs[1:2 * B:2]
    gc_ref, nw_ref, h_ref, c_ref, n_ref, m_ref = refs[2 * B:]
    n = pl.program_id(0)
    NC = B * HEADS

    @pl.when(n == 0)
    def _():
        c_ref[...] = jnp.zeros_like(c_ref)
        n_ref[...] = jnp.zeros_like(n_ref)
        m_ref[...] = jnp.zeros_like(m_ref)

    qs, ks, vs, os_, is_, fs = [], [], [], [], [], []
    for b in range(B):
        i_pre, logf = _mlstm_gates(gts[b][...], gc_ref)
        for h in range(HEADS):
            qs.append(mains[b][:, h * DK:(h + 1) * DK])
            ks.append(mains[b][:, QK + h * DK:QK + (h + 1) * DK])
            vs.append(mains[b][:, 2 * QK + h * DK:2 * QK + (h + 1) * DK])
            os_.append(mains[b][:, 3 * QK + h * DK:3 * QK + (h + 1) * DK])
            is_.append(i_pre[:, 2 * HEADS + h:2 * HEADS + h + 1])
            fs.append(logf[:, 3 * HEADS + h:3 * HEADS + h + 1])
    q3 = jnp.stack(qs)
    k3 = jnp.stack(ks) * (DK ** -0.5)
    vb = jnp.stack(vs).astype(bf16)
    i3 = jnp.stack(is_)
    f3 = jnp.stack(fs)
    hh, Cm, nr, m = _mlstm_chunk(q3, k3, vb, i3, f3, c_ref[...].reshape(NC, DK, DK),
                                 n_ref[...].reshape(NC, 1, DK), m_ref[...].reshape(NC, 1, DK)[:, :, 0:1])
    nw = jnp.stack([nw_ref[h] for _ in range(B) for h in range(HEADS)])
    hh = _rms(hh) * nw * jax.nn.sigmoid(jnp.stack(os_))
    c_ref[...] = Cm.reshape(B, HEADS, DK, DK)
    n_ref[...] = nr.reshape(B, HEADS, 1, DK)
    m_ref[...] = jnp.broadcast_to(m, (NC, 1, DK)).reshape(B, HEADS, 1, DK)
    for b in range(B):
        for h in range(HEADS):
            h_ref[b, :, h * DK:(h + 1) * DK] = hh[b * HEADS + h].astype(h_ref.dtype)


def _mlstm_prompt(U, B, T, hc, norm_w, *, C=CHUNK):
    N = T // C
    in_specs = []
    for b in range(B):
        in_specs.append(pl.BlockSpec((C, MIX_W), lambda n, b=b: (b * N + n, CB_MLSTM)))
        in_specs.append(pl.BlockSpec((C, DK), lambda n, b=b: (b * N + n, CB_GATES)))
    in_specs += [pl.BlockSpec((8, DK), lambda n: (0, 0)),
                 pl.BlockSpec((HEADS, 1, DK), lambda n: (0, 0, 0))]
    st = lambda r: pl.BlockSpec((B, HEADS, r, DK), lambda n: (0, 0, 0, 0))
    h, Cm, nr, m = pl.pallas_call(
        functools.partial(_mlstm_prompt_kernel, C=C, B=B),
        grid=(N,),
        in_specs=in_specs,
        out_specs=[pl.BlockSpec((B, None, C, QK), lambda n: (0, n, 0, 0)), st(DK), st(1), st(1)],
        out_shape=[jax.ShapeDtypeStruct((B, N, C, QK), bf16),
                   jax.ShapeDtypeStruct((B, HEADS, DK, DK), f32),
                   jax.ShapeDtypeStruct((B, HEADS, 1, DK), f32),
                   jax.ShapeDtypeStruct((B, HEADS, 1, DK), f32)],
        compiler_params=_cparams(("arbitrary",)),
        name="mlstm_prompt",
    )(*([U, U] * B), hc, norm_w.reshape(HEADS, 1, DK))
    return h.reshape(B * T, QK), Cm, nr, m


def _lru_gates(c, wr_ref, br_ref, wi_ref, bi_ref, lam_ref):
    cb = c.astype(bf16)
    r = jax.nn.sigmoid(_dot(cb, wr_ref[...]) + br_ref[...])
    ig = jax.nn.sigmoid(_dot(cb, wi_ref[...]) + bi_ref[...])
    log_a = -LRU_C * r * _softplus(-lam_ref[...])
    a = jnp.exp(log_a)
    u = jnp.sqrt(-_expm1(2.0 * log_a)) * (ig * c)
    return a, u


def _lru_prompt_kernel(x_ref, y_ref, cw_ref, cb_ref, wr_ref, br_ref, wi_ref, bi_ref, lam_ref,
                       o_ref, hl_ref, xp_ref, tail_ref, hc_ref):
    tc = pl.program_id(1)
    Tc, W = x_ref.shape

    @pl.when(tc == 0)
    def _():
        tail_ref[...] = jnp.zeros_like(tail_ref)
        hc_ref[...] = jnp.zeros_like(hc_ref)

    x = x_ref[...]
    c = _conv_carry(x, cw_ref, xp_ref, tail_ref[...]) + cb_ref[...]
    tail_ref[...] = x[Tc - 8:Tc]
    a, u = _lru_gates(c, wr_ref, br_ref, wi_ref, bi_ref, lam_ref)
    row = lax.broadcasted_iota(jnp.int32, (Tc, W), 0)
    d = 1
    while d < Tc:
        keep = row >= d
        a_sh = jnp.where(keep, pltpu.roll(a, d, axis=0), 1.0)
        u_sh = jnp.where(keep, pltpu.roll(u, d, axis=0), 0.0)
        u = a * u_sh + u
        a = a * a_sh
        d *= 2
    hs = u + a * hc_ref[0:1, :]
    last = hs[Tc - 1:Tc]
    hc_ref[...] = jnp.broadcast_to(last, hc_ref.shape)
    hl_ref[...] = last
    o_ref[...] = (hs * jax.nn.gelu(y_ref[...])).astype(o_ref.dtype)


def _lru_prompt(U, B, T, cw, cb, wr, br, wi, bi, lam, *, Tc=512):
    W = LRU_W
    nt = T // Tc
    vec = lambda: pl.BlockSpec((1, W), lambda b, t: (0, 0))
    return pl.pallas_call(
        _lru_prompt_kernel,
        grid=(B, nt),
        in_specs=[pl.BlockSpec((Tc, W), lambda b, t: (b * nt + t, CB_LX)),
                  pl.BlockSpec((Tc, W), lambda b, t: (b * nt + t, CB_LY)),
                  pl.BlockSpec((CONV_W, W), lambda b, t: (0, 0)), vec(),
                  pl.BlockSpec((W, W), lambda b, t: (0, 0)), vec(),
                  pl.BlockSpec((W, W), lambda b, t: (0, 0)), vec(), vec()],
        out_specs=[pl.BlockSpec((Tc, W), lambda b, t: (b * nt + t, 0)),
                   pl.BlockSpec((None, 1, W), lambda b, t: (b, 0, 0))],
        out_shape=[jax.ShapeDtypeStruct((B * T, W), bf16),
                   jax.ShapeDtypeStruct((B, 1, W), f32)],
        scratch_shapes=[pltpu.VMEM((Tc + 8, W), f32), pltpu.VMEM((8, W), f32), pltpu.VMEM((8, W), f32)],
        compiler_params=_cparams(("parallel", "arbitrary")),
        name="lru_prompt",
    )(U, U, cw, cb.reshape(1, W), wr, br.reshape(1, W), wi, bi.reshape(1, W), lam.reshape(1, W))


def _gdn_sample_kernel(q_ref, k_ref, v_ref, z_ref, gt_ref, wq_ref, wk_ref, wv_ref, gc_ref, nw_ref, s0_ref,
                       *rest):
    o_ref, s_ref = rest[-2:]
    h = pl.program_id(0)
    BB = s0_ref.shape[0]
    R = BB * GROUP
    q = _l2(_silu(_conv_groups(q_ref[...], wq_ref))) * (DK ** -0.5)
    k = _l2(_silu(_conv_groups(k_ref[...], wk_ref)))
    v = _silu(_conv_groups(v_ref[...], wv_ref))
    beta, gl = _gdn_gates(gt_ref[...], gc_ref)
    is_tok = _group_pos((R, 1)) >= TOK0
    beta = jnp.where(is_tok, _colsel(beta, h), 0.0)
    gl = jnp.where(is_tok, _colsel(gl, HEADS + h), 0.0)
    grp = lambda a: a.reshape(BB, GROUP, a.shape[-1])
    o, S = _gdn_chunk(grp(q), grp(k), grp(v), grp(beta), grp(gl), s0_ref[...])
    s_ref[...] = S
    o_ref[...] = (_rms(o.reshape(R, DK)) * nw_ref[...] * _silu(z_ref[...])).astype(o_ref.dtype)


def _stacked_out(prev):
    if prev is None:
        return [], [], {}
    return [pl.BlockSpec(memory_space=pl.ANY)], [prev], None


def _gdn_sample(U8, S_all, d, S_prev, conv_w, hc, norm_w, *, BB=SAMPLE_SEQS_PER_STEP):
    B = S_all.shape[1]
    R = BB * GROUP
    tok = lambda cb: pl.BlockSpec((R, DK), lambda h, b: (b, cb + h))
    cw = lambda cb: pl.BlockSpec((CONV_W, DK), lambda h, b: (0, cb + h))
    st = pl.BlockSpec((None, BB, None, DK, DK), lambda h, b: (d, b, h, 0, 0))
    extra_specs, extra_ops, _ = _stacked_out(S_prev)
    n_in = 11
    return pl.pallas_call(
        _gdn_sample_kernel,
        grid=(HEADS, B // BB),
        in_specs=[tok(CB_Q), tok(CB_K), tok(CB_V), tok(CB_Z),
                  pl.BlockSpec((R, DK), lambda h, b: (b, CB_GATES)),
                  cw(CB_Q), cw(CB_K), cw(CB_V),
                  pl.BlockSpec((8, DK), lambda h, b: (0, 0)),
                  pl.BlockSpec((1, DK), lambda h, b: (0, 0)), st] + extra_specs,
        out_specs=[pl.BlockSpec((R, DK), lambda h, b: (b, h)), st],
        out_shape=[jax.ShapeDtypeStruct((B * GROUP, QK), bf16),
                   jax.ShapeDtypeStruct(S_all.shape, f32)],
        input_output_aliases={n_in: 1} if S_prev is not None else {},
        compiler_params=_cparams(("parallel", "parallel")),
        name="gdn_sample",
    )(U8, U8, U8, U8, U8, conv_w, conv_w, conv_w, hc, norm_w.reshape(1, DK), S_all, *extra_ops)


def _mlstm_sample_kernel(q_ref, k_ref, v_ref, og_ref, gt_ref, gc_ref, nw_ref, c0_ref, n0_ref, m0_ref,
                         *rest):
    h_ref, c_ref, n_ref, m_ref = rest[-4:]
    h = pl.program_id(0)
    BB = c0_ref.shape[0]
    R = BB * GROUP
    i_pre, logf = _mlstm_gates(gt_ref[...], gc_ref)
    is_tok = _group_pos((R, 1)) >= TOK0
    i_pre = jnp.where(is_tok, _colsel(i_pre, 2 * HEADS + h), NEG)
    logf = jnp.where(is_tok, _colsel(logf, 3 * HEADS + h), 0.0)
    grp = lambda a: a.reshape(BB, GROUP, a.shape[-1])
    old = lambda ref: grp(ref[...])[:, TOK0 - 1:TOK0, :]
    hh, Cm, nr, m = _mlstm_chunk(grp(q_ref[...]), grp(k_ref[...] * (DK ** -0.5)), grp(v_ref[...]).astype(bf16),
                                 grp(i_pre), grp(logf), c0_ref[...], old(n0_ref), old(m0_ref)[:, :, 0:1])
    c_ref[...] = Cm
    n_ref[...] = jnp.broadcast_to(nr, (BB, GROUP, DK)).reshape(R, DK)
    m_ref[...] = jnp.broadcast_to(m, (BB, GROUP, DK)).reshape(R, DK)
    h_ref[...] = (_rms(hh.reshape(R, DK)) * nw_ref[h] * jax.nn.sigmoid(og_ref[...])).astype(h_ref.dtype)


def _mlstm_sample(U8, C_all, d, C_prev, n0g, m0g, hc, norm_w, *, BB=SAMPLE_SEQS_PER_STEP):
    B = C_all.shape[1]
    R = BB * GROUP
    tok = lambda cb: pl.BlockSpec((R, DK), lambda h, b: (b, cb + h))
    row = pl.BlockSpec((R, DK), lambda h, b: (b, h))
    st = pl.BlockSpec((None, BB, None, DK, DK), lambda h, b: (d, b, h, 0, 0))
    extra_specs, extra_ops, _ = _stacked_out(C_prev)
    n_in = 10
    return pl.pallas_call(
        _mlstm_sample_kernel,
        grid=(HEADS, B // BB),
        in_specs=[tok(CB_MQ), tok(CB_MK), tok(CB_MV), tok(CB_MO),
                  pl.BlockSpec((R, DK), lambda h, b: (b, CB_GATES)),
                  pl.BlockSpec((8, DK), lambda h, b: (0, 0)),
                  pl.BlockSpec((HEADS, 1, DK), lambda h, b: (0, 0, 0)),
                  st, row, row] + extra_specs,
        out_specs=[row, st, row, row],
        out_shape=[jax.ShapeDtypeStruct((B * GROUP, QK), bf16),
                   jax.ShapeDtypeStruct(C_all.shape, f32),
                   jax.ShapeDtypeStruct((B * GROUP, QK), f32),
                   jax.ShapeDtypeStruct((B * GROUP, QK), f32)],
        input_output_aliases={n_in: 1} if C_prev is not None else {},
        compiler_params=_cparams(("parallel", "parallel")),
        name="mlstm_sample",
    )(U8, U8, U8, U8, U8, hc, norm_w.reshape(HEADS, 1, DK), C_all, n0g, m0g, *extra_ops)


def _lru_sample_kernel(x_ref, y_ref, h0_ref, cw_ref, cb_ref, wr_ref, br_ref, wi_ref, bi_ref, lam_ref,
                       o_ref, h_ref):
    c = _conv_groups(x_ref[...], cw_ref) + cb_ref[...]
    a, u = _lru_gates(c, wr_ref, br_ref, wi_ref, bi_ref, lam_ref)
    hs = h0_ref[...]
    pos = _group_pos(hs.shape)
    for t in range(GROUP - TOK0):
        hs = jnp.where(pos == TOK0 + t, a * pltpu.roll(hs, 1, axis=0) + u, hs)
    h_ref[...] = hs
    o_ref[...] = (hs * jax.nn.gelu(y_ref[...])).astype(o_ref.dtype)


def _lru_sample(U8, h0g, cw, cb, wr, br, wi, bi, lam, *, R=128):
    W = LRU_W
    rows = U8.shape[0]
    vec = lambda: pl.BlockSpec((1, W), lambda i: (0, 0))
    return pl.pallas_call(
        _lru_sample_kernel,
        grid=(rows // R,),
        in_specs=[pl.BlockSpec((R, W), lambda i: (i, CB_LX)),
                  pl.BlockSpec((R, W), lambda i: (i, CB_LY)),
                  pl.BlockSpec((R, W), lambda i: (i, 0)),
                  pl.BlockSpec((CONV_W, W), lambda i: (0, 0)), vec(),
                  pl.BlockSpec((W, W), lambda i: (0, 0)), vec(),
                  pl.BlockSpec((W, W), lambda i: (0, 0)), vec(), vec()],
        out_specs=[pl.BlockSpec((R, W), lambda i: (i, 0)), pl.BlockSpec((R, W), lambda i: (i, 0))],
        out_shape=[jax.ShapeDtypeStruct((rows, W), bf16), jax.ShapeDtypeStruct((rows, W), f32)],
        compiler_params=_cparams(("parallel",)),
        name="lru_sample",
    )(U8, U8, h0g, cw, cb.reshape(1, W), wr, br.reshape(1, W), wi, bi.reshape(1, W), lam.reshape(1, W))


def _permute_w_in(w_in):
    wt = jnp.swapaxes(w_in, 1, 2).astype(bf16)
    s = {}
    off = 0
    for name, size in (("qkv", CONV_CH), ("z", QK), ("gb", HEADS), ("ga", HEADS), ("mq", QK), ("mk", QK),
                       ("mv", QK), ("mo", QK), ("mi", HEADS), ("mf", HEADS), ("lx", LRU_W), ("ly", LRU_W)):
        s[name] = wt[:, off:off + size, :]
        off += size
    pad = jnp.zeros((wt.shape[0], DK - 4 * HEADS, wt.shape[2]), wt.dtype)
    return jnp.concatenate([s["qkv"], s["z"], s["mq"], s["mk"], s["mv"], s["mo"], s["lx"], s["ly"],
                            s["gb"], s["ga"], s["mi"], s["mf"], pad], axis=1)


def _block_diag(w):
    nb, d, e = w.shape
    eye = jnp.eye(nb, dtype=w.dtype)
    return (w[:, :, None, :] * eye[:, None, :, None]).reshape(nb * d, nb * e)


def _gate_consts(a_log, dt_bias, i_bias, f_bias):
    row = lambda v, k: jnp.pad(v, (k * HEADS, DK - (k + 1) * HEADS))
    rows = [row(a_log, 1), row(dt_bias, 1), row(i_bias, 2), row(f_bias, 3)]
    return jnp.stack(rows + [jnp.zeros((DK,), f32)] * (8 - len(rows)))


def _state_row(state):
    B, W = state.shape
    return jnp.pad(state[:, None, :], ((0, 0), (TOK0 - 1, GROUP - TOK0), (0, 0))).reshape(B * GROUP, W)


def _tail_rows(U, B, T, col, width):
    return jnp.stack([lax.slice(U, ((b + 1) * T - (CONV_W - 1), col), ((b + 1) * T, col + width))
                      for b in range(B)])


def kernel(x_prompt, x_sample, state_gdn_S, state_gdn_conv, state_mlstm_C, state_mlstm_n, state_mlstm_m, state_lru_h, state_lru_conv, ffn1_wg, ffn1_wu, ffn1_wd, ln_g, ln_b, w_in, gdn_conv_w, gdn_A_log, gdn_dt_bias, gdn_norm_w, mlstm_i_bias, mlstm_f_bias, mlstm_norm_w, lru_conv_w, lru_conv_b, lru_wr, lru_br, lru_wi, lru_bi, lru_lambda, w_out, ffn2_wg, ffn2_wu, ffn2_wd):
    BP, TP, D = x_prompt.shape
    BS, TS, _ = x_sample.shape
    MP, MS = BP * TP, BS * TS
    ffn1 = (ffn1_wg, ffn1_wu, ffn1_wd)
    ffn2 = (ffn2_wg, ffn2_wu, ffn2_wd)
    w_in_p = _permute_w_in(w_in)
    w_out_b = w_out.astype(bf16)

    xp, xs = x_prompt.reshape(MP, D), x_sample.reshape(MS, D)
    outs_p = [[] for _ in range(7)]
    outs_s = [[] for _ in range(7)]
    gS_s = mC_s = None
    for d in range(DEPTH):
        x1p, x1s = _ffn_ln_both(xp, xs, ffn1, d, ln_g[d, 0], ln_b[d, 0])
        U = _inproj(x1p, w_in_p, d)
        hc = _gate_consts(gdn_A_log[d], gdn_dt_bias[d], mlstm_i_bias[d], mlstm_f_bias[d])
        wr = _block_diag(lru_wr[d]).astype(bf16)
        wi = _block_diag(lru_wi[d]).astype(bf16)
        lru_w = (lru_conv_w[d], lru_conv_b[d], wr, lru_br[d], wi, lru_bi[d], lru_lambda[d])

        og_p, gS_p = _gdn_prompt(U, BP, TP, gdn_conv_w[d], hc, gdn_norm_w[d])
        hm_p, mC_p, mn_p, mm_p = _mlstm_prompt(U, BP, TP, hc, mlstm_norm_w[d])
        ol_p, lh_p = _lru_prompt(U, BP, TP, *lru_w)
        outs_p[0].append(gS_p)
        outs_p[1].append(_tail_rows(U, BP, TP, 0, CONV_CH))
        outs_p[2].append(mC_p)
        outs_p[3].append(mn_p.reshape(BP, HEADS, DK))
        outs_p[4].append(mm_p[:, :, 0, 0])
        outs_p[5].append(lh_p.reshape(BP, LRU_W))
        outs_p[6].append(_tail_rows(U, BP, TP, COL_LX, LRU_W))

        Us = _inproj(x1s, w_in_p, d).reshape(BS, TS, D_INP)
        hist = jnp.concatenate([
            state_gdn_conv[d], jnp.zeros((BS, 3, COL_LX - CONV_CH), f32),
            state_lru_conv[d], jnp.zeros((BS, 3, D_INP - COL_LX - LRU_W), f32)], axis=2)
        U8 = jnp.concatenate([jnp.zeros((BS, 1, D_INP), f32), hist, Us], axis=1).reshape(BS * GROUP, D_INP)
        og_s, gS_s = _gdn_sample(U8, state_gdn_S, d, gS_s, gdn_conv_w[d], hc, gdn_norm_w[d])
        hm_s, mC_s, n8, m8 = _mlstm_sample(
            U8, state_mlstm_C, d, mC_s, _state_row(state_mlstm_n[d].reshape(BS, QK)),
            _state_row(jnp.repeat(state_mlstm_m[d], DK, axis=1)), hc, mlstm_norm_w[d])
        ol_s, h8 = _lru_sample(U8, _state_row(state_lru_h[d]), *lru_w)
        toks = lambda a: a.reshape(BS, GROUP, -1)[:, TOK0:].reshape(MS, -1)
        last = lambda a: a.reshape(BS, GROUP, -1)[:, GROUP - 1]
        outs_s[1].append(Us[:, TS - 3:, 0:CONV_CH])
        outs_s[3].append(last(n8).reshape(BS, HEADS, DK))
        outs_s[4].append(last(m8).reshape(BS, HEADS, DK)[:, :, 0])
        outs_s[5].append(last(h8))
        outs_s[6].append(Us[:, TS - 3:, COL_LX:COL_LX + LRU_W])

        x2p = _outproj_ln((og_p, hm_p, ol_p), x1p, w_out_b, d, ln_g[d, 1], ln_b[d, 1])
        x2s = _outproj_ln((toks(og_s), toks(hm_s), toks(ol_s)), x1s, w_out_b, d, ln_g[d, 1], ln_b[d, 1])
        xp, xs = _ffn_ln_both(x2p, x2s, ffn2, d, ln_g[d, 2], ln_b[d, 2])

    y_prompt = xp.reshape(BP, TP, D)
    y_sample = xs.reshape(BS, TS, D)
    stack = lambda o: jnp.stack(o)
    return (y_prompt, y_sample,
            *[stack(o) for o in outs_p],
            gS_s, stack(outs_s[1]), mC_s, *[stack(outs_s[i]) for i in range(3, 7)])
```

```python
import functools
import math

import jax
import jax.numpy as jnp
from jax import lax
from jax.experimental import pallas as pl
from jax.experimental.pallas import tpu as pltpu

f32 = jnp.float32
bf16 = jnp.bfloat16

DEPTH = 2
D_MODEL = 2048
D_FF = 5632
HEADS = 6
DK = 128
LRU_W = 512
LRU_BLOCKS = 8
LRU_C = 8.0
CONV_W = 4
CHUNK = 64
ALPHA = (2 * DEPTH) ** 0.25
NORM_EPS = 1e-6
NEG = -1e30

QK = HEADS * DK
CONV_CH = 3 * QK
MIX_W = 4 * QK
CB_Q, CB_K, CB_V, CB_Z = 0, 6, 12, 18
SMALL_W = 2 * LRU_W + DK
CB_LX, CB_LY = 0, 1
CB_GATES = 2 * LRU_W // DK
SRC_GATES_G, SRC_MLSTM, SRC_GATES_M, SRC_LRU = 3072, 3084, 6156, 6168
MLSTM_SLAB0 = 3072
MLSTM_SLAB_ROWS = 3088
CONV_ROW_TILE = 256
GROUP = 8
TOK0 = 4
ROW_TILE = 512
PROMPT_ROW_TILE = 1024
SAMPLE_SEQS_PER_STEP = 32

VMEM_LIMIT = 60 * 1024 * 1024


def _cparams(sem):
    return pltpu.CompilerParams(dimension_semantics=sem, vmem_limit_bytes=VMEM_LIMIT)


def _silu(x):
    return x * jax.nn.sigmoid(x)


def _softplus(x):
    return jnp.maximum(x, 0.0) + jnp.log1p(jnp.exp(-jnp.abs(x)))


def _expm1(x):
    u = jnp.exp(x)
    um1 = u - 1.0
    lg = jnp.where(u == 1.0, 1.0, jnp.log(jnp.where(u == 0.0, 1.0, u)))
    return jnp.where(u == 1.0, x, jnp.where(u == 0.0, -1.0, um1 * x / lg))


def _layernorm(y, g, b):
    mu = jnp.mean(y, -1, keepdims=True)
    d = y - mu
    var = jnp.mean(d * d, -1, keepdims=True)
    return d * lax.rsqrt(var + NORM_EPS) * g + b


def _rms(x):
    return x * lax.rsqrt(jnp.mean(x * x, -1, keepdims=True) + NORM_EPS)


def _l2(x):
    return x * lax.rsqrt(jnp.sum(x * x, -1, keepdims=True) + NORM_EPS)


def _colsel(gates, idx):
    lane = lax.broadcasted_iota(jnp.int32, gates.shape, 1)
    return jnp.sum(jnp.where(lane == idx, gates, 0.0), axis=1, keepdims=True)


def _dot(a, b):
    return jnp.dot(a, b, preferred_element_type=f32)


def _bmm(a, b):
    return jnp.einsum('nij,njk->nik', a, b, preferred_element_type=f32)


def _bmm_nt(a, b):
    return jnp.einsum('nid,njd->nij', a, b, preferred_element_type=f32)


def _bmm_tn(a, b):
    return jnp.einsum('nck,ncv->nkv', a, b, preferred_element_type=f32)


def _ij(C):
    return (lax.broadcasted_iota(jnp.int32, (C, C), 0), lax.broadcasted_iota(jnp.int32, (C, C), 1))


def _cumsum_row(col3, ii, jj):
    return jnp.sum(jnp.where((ii <= jj)[None], col3, 0.0), axis=1, keepdims=True)


def _row2col(row3, ii, jj):
    return jnp.sum(jnp.where((ii == jj)[None], row3, 0.0), axis=2, keepdims=True)


def _col2row(col3, ii, jj):
    return jnp.sum(jnp.where((ii == jj)[None], col3, 0.0), axis=1, keepdims=True)


def _conv_carry(x, w_ref, xp_ref, tail):
    T = x.shape[0]
    xp_ref[0:8, :] = tail
    xp_ref[8:T + 8, :] = x
    y = w_ref[3:4, :] * x
    for k in range(CONV_W - 1):
        y = y + w_ref[k:k + 1, :] * xp_ref[pl.ds(5 + k, T), :]
    return y


def _conv_groups(x, w_ref):
    y = w_ref[3:4, :] * x
    for s in range(1, CONV_W):
        y = y + w_ref[3 - s:4 - s, :] * pltpu.roll(x, s, axis=0)
    return y


def _group_pos(shape):
    return lax.broadcasted_iota(jnp.int32, shape, 0) & (GROUP - 1)


def _ffn_ln_kernel(*refs, cast_weights, aliased):
    x_ref, wg_ref, wu_ref, wd_ref, g_ref, b_ref = refs[:6]
    n_in = 7 if aliased else 6
    o_ref = refs[n_in]
    xb_ref = refs[-1]
    j = pl.program_id(1)

    @pl.when(j == 0)
    def _():
        o_ref[...] = jnp.zeros_like(o_ref)
        xb_ref[...] = x_ref[...].astype(bf16)

    if cast_weights:
        wg, wu, wd = (r[...].astype(bf16) for r in (wg_ref, wu_ref, wd_ref))
        for w_out_ref, w in zip(refs[n_in + 1:n_in + 4], (wg, wu, wd)):
            w_out_ref[...] = w
    else:
        wg, wu, wd = wg_ref[...], wu_ref[...], wd_ref[...]
    xb = xb_ref[...]
    h = (_silu(_dot(xb, wg)) * _dot(xb, wu)).astype(bf16)
    o_ref[...] += _dot(h, wd)

    @pl.when(j == pl.num_programs(1) - 1)
    def _():
        y = ALPHA * x_ref[...] + 0.5 * o_ref[...]
        o_ref[...] = _layernorm(y, g_ref[...], b_ref[...])


def _ffn_ln(x, wg, wu, wd, g, b, *, tm, tf=512, row0=0, prev=None):
    M, D = x.shape
    F = wg.shape[1]
    rows = lambda i, j: (i + row0, 0)
    in_specs = [
        pl.BlockSpec((tm, D), rows, pipeline_mode=pl.Buffered(1)),
        pl.BlockSpec((D, tf), lambda i, j: (0, j)),
        pl.BlockSpec((D, tf), lambda i, j: (0, j)),
        pl.BlockSpec((tf, D), lambda i, j: (j, 0)),
        pl.BlockSpec((1, D), lambda i, j: (0, 0)),
        pl.BlockSpec((1, D), lambda i, j: (0, 0)),
    ]
    operands = [x, wg, wu, wd, g.reshape(1, D), b.reshape(1, D)]
    if prev is not None:
        in_specs.append(pl.BlockSpec(memory_space=pl.ANY))
        operands.append(prev)
    return pl.pallas_call(
        functools.partial(_ffn_ln_kernel, cast_weights=False, aliased=prev is not None),
        grid=(M // tm - row0, F // tf),
        in_specs=in_specs,
        out_specs=pl.BlockSpec((tm, D), rows),
        out_shape=jax.ShapeDtypeStruct((M, D), f32),
        input_output_aliases={6: 0} if prev is not None else {},
        scratch_shapes=[pltpu.VMEM((tm, D), bf16)],
        compiler_params=_cparams(("parallel", "arbitrary")),
        name="ffn_ln",
    )(*operands)


def _ffn_ln_head(x, wg, wu, wd, d, g, b, *, tm, tf=256):
    M, D = x.shape
    F = wg.shape[2]
    return pl.pallas_call(
        functools.partial(_ffn_ln_kernel, cast_weights=True, aliased=False),
        grid=(1, F // tf),
        in_specs=[
            pl.BlockSpec((tm, D), lambda i, j: (0, 0), pipeline_mode=pl.Buffered(1)),
            pl.BlockSpec((None, D, tf), lambda i, j: (d, 0, j)),
            pl.BlockSpec((None, D, tf), lambda i, j: (d, 0, j)),
            pl.BlockSpec((None, tf, D), lambda i, j: (d, j, 0)),
            pl.BlockSpec((1, D), lambda i, j: (0, 0)),
            pl.BlockSpec((1, D), lambda i, j: (0, 0)),
        ],
        out_specs=[pl.BlockSpec((tm, D), lambda i, j: (0, 0)),
                   pl.BlockSpec((D, tf), lambda i, j: (0, j)),
                   pl.BlockSpec((D, tf), lambda i, j: (0, j)),
                   pl.BlockSpec((tf, D), lambda i, j: (j, 0))],
        out_shape=[jax.ShapeDtypeStruct((M, D), f32), jax.ShapeDtypeStruct((D, F), bf16),
                   jax.ShapeDtypeStruct((D, F), bf16), jax.ShapeDtypeStruct((F, D), bf16)],
        scratch_shapes=[pltpu.VMEM((tm, D), bf16)],
        compiler_params=_cparams(("arbitrary", "arbitrary")),
        name="ffn_ln_head",
    )(x, wg, wu, wd, g.reshape(1, D), b.reshape(1, D))


def _ffn_ln_both(xp, xs, w32, d, g, b):
    out, *wb = _ffn_ln_head(xp, *w32, d, g, b, tm=PROMPT_ROW_TILE)
    yp = _ffn_ln(xp, *wb, g, b, tm=PROMPT_ROW_TILE, row0=1, prev=out)
    ys = _ffn_ln(xs, *wb, g, b, tm=ROW_TILE)
    return yp, ys


def _load_wt(wt_ref, w_scr, shift):
    t = wt_ref[...].T
    w_scr[...] = t[:, shift:shift + w_scr.shape[1]]


def _inproj_t_kernel(x_ref, wt_ref, o_ref, w_scr, *, shift):
    @pl.when(pl.program_id(0) == 0)
    def _():
        _load_wt(wt_ref, w_scr, shift)

    o_ref[...] = _dot(x_ref[...].astype(bf16), w_scr[...])


def _inproj_t(x, wt, d, *, slab_rows, shift=0, n_out=MIX_W, tm=ROW_TILE):
    M, D = x.shape
    return pl.pallas_call(
        functools.partial(_inproj_t_kernel, shift=shift),
        grid=(M // tm,),
        in_specs=[pl.BlockSpec((tm, D), lambda i: (i, 0)),
                  pl.BlockSpec((None, slab_rows, D), lambda i: (d, 0, 0), pipeline_mode=pl.Buffered(1))],
        out_specs=pl.BlockSpec((tm, n_out), lambda i: (i, 0)),
        out_shape=jax.ShapeDtypeStruct((M, n_out), f32),
        scratch_shapes=[pltpu.VMEM((D, n_out), bf16)],
        compiler_params=_cparams(("arbitrary",)),
        name="in_proj_t",
    )(x, wt)


def _inproj_n_kernel(x_ref, w_ref, o_ref):
    o_ref[...] = _dot(x_ref[...].astype(bf16), w_ref[...])


def _inproj_n(x, w, d, *, tm=ROW_TILE):
    M, D = x.shape
    N = w.shape[2]
    return pl.pallas_call(
        _inproj_n_kernel,
        grid=(M // tm,),
        in_specs=[pl.BlockSpec((tm, D), lambda i: (i, 0)),
                  pl.BlockSpec((None, D, N), lambda i: (d, 0, 0))],
        out_specs=pl.BlockSpec((tm, N), lambda i: (i, 0)),
        out_shape=jax.ShapeDtypeStruct((M, N), f32),
        compiler_params=_cparams(("parallel",)),
        name="in_proj_n",
    )(x, w)


def _inproj_gdn_kernel(x_ref, wt_ref, cw_ref, o_ref, tails_ref, w_scr, acc_a, acc_b, xp_scr, tail_scr,
                       *, blocks_per_seq):
    i = pl.program_id(0)
    tm = x_ref.shape[0]

    @pl.when(i == 0)
    def _():
        _load_wt(wt_ref, w_scr, 0)
        acc_b[...] = jnp.zeros_like(acc_b)
        tail_scr[...] = jnp.zeros_like(tail_scr)

    def body(acc_w, acc_r):
        xb = x_ref[...].astype(bf16)
        seq_start = lax.rem(i - 1, blocks_per_seq) == 0
        for c in range(4):
            cols = slice(c * QK, (c + 1) * QK)
            acc_w[:, cols] = _dot(xb, w_scr[:, cols])
            raw = acc_r[:, cols]
            if c == 3:
                o_ref[:, cols] = raw
                continue
            tail = jnp.where(seq_start, 0.0, tail_scr[:, cols])
            y = _silu(_conv_carry(raw, cw_ref[:, cols], xp_scr.at[:, cols], tail))
            last8 = raw[tm - 8:tm]
            tail_scr[:, cols] = last8
            tails_ref[:, cols] = last8
            if c == 2:
                o_ref[:, cols] = y
                continue
            for h in range(HEADS):
                hc = slice(c * QK + h * DK, c * QK + (h + 1) * DK)
                qk = _l2(y[:, h * DK:(h + 1) * DK])
                o_ref[:, hc] = qk * (DK ** -0.5) if c == 0 else qk

    even = (i & 1) == 0
    pl.when(even)(lambda: body(acc_a, acc_b))
    pl.when(jnp.logical_not(even))(lambda: body(acc_b, acc_a))


def _inproj_gdn(x, wt, d, conv_w, T, *, tm=CONV_ROW_TILE):
    M, D = x.shape
    nblk = M // tm
    bps = T // tm
    prev = lambda i: jnp.maximum(i - 1, 0)
    return pl.pallas_call(
        functools.partial(_inproj_gdn_kernel, blocks_per_seq=bps),
        grid=(nblk + 1,),
        in_specs=[pl.BlockSpec((tm, D), lambda i: (jnp.minimum(i, nblk - 1), 0)),
                  pl.BlockSpec((None, MIX_W, D), lambda i: (d, 0, 0), pipeline_mode=pl.Buffered(1)),
                  pl.BlockSpec((CONV_W, CONV_CH), lambda i: (0, 0))],
        out_specs=[pl.BlockSpec((tm, MIX_W), lambda i: (prev(i), 0)),
                   pl.BlockSpec((None, 8, CONV_CH), lambda i: (prev(i) // bps, 0, 0))],
        out_shape=[jax.ShapeDtypeStruct((M, MIX_W), f32),
                   jax.ShapeDtypeStruct((M // T, 8, CONV_CH), f32)],
        scratch_shapes=[pltpu.VMEM((D, MIX_W), bf16), pltpu.VMEM((tm, MIX_W), f32), pltpu.VMEM((tm, MIX_W), f32),
                        pltpu.VMEM((tm + 8, CONV_CH), f32), pltpu.VMEM((8, CONV_CH), f32)],
        compiler_params=_cparams(("arbitrary",)),
        name="in_proj_gdn",
    )(x, wt, conv_w)


def _outproj_ln_kernel(og_ref, hm_ref, ol_ref, x_ref, w_ref, g_ref, b_ref, o_ref):
    mix = (_dot(og_ref[...], w_ref[0:QK, :]) + _dot(hm_ref[...], w_ref[QK:2 * QK, :])
           + _dot(ol_ref[...], w_ref[2 * QK:, :]))
    o_ref[...] = _layernorm(ALPHA * x_ref[...] + mix, g_ref[...], b_ref[...])


def _outproj_ln(mix, x, w, d, g, b, *, tm=ROW_TILE):
    M, D = x.shape
    return pl.pallas_call(
        _outproj_ln_kernel,
        grid=(M // tm,),
        in_specs=[pl.BlockSpec((tm, wd), lambda i: (i, 0)) for wd in (QK, QK, LRU_W)]
        + [pl.BlockSpec((tm, D), lambda i: (i, 0)),
           pl.BlockSpec((None, D, D), lambda i: (d, 0, 0)),
           pl.BlockSpec((1, D), lambda i: (0, 0)),
           pl.BlockSpec((1, D), lambda i: (0, 0))],
        out_specs=pl.BlockSpec((tm, D), lambda i: (i, 0)),
        out_shape=jax.ShapeDtypeStruct((M, D), f32),
        compiler_params=_cparams(("parallel",)),
        name="out_proj_ln",
    )(*mix, x, w, g.reshape(1, D), b.reshape(1, D))


def _gdn_gates(gates, gc_ref):
    beta = jax.nn.sigmoid(gates)
    gl = -jnp.exp(gc_ref[0:1, :]) * _softplus(gates + gc_ref[1:2, :])
    return beta, gl


def _mlstm_gates(gates, gc_ref):
    i_pre = gates + gc_ref[2:3, :]
    logf = -_softplus(-(gates + gc_ref[3:4, :]))
    return i_pre, logf


def _gdn_chunk(q3, k3, v3, b3, g3, S):
    C = q3.shape[1]
    ii, jj = _ij(C)
    g_row = _cumsum_row(g3, ii, jj)
    g_col = _row2col(g_row, ii, jj)
    decay = jnp.exp(jnp.where((ii >= jj)[None], g_col - g_row, NEG))
    kb = k3.astype(bf16)
    kk = _bmm_nt(kb, kb)
    p = jnp.where((ii > jj)[None], -(b3 * kk * decay), 0.0)
    e = p
    for _ in range(int(math.log2(C)) - 1):
        pb = p.astype(bf16)
        p = _bmm(pb, pb)
        e = e + p + _bmm(e.astype(bf16), p.astype(bf16))
    eb = e.astype(bf16)
    e_g = jnp.exp(g_col)
    rv = b3 * v3
    rk = (b3 * e_g) * k3
    uv = rv + _bmm(eb, rv.astype(bf16))
    wks = (rk + _bmm(eb, rk.astype(bf16))).astype(bf16)
    qk = (_bmm_nt(q3.astype(bf16), kb) * decay).astype(bf16)
    g_last = g_row[:, :, C - 1:C]
    qd = (q3 * e_g).astype(bf16)
    kt = (k3 * jnp.exp(g_last - g_col)).astype(bf16)
    Sb = S.astype(bf16)
    Ub = (uv - _bmm(wks, Sb)).astype(bf16)
    o = _bmm(qd, Sb) + _bmm(qk, Ub)
    return o, jnp.exp(g_last) * S + _bmm_tn(kt, Ub)


def _mlstm_chunk(q3, k3, vb, i3, f3, Cm, nr, m):
    C = q3.shape[1]
    ii, jj = _ij(C)
    b_row = _cumsum_row(f3, ii, jj)
    b_col = _row2col(b_row, ii, jj)
    i_row = _col2row(i3, ii, jj)
    D = jnp.where((ii >= jj)[None], b_col - b_row + i_row, NEG)
    d_max = jnp.max(D, axis=2, keepdims=True)
    qb = q3.astype(bf16)
    qk = _bmm_nt(qb, k3.astype(bf16))
    m_t = jnp.maximum(b_col + m, d_max)
    inter = jnp.exp(b_col + m - m_t)
    Sw = jnp.exp(D - m_t) * qk
    num = inter * _bmm(qb, Cm.astype(bf16)) + _bmm(Sw.astype(bf16), vb)
    den = inter * jnp.sum(q3 * nr, -1, keepdims=True) + jnp.sum(Sw, -1, keepdims=True)
    hh = num / jnp.maximum(jnp.abs(den), jnp.exp(-m_t))
    m_new = m_t[:, C - 1:C]
    b_last = b_col[:, C - 1:C]
    kw = jnp.exp(b_last - b_col + i3 - m_new) * k3
    dec = jnp.exp(b_last + m - m_new)
    return (hh, dec * Cm + _bmm_tn(kw.astype(bf16), vb), dec * nr + jnp.sum(kw, axis=1, keepdims=True), m_new)


def _gdn_prompt_kernel(*refs, C, B):
    mains, gts = refs[0:2 * B:2], refs[1:2 * B:2]
    gc_ref, nw_ref, o_ref, s_ref = refs[2 * B:]
    n = pl.program_id(0)
    NC = B * HEADS

    @pl.when(n == 0)
    def _():
        s_ref[...] = jnp.zeros_like(s_ref)

    qs, ks, vs, zs, bs, gs = [], [], [], [], [], []
    for b in range(B):
        beta, gl = _gdn_gates(gts[b][...], gc_ref)
        for h in range(HEADS):
            qs.append(mains[b][:, h * DK:(h + 1) * DK])
            ks.append(mains[b][:, QK + h * DK:QK + (h + 1) * DK])
            vs.append(mains[b][:, 2 * QK + h * DK:2 * QK + (h + 1) * DK])
            zs.append(mains[b][:, CONV_CH + h * DK:CONV_CH + (h + 1) * DK])
            bs.append(beta[:, h:h + 1])
            gs.append(gl[:, HEADS + h:HEADS + h + 1])
    q3 = jnp.stack(qs)
    k3 = jnp.stack(ks)
    v3 = jnp.stack(vs)
    b3 = jnp.stack(bs)
    g3 = jnp.stack(gs)
    o, S = _gdn_chunk(q3, k3, v3, b3, g3, s_ref[...].reshape(NC, DK, DK))
    s_ref[...] = S.reshape(B, HEADS, DK, DK)
    o = _rms(o) * nw_ref[...] * _silu(jnp.stack(zs))
    for b in range(B):
        for h in range(HEADS):
            o_ref[b, :, h * DK:(h + 1) * DK] = o[b * HEADS + h].astype(o_ref.dtype)


def _gdn_prompt(Ug, Uc, B, T, hc, norm_w, *, C=CHUNK):
    N = T // C
    in_specs = []
    for b in range(B):
        in_specs.append(pl.BlockSpec((C, MIX_W), lambda n, b=b: (b * N + n, 0)))
        in_specs.append(pl.BlockSpec((C, DK), lambda n, b=b: (b * N + n, CB_GATES)))
    in_specs += [pl.BlockSpec((8, DK), lambda n: (0, 0)),
                 pl.BlockSpec((1, DK), lambda n: (0, 0))]
    o, S = pl.pallas_call(
        functools.partial(_gdn_prompt_kernel, C=C, B=B),
        grid=(N,),
        in_specs=in_specs,
        out_specs=[pl.BlockSpec((B, None, C, QK), lambda n: (0, n, 0, 0)),
                   pl.BlockSpec((B, HEADS, DK, DK), lambda n: (0, 0, 0, 0))],
        out_shape=[jax.ShapeDtypeStruct((B, N, C, QK), bf16),
                   jax.ShapeDtypeStruct((B, HEADS, DK, DK), f32)],
        compiler_params=_cparams(("arbitrary",)),
        name="gdn_prompt",
    )(*([Ug, Uc] * B), hc, norm_w.reshape(1, DK))
    return o.reshape(B * T, QK), S


def _mlstm_prompt_kernel(*refs, C, B):
    mains, gts = refs[0:2 * B:2], refs[1:2 * B:2]
    gc_ref, nw_ref, h_ref, c_ref, n_ref, m_ref = refs[2 * B:]
    n = pl.program_id(0)
    NC = B * HEADS

    @pl.when(n == 0)
    def _():
        c_ref[...] = jnp.zeros_like(c_ref)
        n_ref[...] = jnp.zeros_like(n_ref)
        m_ref[...] = jnp.zeros_like(m_ref)

    qs, ks, vs, os_, is_, fs = [], [], [], [], [], []
    for b in range(B):
        i_pre, logf = _mlstm_gates(gts[b][...], gc_ref)
        for h in range(HEADS):
            qs.append(mains[b][:, h * DK:(h + 1) * DK])
            ks.append(mains[b][:, QK + h * DK:QK + (h + 1) * DK])
            vs.append(mains[b][:, 2 * QK + h * DK:2 * QK + (h + 1) * DK])
            os_.append(mains[b][:, 3 * QK + h * DK:3 * QK + (h + 1) * DK])
            is_.append(i_pre[:, 2 * HEADS + h:2 * HEADS + h + 1])
            fs.append(logf[:, 3 * HEADS + h:3 * HEADS + h + 1])
    q3 = jnp.stack(qs)
    k3 = jnp.stack(ks) * (DK ** -0.5)
    vb = jnp.stack(vs).astype(bf16)
    i3 = jnp.stack(is_)
    f3 = jnp.stack(fs)
    hh, Cm, nr, m = _mlstm_chunk(q3, k3, vb, i3, f3, c_ref[...].reshape(NC, DK, DK),
                                 n_ref[...].reshape(NC, 1, DK), m_ref[...].reshape(NC, 1, DK)[:, :, 0:1])
    nw = jnp.stack([nw_ref[h] for _ in range(B) for h in range(HEADS)])
    hh = _rms(hh) * nw * jax.nn.sigmoid(jnp.stack(os_))
    c_ref[...] = Cm.reshape(B, HEADS, DK, DK)
    n_ref[...] = nr.reshape(B, HEADS, 1, DK)
    m_ref[...] = jnp.broadcast_to(m, (NC, 1, DK)).reshape(B, HEADS, 1, DK)
    for b in range(B):
        for h in range(HEADS):
            h_ref[b, :, h * DK:(h + 1) * DK] = hh[b * HEADS + h].astype(h_ref.dtype)


def _mlstm_prompt(Um, Uc, B, T, hc, norm_w, *, C=CHUNK):
    N = T // C
    in_specs = []
    for b in range(B):
        in_specs.append(pl.BlockSpec((C, MIX_W), lambda n, b=b: (b * N + n, 0)))
        in_specs.append(pl.BlockSpec((C, DK), lambda n, b=b: (b * N + n, CB_GATES)))
    in_specs += [pl.BlockSpec((8, DK), lambda n: (0, 0)),
                 pl.BlockSpec((HEADS, 1, DK), lambda n: (0, 0, 0))]
    st = lambda r: pl.BlockSpec((B, HEADS, r, DK), lambda n: (0, 0, 0, 0))
    h, Cm, nr, m = pl.pallas_call(
        functools.partial(_mlstm_prompt_kernel, C=C, B=B),
        grid=(N,),
        in_specs=in_specs,
        out_specs=[pl.BlockSpec((B, None, C, QK), lambda n: (0, n, 0, 0)), st(DK), st(1), st(1)],
        out_shape=[jax.ShapeDtypeStruct((B, N, C, QK), bf16),
                   jax.ShapeDtypeStruct((B, HEADS, DK, DK), f32),
                   jax.ShapeDtypeStruct((B, HEADS, 1, DK), f32),
                   jax.ShapeDtypeStruct((B, HEADS, 1, DK), f32)],
        compiler_params=_cparams(("arbitrary",)),
        name="mlstm_prompt",
    )(*([Um, Uc] * B), hc, norm_w.reshape(HEADS, 1, DK))
    return h.reshape(B * T, QK), Cm, nr, m


def _lru_gates(c, wr_ref, br_ref, wi_ref, bi_ref, lam_ref):
    cb = c.astype(bf16)
    r = jax.nn.sigmoid(_dot(cb, wr_ref[...]) + br_ref[...])
    ig = jax.nn.sigmoid(_dot(cb, wi_ref[...]) + bi_ref[...])
    log_a = -LRU_C * r * _softplus(-lam_ref[...])
    a = jnp.exp(log_a)
    u = jnp.sqrt(-_expm1(2.0 * log_a)) * (ig * c)
    return a, u


def _lru_prompt_kernel(x_ref, y_ref, cw_ref, cb_ref, wr_ref, br_ref, wi_ref, bi_ref, lam_ref,
                       o_ref, hl_ref, xp_ref, tail_ref, hc_ref):
    tc = pl.program_id(1)
    Tc, W = x_ref.shape

    @pl.when(tc == 0)
    def _():
        tail_ref[...] = jnp.zeros_like(tail_ref)
        hc_ref[...] = jnp.zeros_like(hc_ref)

    x = x_ref[...]
    c = _conv_carry(x, cw_ref, xp_ref, tail_ref[...]) + cb_ref[...]
    tail_ref[...] = x[Tc - 8:Tc]
    a, u = _lru_gates(c, wr_ref, br_ref, wi_ref, bi_ref, lam_ref)
    row = lax.broadcasted_iota(jnp.int32, (Tc, W), 0)
    d = 1
    while d < Tc:
        keep = row >= d
        a_sh = jnp.where(keep, pltpu.roll(a, d, axis=0), 1.0)
        u_sh = jnp.where(keep, pltpu.roll(u, d, axis=0), 0.0)
        u = a * u_sh + u
        a = a * a_sh
        d *= 2
    hs = u + a * hc_ref[0:1, :]
    last = hs[Tc - 1:Tc]
    hc_ref[...] = jnp.broadcast_to(last, hc_ref.shape)
    hl_ref[...] = last
    o_ref[...] = (hs * jax.nn.gelu(y_ref[...])).astype(o_ref.dtype)


def _lru_prompt(U, B, T, cw, cb, wr, br, wi, bi, lam, *, Tc=512):
    W = LRU_W
    nt = T // Tc
    vec = lambda: pl.BlockSpec((1, W), lambda b, t: (0, 0))
    return pl.pallas_call(
        _lru_prompt_kernel,
        grid=(B, nt),
        in_specs=[pl.BlockSpec((Tc, W), lambda b, t: (b * nt + t, CB_LX)),
                  pl.BlockSpec((Tc, W), lambda b, t: (b * nt + t, CB_LY)),
                  pl.BlockSpec((CONV_W, W), lambda b, t: (0, 0)), vec(),
                  pl.BlockSpec((W, W), lambda b, t: (0, 0)), vec(),
                  pl.BlockSpec((W, W), lambda b, t: (0, 0)), vec(), vec()],
        out_specs=[pl.BlockSpec((Tc, W), lambda b, t: (b * nt + t, 0)),
                   pl.BlockSpec((None, 1, W), lambda b, t: (b, 0, 0))],
        out_shape=[jax.ShapeDtypeStruct((B * T, W), bf16),
                   jax.ShapeDtypeStruct((B, 1, W), f32)],
        scratch_shapes=[pltpu.VMEM((Tc + 8, W), f32), pltpu.VMEM((8, W), f32), pltpu.VMEM((8, W), f32)],
        compiler_params=_cparams(("parallel", "arbitrary")),
        name="lru_prompt",
    )(U, U, cw, cb.reshape(1, W), wr, br.reshape(1, W), wi, bi.reshape(1, W), lam.reshape(1, W))


def _gdn_sample_kernel(q_ref, k_ref, v_ref, z_ref, gt_ref, wq_ref, wk_ref, wv_ref, gc_ref, nw_ref, s0_ref,
                       *rest):
    o_ref, s_ref = rest[-2:]
    h = pl.program_id(0)
    BB = s0_ref.shape[0]
    R = BB * GROUP
    q = _l2(_silu(_conv_groups(q_ref[...], wq_ref))) * (DK ** -0.5)
    k = _l2(_silu(_conv_groups(k_ref[...], wk_ref)))
    v = _silu(_conv_groups(v_ref[...], wv_ref))
    beta, gl = _gdn_gates(gt_ref[...], gc_ref)
    is_tok = _group_pos((R, 1)) >= TOK0
    beta = jnp.where(is_tok, _colsel(beta, h), 0.0)
    gl = jnp.where(is_tok, _colsel(gl, HEADS + h), 0.0)
    grp = lambda a: a.reshape(BB, GROUP, a.shape[-1])
    o, S = _gdn_chunk(grp(q), grp(k), grp(v), grp(beta), grp(gl), s0_ref[...])
    s_ref[...] = S
    o_ref[...] = (_rms(o.reshape(R, DK)) * nw_ref[...] * _silu(z_ref[...])).astype(o_ref.dtype)


def _stacked_out(prev):
    if prev is None:
        return [], [], {}
    return [pl.BlockSpec(memory_space=pl.ANY)], [prev], None


def _gdn_sample(U8, U8c, S_all, d, S_prev, conv_w, hc, norm_w, *, BB=SAMPLE_SEQS_PER_STEP):
    B = S_all.shape[1]
    R = BB * GROUP
    tok = lambda cb: pl.BlockSpec((R, DK), lambda h, b: (b, cb + h))
    cw = lambda cb: pl.BlockSpec((CONV_W, DK), lambda h, b: (0, cb + h))
    st = pl.BlockSpec((None, BB, None, DK, DK), lambda h, b: (d, b, h, 0, 0))
    extra_specs, extra_ops, _ = _stacked_out(S_prev)
    n_in = 11
    return pl.pallas_call(
        _gdn_sample_kernel,
        grid=(HEADS, B // BB),
        in_specs=[tok(CB_Q), tok(CB_K), tok(CB_V), tok(CB_Z),
                  pl.BlockSpec((R, DK), lambda h, b: (b, CB_GATES)),
                  cw(CB_Q), cw(CB_K), cw(CB_V),
                  pl.BlockSpec((8, DK), lambda h, b: (0, 0)),
                  pl.BlockSpec((1, DK), lambda h, b: (0, 0)), st] + extra_specs,
        out_specs=[pl.BlockSpec((R, DK), lambda h, b: (b, h)), st],
        out_shape=[jax.ShapeDtypeStruct((B * GROUP, QK), bf16),
                   jax.ShapeDtypeStruct(S_all.shape, f32)],
        input_output_aliases={n_in: 1} if S_prev is not None else {},
        compiler_params=_cparams(("parallel", "parallel")),
        name="gdn_sample",
    )(U8, U8, U8, U8, U8c, conv_w, conv_w, conv_w, hc, norm_w.reshape(1, DK), S_all, *extra_ops)


def _mlstm_sample_kernel(q_ref, k_ref, v_ref, og_ref, gt_ref, gc_ref, nw_ref, c0_ref, n0_ref, m0_ref,
                         *rest):
    h_ref, c_ref, n_ref, m_ref = rest[-4:]
    h = pl.program_id(0)
    BB = c0_ref.shape[0]
    R = BB * GROUP
    i_pre, logf = _mlstm_gates(gt_ref[...], gc_ref)
    is_tok = _group_pos((R, 1)) >= TOK0
    i_pre = jnp.where(is_tok, _colsel(i_pre, 2 * HEADS + h), NEG)
    logf = jnp.where(is_tok, _colsel(logf, 3 * HEADS + h), 0.0)
    grp = lambda a: a.reshape(BB, GROUP, a.shape[-1])
    old = lambda ref: grp(ref[...])[:, TOK0 - 1:TOK0, :]
    hh, Cm, nr, m = _mlstm_chunk(grp(q_ref[...]), grp(k_ref[...] * (DK ** -0.5)), grp(v_ref[...]).astype(bf16),
                                 grp(i_pre), grp(logf), c0_ref[...], old(n0_ref), old(m0_ref)[:, :, 0:1])
    c_ref[...] = Cm
    n_ref[...] = jnp.broadcast_to(nr, (BB, GROUP, DK)).reshape(R, DK)
    m_ref[...] = jnp.broadcast_to(m, (BB, GROUP, DK)).reshape(R, DK)
    h_ref[...] = (_rms(hh.reshape(R, DK)) * nw_ref[h] * jax.nn.sigmoid(og_ref[...])).astype(h_ref.dtype)


def _mlstm_sample(U8, U8c, C_all, d, C_prev, n0g, m0g, hc, norm_w, *, BB=SAMPLE_SEQS_PER_STEP):
    B = C_all.shape[1]
    R = BB * GROUP
    tok = lambda cb: pl.BlockSpec((R, DK), lambda h, b: (b, cb + h))
    row = pl.BlockSpec((R, DK), lambda h, b: (b, h))
    st = pl.BlockSpec((None, BB, None, DK, DK), lambda h, b: (d, b, h, 0, 0))
    extra_specs, extra_ops, _ = _stacked_out(C_prev)
    n_in = 10
    return pl.pallas_call(
        _mlstm_sample_kernel,
        grid=(HEADS, B // BB),
        in_specs=[tok(CB_Q), tok(CB_K), tok(CB_V), tok(CB_Z),
                  pl.BlockSpec((R, DK), lambda h, b: (b, CB_GATES)),
                  pl.BlockSpec((8, DK), lambda h, b: (0, 0)),
                  pl.BlockSpec((HEADS, 1, DK), lambda h, b: (0, 0, 0)),
                  st, row, row] + extra_specs,
        out_specs=[row, st, row, row],
        out_shape=[jax.ShapeDtypeStruct((B * GROUP, QK), bf16),
                   jax.ShapeDtypeStruct(C_all.shape, f32),
                   jax.ShapeDtypeStruct((B * GROUP, QK), f32),
                   jax.ShapeDtypeStruct((B * GROUP, QK), f32)],
        input_output_aliases={n_in: 1} if C_prev is not None else {},
        compiler_params=_cparams(("parallel", "parallel")),
        name="mlstm_sample",
    )(U8, U8, U8, U8, U8c, hc, norm_w.reshape(HEADS, 1, DK), C_all, n0g, m0g, *extra_ops)


def _lru_sample_kernel(x_ref, y_ref, h0_ref, cw_ref, cb_ref, wr_ref, br_ref, wi_ref, bi_ref, lam_ref,
                       o_ref, h_ref):
    c = _conv_groups(x_ref[...], cw_ref) + cb_ref[...]
    a, u = _lru_gates(c, wr_ref, br_ref, wi_ref, bi_ref, lam_ref)
    hs = h0_ref[...]
    pos = _group_pos(hs.shape)
    for t in range(GROUP - TOK0):
        hs = jnp.where(pos == TOK0 + t, a * pltpu.roll(hs, 1, axis=0) + u, hs)
    h_ref[...] = hs
    o_ref[...] = (hs * jax.nn.gelu(y_ref[...])).astype(o_ref.dtype)


def _lru_sample(U8, h0g, cw, cb, wr, br, wi, bi, lam, *, R=128):
    W = LRU_W
    rows = U8.shape[0]
    vec = lambda: pl.BlockSpec((1, W), lambda i: (0, 0))
    return pl.pallas_call(
        _lru_sample_kernel,
        grid=(rows // R,),
        in_specs=[pl.BlockSpec((R, W), lambda i: (i, CB_LX)),
                  pl.BlockSpec((R, W), lambda i: (i, CB_LY)),
                  pl.BlockSpec((R, W), lambda i: (i, 0)),
                  pl.BlockSpec((CONV_W, W), lambda i: (0, 0)), vec(),
                  pl.BlockSpec((W, W), lambda i: (0, 0)), vec(),
                  pl.BlockSpec((W, W), lambda i: (0, 0)), vec(), vec()],
        out_specs=[pl.BlockSpec((R, W), lambda i: (i, 0)), pl.BlockSpec((R, W), lambda i: (i, 0))],
        out_shape=[jax.ShapeDtypeStruct((rows, W), bf16), jax.ShapeDtypeStruct((rows, W), f32)],
        compiler_params=_cparams(("parallel",)),
        name="lru_sample",
    )(U8, U8, h0g, cw, cb.reshape(1, W), wr, br.reshape(1, W), wi, bi.reshape(1, W), lam.reshape(1, W))


def _prep_w_in(w_in):
    wt = jnp.swapaxes(w_in, 1, 2).astype(bf16)
    wm = wt[:, MLSTM_SLAB0:MLSTM_SLAB0 + MLSTM_SLAB_ROWS, :]
    pad = jnp.zeros(w_in.shape[:2] + (DK - 4 * HEADS,), w_in.dtype)
    ws = jnp.concatenate([w_in[..., SRC_LRU:SRC_LRU + 2 * LRU_W],
                          w_in[..., SRC_GATES_G:SRC_GATES_G + 2 * HEADS],
                          w_in[..., SRC_GATES_M:SRC_GATES_M + 2 * HEADS], pad], axis=-1).astype(bf16)
    return wt, wm, ws


def _inproj_all(x, w_in_parts, d, conv_w=None, T=None):
    wt, wm, ws = w_in_parts
    um = _inproj_t(x, wm, d, slab_rows=MLSTM_SLAB_ROWS, shift=SRC_MLSTM - MLSTM_SLAB0)
    us = _inproj_n(x, ws, d)
    if conv_w is None:
        return _inproj_t(x, wt, d, slab_rows=MIX_W), um, us
    ug, tails = _inproj_gdn(x, wt, d, conv_w, T)
    return ug, um, us, tails


def _block_diag(w):
    nb, d, e = w.shape
    eye = jnp.eye(nb, dtype=w.dtype)
    return (w[:, :, None, :] * eye[:, None, :, None]).reshape(nb * d, nb * e)


def _gate_consts(a_log, dt_bias, i_bias, f_bias):
    row = lambda v, k: jnp.pad(v, (k * HEADS, DK - (k + 1) * HEADS))
    rows = [row(a_log, 1), row(dt_bias, 1), row(i_bias, 2), row(f_bias, 3)]
    return jnp.stack(rows + [jnp.zeros((DK,), f32)] * (8 - len(rows)))


def _state_row(state):
    B, W = state.shape
    return jnp.pad(state[:, None, :], ((0, 0), (TOK0 - 1, GROUP - TOK0), (0, 0))).reshape(B * GROUP, W)


def _tail_rows(U, B, T, col, width):
    return jnp.stack([lax.slice(U, ((b + 1) * T - (CONV_W - 1), col), ((b + 1) * T, col + width))
                      for b in range(B)])


def kernel(x_prompt, x_sample, state_gdn_S, state_gdn_conv, state_mlstm_C, state_mlstm_n, state_mlstm_m, state_lru_h, state_lru_conv, ffn1_wg, ffn1_wu, ffn1_wd, ln_g, ln_b, w_in, gdn_conv_w, gdn_A_log, gdn_dt_bias, gdn_norm_w, mlstm_i_bias, mlstm_f_bias, mlstm_norm_w, lru_conv_w, lru_conv_b, lru_wr, lru_br, lru_wi, lru_bi, lru_lambda, w_out, ffn2_wg, ffn2_wu, ffn2_wd):
    BP, TP, D = x_prompt.shape
    BS, TS, _ = x_sample.shape
    MP, MS = BP * TP, BS * TS
    ffn1 = (ffn1_wg, ffn1_wu, ffn1_wd)
    ffn2 = (ffn2_wg, ffn2_wu, ffn2_wd)
    w_in_parts = _prep_w_in(w_in)
    w_out_b = w_out.astype(bf16)

    xp, xs = x_prompt.reshape(MP, D), x_sample.reshape(MS, D)
    outs_p = [[] for _ in range(7)]
    outs_s = [[] for _ in range(7)]
    gS_s = mC_s = None
    for d in range(DEPTH):
        x1p, x1s = _ffn_ln_both(xp, xs, ffn1, d, ln_g[d, 0], ln_b[d, 0])
        Ug, Um, Uc, tails = _inproj_all(x1p, w_in_parts, d, gdn_conv_w[d], TP)
        hc = _gate_consts(gdn_A_log[d], gdn_dt_bias[d], mlstm_i_bias[d], mlstm_f_bias[d])
        wr = _block_diag(lru_wr[d]).astype(bf16)
        wi = _block_diag(lru_wi[d]).astype(bf16)
        lru_w = (lru_conv_w[d], lru_conv_b[d], wr, lru_br[d], wi, lru_bi[d], lru_lambda[d])

        og_p, gS_p = _gdn_prompt(Ug, Uc, BP, TP, hc, gdn_norm_w[d])
        hm_p, mC_p, mn_p, mm_p = _mlstm_prompt(Um, Uc, BP, TP, hc, mlstm_norm_w[d])
        ol_p, lh_p = _lru_prompt(Uc, BP, TP, *lru_w)
        outs_p[0].append(gS_p)
        outs_p[1].append(tails[:, 8 - (CONV_W - 1):, :])
        outs_p[2].append(mC_p)
        outs_p[3].append(mn_p.reshape(BP, HEADS, DK))
        outs_p[4].append(mm_p[:, :, 0, 0])
        outs_p[5].append(lh_p.reshape(BP, LRU_W))
        outs_p[6].append(_tail_rows(Uc, BP, TP, 0, LRU_W))

        Us_g, Us_m, Us_c = (u.reshape(BS, TS, -1) for u in _inproj_all(x1s, w_in_parts, d))

        def groups(us, hist=None):
            width = us.shape[-1]
            if hist is None:
                head = jnp.zeros((BS, TOK0, width), f32)
            else:
                head = jnp.pad(hist, ((0, 0), (TOK0 - hist.shape[1], 0), (0, width - hist.shape[2])))
            return jnp.concatenate([head, us], axis=1).reshape(BS * GROUP, width)

        U8g, U8m, U8c = groups(Us_g, state_gdn_conv[d]), groups(Us_m), groups(Us_c, state_lru_conv[d])
        og_s, gS_s = _gdn_sample(U8g, U8c, state_gdn_S, d, gS_s, gdn_conv_w[d], hc, gdn_norm_w[d])
        hm_s, mC_s, n8, m8 = _mlstm_sample(
            U8m, U8c, state_mlstm_C, d, mC_s, _state_row(state_mlstm_n[d].reshape(BS, QK)),
            _state_row(jnp.repeat(state_mlstm_m[d], DK, axis=1)), hc, mlstm_norm_w[d])
        ol_s, h8 = _lru_sample(U8c, _state_row(state_lru_h[d]), *lru_w)
        toks = lambda a: a.reshape(BS, GROUP, -1)[:, TOK0:].reshape(MS, -1)
        last = lambda a: a.reshape(BS, GROUP, -1)[:, GROUP - 1]
        outs_s[1].append(Us_g[:, TS - 3:, 0:CONV_CH])
        outs_s[3].append(last(n8).reshape(BS, HEADS, DK))
        outs_s[4].append(last(m8).reshape(BS, HEADS, DK)[:, :, 0])
        outs_s[5].append(last(h8))
        outs_s[6].append(Us_c[:, TS - 3:, 0:LRU_W])

        x2p = _outproj_ln((og_p, hm_p, ol_p), x1p, w_out_b, d, ln_g[d, 1], ln_b[d, 1])
        x2s = _outproj_ln((toks(og_s), toks(hm_s), toks(ol_s)), x1s, w_out_b, d, ln_g[d, 1], ln_b[d, 1])
        xp, xs = _ffn_ln_both(x2p, x2s, ffn2, d, ln_g[d, 2], ln_b[d, 2])

    y_prompt = xp.reshape(BP, TP, D)
    y_sample = xs.reshape(BS, TS, D)
    stack = lambda o: jnp.stack(o)
    return (y_prompt, y_sample,
            *[stack(o) for o in outs_p],
            gS_s, stack(outs_s[1]), mC_s, *[stack(outs_s[i]) for i in range(3, 7)])
```

```python
import functools
import math

import jax
import jax.numpy as jnp
from jax import lax
from jax.experimental import pallas as pl
from jax.experimental.pallas import tpu as pltpu

f32 = jnp.float32
bf16 = jnp.bfloat16

DEPTH = 2
D_MODEL = 2048
D_FF = 5632
HEADS = 6
DK = 128
LRU_W = 512
LRU_BLOCKS = 8
LRU_C = 8.0
CONV_W = 4
CHUNK = 64
ALPHA = (2 * DEPTH) ** 0.25
NORM_EPS = 1e-6
NEG = -1e30

QK = HEADS * DK
CONV_CH = 3 * QK
MIX_W = 4 * QK
D_IN = 7192
D_INP = 7296
CB_Q, CB_K, CB_V, CB_Z = 0, 6, 12, 18
CB_GATES_G = 24
CB_GATES_M = 48
SHIFT_M = 12
COL_LX = 6168
SHIFT_L = 24
CB_L = 12
CB_L_TAIL = 56
GROUP = 8
TOK0 = 4
ROW_TILE = 512
PROMPT_ROW_TILE = 1024
SAMPLE_SEQS_PER_STEP = 32

VMEM_LIMIT = 60 * 1024 * 1024


def _cparams(sem):
    return pltpu.CompilerParams(dimension_semantics=sem, vmem_limit_bytes=VMEM_LIMIT)


def _silu(x):
    return x * jax.nn.sigmoid(x)


def _softplus(x):
    return jnp.maximum(x, 0.0) + jnp.log1p(jnp.exp(-jnp.abs(x)))


def _expm1(x):
    u = jnp.exp(x)
    um1 = u - 1.0
    lg = jnp.where(u == 1.0, 1.0, jnp.log(jnp.where(u == 0.0, 1.0, u)))
    return jnp.where(u == 1.0, x, jnp.where(u == 0.0, -1.0, um1 * x / lg))


def _layernorm(y, g, b):
    mu = jnp.mean(y, -1, keepdims=True)
    d = y - mu
    var = jnp.mean(d * d, -1, keepdims=True)
    return d * lax.rsqrt(var + NORM_EPS) * g + b


def _rms(x):
    return x * lax.rsqrt(jnp.mean(x * x, -1, keepdims=True) + NORM_EPS)


def _l2(x):
    return x * lax.rsqrt(jnp.sum(x * x, -1, keepdims=True) + NORM_EPS)


def _colsel(gates, idx):
    lane = lax.broadcasted_iota(jnp.int32, gates.shape, 1)
    return jnp.sum(jnp.where(lane == idx, gates, 0.0), axis=1, keepdims=True)


def _dot(a, b):
    return jnp.dot(a, b, preferred_element_type=f32)


def _bmm(a, b):
    return jnp.einsum('nij,njk->nik', a, b, preferred_element_type=f32)


def _bmm_nt(a, b):
    return jnp.einsum('nid,njd->nij', a, b, preferred_element_type=f32)


def _bmm_tn(a, b):
    return jnp.einsum('nck,ncv->nkv', a, b, preferred_element_type=f32)


def _ij(C):
    return (lax.broadcasted_iota(jnp.int32, (C, C), 0), lax.broadcasted_iota(jnp.int32, (C, C), 1))


def _cumsum_row(col3, ii, jj):
    return jnp.sum(jnp.where((ii <= jj)[None], col3, 0.0), axis=1, keepdims=True)


def _row2col(row3, ii, jj):
    return jnp.sum(jnp.where((ii == jj)[None], row3, 0.0), axis=2, keepdims=True)


def _col2row(col3, ii, jj):
    return jnp.sum(jnp.where((ii == jj)[None], col3, 0.0), axis=1, keepdims=True)


def _conv_carry(x, w_ref, xp_ref, tail):
    T = x.shape[0]
    xp_ref[0:8, :] = tail
    xp_ref[8:T + 8, :] = x
    y = w_ref[3:4, :] * x
    for k in range(CONV_W - 1):
        y = y + w_ref[k:k + 1, :] * xp_ref[pl.ds(5 + k, T), :]
    return y


def _conv_groups(x, w_ref):
    y = w_ref[3:4, :] * x
    for s in range(1, CONV_W):
        y = y + w_ref[3 - s:4 - s, :] * pltpu.roll(x, s, axis=0)
    return y


def _group_pos(shape):
    return lax.broadcasted_iota(jnp.int32, shape, 0) & (GROUP - 1)


def _ffn_ln_kernel(*refs, cast_weights, aliased):
    x_ref, wg_ref, wu_ref, wd_ref, g_ref, b_ref = refs[:6]
    n_in = 7 if aliased else 6
    o_ref = refs[n_in]
    xb_ref = refs[-1]
    j = pl.program_id(1)

    @pl.when(j == 0)
    def _():
        o_ref[...] = jnp.zeros_like(o_ref)
        xb_ref[...] = x_ref[...].astype(bf16)

    if cast_weights:
        wg, wu, wd = (r[...].astype(bf16) for r in (wg_ref, wu_ref, wd_ref))
        for w_out_ref, w in zip(refs[n_in + 1:n_in + 4], (wg, wu, wd)):
            w_out_ref[...] = w
    else:
        wg, wu, wd = wg_ref[...], wu_ref[...], wd_ref[...]
    xb = xb_ref[...]
    h = (_silu(_dot(xb, wg)) * _dot(xb, wu)).astype(bf16)
    o_ref[...] += _dot(h, wd)

    @pl.when(j == pl.num_programs(1) - 1)
    def _():
        y = ALPHA * x_ref[...] + 0.5 * o_ref[...]
        o_ref[...] = _layernorm(y, g_ref[...], b_ref[...])


def _ffn_ln(x, wg, wu, wd, g, b, *, tm, tf=512, row0=0, prev=None):
    M, D = x.shape
    F = wg.shape[1]
    rows = lambda i, j: (i + row0, 0)
    in_specs = [
        pl.BlockSpec((tm, D), rows, pipeline_mode=pl.Buffered(1)),
        pl.BlockSpec((D, tf), lambda i, j: (0, j)),
        pl.BlockSpec((D, tf), lambda i, j: (0, j)),
        pl.BlockSpec((tf, D), lambda i, j: (j, 0)),
        pl.BlockSpec((1, D), lambda i, j: (0, 0)),
        pl.BlockSpec((1, D), lambda i, j: (0, 0)),
    ]
    operands = [x, wg, wu, wd, g.reshape(1, D), b.reshape(1, D)]
    if prev is not None:
        in_specs.append(pl.BlockSpec(memory_space=pl.ANY))
        operands.append(prev)
    return pl.pallas_call(
        functools.partial(_ffn_ln_kernel, cast_weights=False, aliased=prev is not None),
        grid=(M // tm - row0, F // tf),
        in_specs=in_specs,
        out_specs=pl.BlockSpec((tm, D), rows),
        out_shape=jax.ShapeDtypeStruct((M, D), f32),
        input_output_aliases={6: 0} if prev is not None else {},
        scratch_shapes=[pltpu.VMEM((tm, D), bf16)],
        compiler_params=_cparams(("parallel", "arbitrary")),
        name="ffn_ln",
    )(*operands)


def _ffn_ln_head(x, wg, wu, wd, d, g, b, *, tm, tf=256):
    M, D = x.shape
    F = wg.shape[2]
    return pl.pallas_call(
        functools.partial(_ffn_ln_kernel, cast_weights=True, aliased=False),
        grid=(1, F // tf),
        in_specs=[
            pl.BlockSpec((tm, D), lambda i, j: (0, 0), pipeline_mode=pl.Buffered(1)),
            pl.BlockSpec((None, D, tf), lambda i, j: (d, 0, j)),
            pl.BlockSpec((None, D, tf), lambda i, j: (d, 0, j)),
            pl.BlockSpec((None, tf, D), lambda i, j: (d, j, 0)),
            pl.BlockSpec((1, D), lambda i, j: (0, 0)),
            pl.BlockSpec((1, D), lambda i, j: (0, 0)),
        ],
        out_specs=[pl.BlockSpec((tm, D), lambda i, j: (0, 0)),
                   pl.BlockSpec((D, tf), lambda i, j: (0, j)),
                   pl.BlockSpec((D, tf), lambda i, j: (0, j)),
                   pl.BlockSpec((tf, D), lambda i, j: (j, 0))],
        out_shape=[jax.ShapeDtypeStruct((M, D), f32), jax.ShapeDtypeStruct((D, F), bf16),
                   jax.ShapeDtypeStruct((D, F), bf16), jax.ShapeDtypeStruct((F, D), bf16)],
        scratch_shapes=[pltpu.VMEM((tm, D), bf16)],
        compiler_params=_cparams(("arbitrary", "arbitrary")),
        name="ffn_ln_head",
    )(x, wg, wu, wd, g.reshape(1, D), b.reshape(1, D))


def _ffn_ln_both(xp, xs, w32, d, g, b):
    out, *wb = _ffn_ln_head(xp, *w32, d, g, b, tm=PROMPT_ROW_TILE)
    yp = _ffn_ln(xp, *wb, g, b, tm=PROMPT_ROW_TILE, row0=1, prev=out)
    ys = _ffn_ln(xs, *wb, g, b, tm=ROW_TILE)
    return yp, ys


def _inproj_kernel(x_ref, wt_ref, o_ref, w_scr):
    @pl.when(pl.program_id(1) == 0)
    def _():
        w_scr[...] = wt_ref[...].T

    o_ref[...] = _dot(x_ref[...].astype(bf16), w_scr[...])


def _inproj(x, wt, d, *, tm=ROW_TILE, tn=2432):
    M, D = x.shape
    N = wt.shape[1]
    return pl.pallas_call(
        _inproj_kernel,
        grid=(N // tn, M // tm),
        in_specs=[pl.BlockSpec((tm, D), lambda n, i: (i, 0)),
                  pl.BlockSpec((None, tn, D), lambda n, i: (d, n, 0))],
        out_specs=pl.BlockSpec((tm, tn), lambda n, i: (i, n)),
        out_shape=jax.ShapeDtypeStruct((M, N), f32),
        scratch_shapes=[pltpu.VMEM((D, tn), bf16)],
        compiler_params=_cparams(("arbitrary", "arbitrary")),
        name="in_proj",
    )(x, wt)


def _outproj_ln_kernel(og_ref, hm_ref, ol_ref, x_ref, w_ref, g_ref, b_ref, o_ref):
    mix = (_dot(og_ref[...], w_ref[0:QK, :]) + _dot(hm_ref[...], w_ref[QK:2 * QK, :])
           + _dot(ol_ref[...], w_ref[2 * QK:, :]))
    o_ref[...] = _layernorm(ALPHA * x_ref[...] + mix, g_ref[...], b_ref[...])


def _outproj_ln(mix, x, w, d, g, b, *, tm=ROW_TILE):
    M, D = x.shape
    return pl.pallas_call(
        _outproj_ln_kernel,
        grid=(M // tm,),
        in_specs=[pl.BlockSpec((tm, wd), lambda i: (i, 0)) for wd in (QK, QK, LRU_W)]
        + [pl.BlockSpec((tm, D), lambda i: (i, 0)),
           pl.BlockSpec((None, D, D), lambda i: (d, 0, 0)),
           pl.BlockSpec((1, D), lambda i: (0, 0)),
           pl.BlockSpec((1, D), lambda i: (0, 0))],
        out_specs=pl.BlockSpec((tm, D), lambda i: (i, 0)),
        out_shape=jax.ShapeDtypeStruct((M, D), f32),
        compiler_params=_cparams(("parallel",)),
        name="out_proj_ln",
    )(*mix, x, w, g.reshape(1, D), b.reshape(1, D))


def _gate_block(g_ref, m_ref):
    lane = lax.broadcasted_iota(jnp.int32, g_ref.shape, 1)
    return jnp.where(lane < 2 * HEADS, g_ref[...], jnp.where(lane < 4 * HEADS, m_ref[...], 0.0))


def _gdn_gates(gates, gc_ref):
    beta = jax.nn.sigmoid(gates)
    gl = -jnp.exp(gc_ref[0:1, :]) * _softplus(gates + gc_ref[1:2, :])
    return beta, gl


def _mlstm_gates(gates, gc_ref):
    i_pre = gates + gc_ref[2:3, :]
    logf = -_softplus(-(gates + gc_ref[3:4, :]))
    return i_pre, logf


def _gdn_chunk(q3, k3, v3, b3, g3, S):
    C = q3.shape[1]
    ii, jj = _ij(C)
    g_row = _cumsum_row(g3, ii, jj)
    g_col = _row2col(g_row, ii, jj)
    decay = jnp.exp(jnp.where((ii >= jj)[None], g_col - g_row, NEG))
    kb = k3.astype(bf16)
    kk = _bmm_nt(kb, kb)
    p = jnp.where((ii > jj)[None], -(b3 * kk * decay), 0.0)
    e = p
    for _ in range(int(math.log2(C)) - 1):
        pb = p.astype(bf16)
        p = _bmm(pb, pb)
        e = e + p + _bmm(e.astype(bf16), p.astype(bf16))
    eb = e.astype(bf16)
    e_g = jnp.exp(g_col)
    rv = b3 * v3
    rk = (b3 * e_g) * k3
    uv = rv + _bmm(eb, rv.astype(bf16))
    wks = (rk + _bmm(eb, rk.astype(bf16))).astype(bf16)
    qk = (_bmm_nt(q3.astype(bf16), kb) * decay).astype(bf16)
    g_last = g_row[:, :, C - 1:C]
    qd = (q3 * e_g).astype(bf16)
    kt = (k3 * jnp.exp(g_last - g_col)).astype(bf16)
    Sb = S.astype(bf16)
    Ub = (uv - _bmm(wks, Sb)).astype(bf16)
    o = _bmm(qd, Sb) + _bmm(qk, Ub)
    return o, jnp.exp(g_last) * S + _bmm_tn(kt, Ub)


def _mlstm_chunk(q3, k3, vb, i3, f3, Cm, nr, m):
    C = q3.shape[1]
    ii, jj = _ij(C)
    b_row = _cumsum_row(f3, ii, jj)
    b_col = _row2col(b_row, ii, jj)
    i_row = _col2row(i3, ii, jj)
    D = jnp.where((ii >= jj)[None], b_col - b_row + i_row, NEG)
    d_max = jnp.max(D, axis=2, keepdims=True)
    qb = q3.astype(bf16)
    qk = _bmm_nt(qb, k3.astype(bf16))
    m_t = jnp.maximum(b_col + m, d_max)
    inter = jnp.exp(b_col + m - m_t)
    Sw = jnp.exp(D - m_t) * qk
    num = inter * _bmm(qb, Cm.astype(bf16)) + _bmm(Sw.astype(bf16), vb)
    den = inter * jnp.sum(q3 * nr, -1, keepdims=True) + jnp.sum(Sw, -1, keepdims=True)
    hh = num / jnp.maximum(jnp.abs(den), jnp.exp(-m_t))
    m_new = m_t[:, C - 1:C]
    b_last = b_col[:, C - 1:C]
    kw = jnp.exp(b_last - b_col + i3 - m_new) * k3
    dec = jnp.exp(b_last + m - m_new)
    return (hh, dec * Cm + _bmm_tn(kw.astype(bf16), vb), dec * nr + jnp.sum(kw, axis=1, keepdims=True), m_new)


def _gdn_prompt_kernel(*refs, C, B):
    mains, ggs, gms = refs[0:3 * B:3], refs[1:3 * B:3], refs[2:3 * B:3]
    cw_ref, gc_ref, nw_ref, o_ref, s_ref, xp_ref, tail_ref = refs[3 * B:]
    n = pl.program_id(0)
    NC = B * HEADS

    @pl.when(n == 0)
    def _():
        tail_ref[...] = jnp.zeros_like(tail_ref)
        s_ref[...] = jnp.zeros_like(s_ref)

    qs, ks, vs, zs, bs, gs = [], [], [], [], [], []
    for b in range(B):
        x = mains[b][:, 0:CONV_CH]
        y = _silu(_conv_carry(x, cw_ref, xp_ref, tail_ref[b]))
        tail_ref[b] = x[C - 8:C]
        beta, gl = _gdn_gates(_gate_block(ggs[b], gms[b]), gc_ref)
        for h in range(HEADS):
            qs.append(y[:, h * DK:(h + 1) * DK])
            ks.append(y[:, QK + h * DK:QK + (h + 1) * DK])
            vs.append(y[:, 2 * QK + h * DK:2 * QK + (h + 1) * DK])
            zs.append(mains[b][:, CONV_CH + h * DK:CONV_CH + (h + 1) * DK])
            bs.append(beta[:, h:h + 1])
            gs.append(gl[:, HEADS + h:HEADS + h + 1])
    q3 = _l2(jnp.stack(qs)) * (DK ** -0.5)
    k3 = _l2(jnp.stack(ks))
    v3 = jnp.stack(vs)
    b3 = jnp.stack(bs)
    g3 = jnp.stack(gs)
    o, S = _gdn_chunk(q3, k3, v3, b3, g3, s_ref[...].reshape(NC, DK, DK))
    s_ref[...] = S.reshape(B, HEADS, DK, DK)
    o = _rms(o) * nw_ref[...] * _silu(jnp.stack(zs))
    for b in range(B):
        for h in range(HEADS):
            o_ref[b, :, h * DK:(h + 1) * DK] = o[b * HEADS + h].astype(o_ref.dtype)


def _gdn_prompt(U, B, T, conv_w, hc, norm_w, *, C=CHUNK):
    N = T // C
    in_specs = []
    for b in range(B):
        in_specs.append(pl.BlockSpec((C, MIX_W), lambda n, b=b: (b * N + n, 0)))
        in_specs.append(pl.BlockSpec((C, DK), lambda n, b=b: (b * N + n, CB_GATES_G)))
        in_specs.append(pl.BlockSpec((C, DK), lambda n, b=b: (b * N + n, CB_GATES_M)))
    in_specs += [pl.BlockSpec((CONV_W, CONV_CH), lambda n: (0, 0)),
                 pl.BlockSpec((8, DK), lambda n: (0, 0)),
                 pl.BlockSpec((1, DK), lambda n: (0, 0))]
    o, S = pl.pallas_call(
        functools.partial(_gdn_prompt_kernel, C=C, B=B),
        grid=(N,),
        in_specs=in_specs,
        out_specs=[pl.BlockSpec((B, None, C, QK), lambda n: (0, n, 0, 0)),
                   pl.BlockSpec((B, HEADS, DK, DK), lambda n: (0, 0, 0, 0))],
        out_shape=[jax.ShapeDtypeStruct((B, N, C, QK), bf16),
                   jax.ShapeDtypeStruct((B, HEADS, DK, DK), f32)],
        scratch_shapes=[pltpu.VMEM((C + 8, CONV_CH), f32), pltpu.VMEM((B, 8, CONV_CH), f32)],
        compiler_params=_cparams(("arbitrary",)),
        name="gdn_prompt",
    )(*([U] * (3 * B)), conv_w, hc, norm_w.reshape(1, DK))
    return o.reshape(B * T, QK), S


def _mlstm_prompt_kernel(*refs, C, B):
    mains, ggs, gms = refs[0:3 * B:3], refs[1:3 * B:3], refs[2:3 * B:3]
    gc_ref, nw_ref, h_ref, c_ref, n_ref, m_ref = refs[3 * B:]
    n = pl.program_id(0)
    NC = B * HEADS

    @pl.when(n == 0)
    def _():
        c_ref[...] = jnp.zeros_like(c_ref)
        n_ref[...] = jnp.zeros_like(n_ref)
        m_ref[...] = jnp.zeros_like(m_ref)

    qs, ks, vs, os_, is_, fs = [], [], [], [], [], []
    for b in range(B):
        i_pre, logf = _mlstm_gates(_gate_block(ggs[b], gms[b]), gc_ref)
        grp = jnp.concatenate([mains[b][...], gms[b][...]], axis=1)[:, SHIFT_M:SHIFT_M + MIX_W]
        for h in range(HEADS):
            qs.append(grp[:, h * DK:(h + 1) * DK])
            ks.append(grp[:, QK + h * DK:QK + (h + 1) * DK])
            vs.append(grp[:, 2 * QK + h * DK:2 * QK + (h + 1) * DK])
            os_.append(grp[:, 3 * QK + h * DK:3 * QK + (h + 1) * DK])
            is_.append(i_pre[:, 2 * HEADS + h:2 * HEADS + h + 1])
            fs.append(logf[:, 3 * HEADS + h:3 * HEADS + h + 1])
    q3 = jnp.stack(qs)
    k3 = jnp.stack(ks) * (DK ** -0.5)
    vb = jnp.stack(vs).astype(bf16)
    i3 = jnp.stack(is_)
    f3 = jnp.stack(fs)
    hh, Cm, nr, m = _mlstm_chunk(q3, k3, vb, i3, f3, c_ref[...].reshape(NC, DK, DK),
                                 n_ref[...].reshape(NC, 1, DK), m_ref[...].reshape(NC, 1, DK)[:, :, 0:1])
    nw = jnp.stack([nw_ref[h] for _ in range(B) for h in range(HEADS)])
    hh = _rms(hh) * nw * jax.nn.sigmoid(jnp.stack(os_))
    c_ref[...] = Cm.reshape(B, HEADS, DK, DK)
    n_ref[...] = nr.reshape(B, HEADS, 1, DK)
    m_ref[...] = jnp.broadcast_to(m, (NC, 1, DK)).reshape(B, HEADS, 1, DK)
    for b in range(B):
        for h in range(HEADS):
            h_ref[b, :, h * DK:(h + 1) * DK] = hh[b * HEADS + h].astype(h_ref.dtype)


def _mlstm_prompt(U, B, T, hc, norm_w, *, C=CHUNK):
    N = T // C
    in_specs = []
    for b in range(B):
        in_specs.append(pl.BlockSpec((C, MIX_W), lambda n, b=b: (b * N + n, 1)))
        in_specs.append(pl.BlockSpec((C, DK), lambda n, b=b: (b * N + n, CB_GATES_G)))
        in_specs.append(pl.BlockSpec((C, DK), lambda n, b=b: (b * N + n, CB_GATES_M)))
    in_specs += [pl.BlockSpec((8, DK), lambda n: (0, 0)),
                 pl.BlockSpec((HEADS, 1, DK), lambda n: (0, 0, 0))]
    st = lambda r: pl.BlockSpec((B, HEADS, r, DK), lambda n: (0, 0, 0, 0))
    h, Cm, nr, m = pl.pallas_call(
        functools.partial(_mlstm_prompt_kernel, C=C, B=B),
        grid=(N,),
        in_specs=in_specs,
        out_specs=[pl.BlockSpec((B, None, C, QK), lambda n: (0, n, 0, 0)), st(DK), st(1), st(1)],
        out_shape=[jax.ShapeDtypeStruct((B, N, C, QK), bf16),
                   jax.ShapeDtypeStruct((B, HEADS, DK, DK), f32),
                   jax.ShapeDtypeStruct((B, HEADS, 1, DK), f32),
                   jax.ShapeDtypeStruct((B, HEADS, 1, DK), f32)],
        compiler_params=_cparams(("arbitrary",)),
        name="mlstm_prompt",
    )(*([U] * (3 * B)), hc, norm_w.reshape(HEADS, 1, DK))
    return h.reshape(B * T, QK), Cm, nr, m


def _lru_xy(a_ref, b_ref, t_ref):
    cat = jnp.concatenate([a_ref[...], b_ref[...], t_ref[...]], axis=1)
    return cat[:, SHIFT_L:SHIFT_L + LRU_W], cat[:, SHIFT_L + LRU_W:SHIFT_L + 2 * LRU_W]


def _lru_gates(c, wr_ref, br_ref, wi_ref, bi_ref, lam_ref):
    cb = c.astype(bf16)
    r = jax.nn.sigmoid(_dot(cb, wr_ref[...]) + br_ref[...])
    ig = jax.nn.sigmoid(_dot(cb, wi_ref[...]) + bi_ref[...])
    log_a = -LRU_C * r * _softplus(-lam_ref[...])
    a = jnp.exp(log_a)
    u = jnp.sqrt(-_expm1(2.0 * log_a)) * (ig * c)
    return a, u


def _lru_prompt_kernel(ua_ref, ub_ref, ut_ref, cw_ref, cb_ref, wr_ref, br_ref, wi_ref, bi_ref, lam_ref,
                       o_ref, hl_ref, xp_ref, tail_ref, hc_ref):
    tc = pl.program_id(1)
    Tc, W = ua_ref.shape

    @pl.when(tc == 0)
    def _():
        tail_ref[...] = jnp.zeros_like(tail_ref)
        hc_ref[...] = jnp.zeros_like(hc_ref)

    x, y = _lru_xy(ua_ref, ub_ref, ut_ref)
    c = _conv_carry(x, cw_ref, xp_ref, tail_ref[...]) + cb_ref[...]
    tail_ref[...] = x[Tc - 8:Tc]
    a, u = _lru_gates(c, wr_ref, br_ref, wi_ref, bi_ref, lam_ref)
    row = lax.broadcasted_iota(jnp.int32, (Tc, W), 0)
    d = 1
    while d < Tc:
        keep = row >= d
        a_sh = jnp.where(keep, pltpu.roll(a, d, axis=0), 1.0)
        u_sh = jnp.where(keep, pltpu.roll(u, d, axis=0), 0.0)
        u = a * u_sh + u
        a = a * a_sh
        d *= 2
    hs = u + a * hc_ref[0:1, :]
    last = hs[Tc - 1:Tc]
    hc_ref[...] = jnp.broadcast_to(last, hc_ref.shape)
    hl_ref[...] = last
    o_ref[...] = (hs * jax.nn.gelu(y)).astype(o_ref.dtype)


def _lru_prompt(U, B, T, cw, cb, wr, br, wi, bi, lam, *, Tc=512):
    W = LRU_W
    nt = T // Tc
    vec = lambda: pl.BlockSpec((1, W), lambda b, t: (0, 0))
    return pl.pallas_call(
        _lru_prompt_kernel,
        grid=(B, nt),
        in_specs=[pl.BlockSpec((Tc, W), lambda b, t: (b * nt + t, CB_L)),
                  pl.BlockSpec((Tc, W), lambda b, t: (b * nt + t, CB_L + 1)),
                  pl.BlockSpec((Tc, DK), lambda b, t: (b * nt + t, CB_L_TAIL)),
                  pl.BlockSpec((CONV_W, W), lambda b, t: (0, 0)), vec(),
                  pl.BlockSpec((W, W), lambda b, t: (0, 0)), vec(),
                  pl.BlockSpec((W, W), lambda b, t: (0, 0)), vec(), vec()],
        out_specs=[pl.BlockSpec((Tc, W), lambda b, t: (b * nt + t, 0)),
                   pl.BlockSpec((None, 1, W), lambda b, t: (b, 0, 0))],
        out_shape=[jax.ShapeDtypeStruct((B * T, W), bf16),
                   jax.ShapeDtypeStruct((B, 1, W), f32)],
        scratch_shapes=[pltpu.VMEM((Tc + 8, W), f32), pltpu.VMEM((8, W), f32), pltpu.VMEM((8, W), f32)],
        compiler_params=_cparams(("parallel", "arbitrary")),
        name="lru_prompt",
    )(U, U, U, cw, cb.reshape(1, W), wr, br.reshape(1, W), wi, bi.reshape(1, W), lam.reshape(1, W))


def _gdn_sample_kernel(q_ref, k_ref, v_ref, z_ref, gg_ref, gm_ref, wq_ref, wk_ref, wv_ref, gc_ref, nw_ref, s0_ref,
                       *rest):
    o_ref, s_ref = rest[-2:]
    h = pl.program_id(0)
    BB = s0_ref.shape[0]
    R = BB * GROUP
    q = _l2(_silu(_conv_groups(q_ref[...], wq_ref))) * (DK ** -0.5)
    k = _l2(_silu(_conv_groups(k_ref[...], wk_ref)))
    v = _silu(_conv_groups(v_ref[...], wv_ref))
    beta, gl = _gdn_gates(_gate_block(gg_ref, gm_ref), gc_ref)
    is_tok = _group_pos((R, 1)) >= TOK0
    beta = jnp.where(is_tok, _colsel(beta, h), 0.0)
    gl = jnp.where(is_tok, _colsel(gl, HEADS + h), 0.0)
    grp = lambda a: a.reshape(BB, GROUP, a.shape[-1])
    o, S = _gdn_chunk(grp(q), grp(k), grp(v), grp(beta), grp(gl), s0_ref[...])
    s_ref[...] = S
    o_ref[...] = (_rms(o.reshape(R, DK)) * nw_ref[...] * _silu(z_ref[...])).astype(o_ref.dtype)


def _stacked_out(prev):
    if prev is None:
        return [], [], {}
    return [pl.BlockSpec(memory_space=pl.ANY)], [prev], None


def _gdn_sample(U8, S_all, d, S_prev, conv_w, hc, norm_w, *, BB=SAMPLE_SEQS_PER_STEP):
    B = S_all.shape[1]
    R = BB * GROUP
    tok = lambda cb: pl.BlockSpec((R, DK), lambda h, b: (b, cb + h))
    cw = lambda cb: pl.BlockSpec((CONV_W, DK), lambda h, b: (0, cb + h))
    st = pl.BlockSpec((None, BB, None, DK, DK), lambda h, b: (d, b, h, 0, 0))
    extra_specs, extra_ops, _ = _stacked_out(S_prev)
    n_in = 12
    return pl.pallas_call(
        _gdn_sample_kernel,
        grid=(HEADS, B // BB),
        in_specs=[tok(CB_Q), tok(CB_K), tok(CB_V), tok(CB_Z),
                  pl.BlockSpec((R, DK), lambda h, b: (b, CB_GATES_G)),
                  pl.BlockSpec((R, DK), lambda h, b: (b, CB_GATES_M)),
                  cw(CB_Q), cw(CB_K), cw(CB_V),
                  pl.BlockSpec((8, DK), lambda h, b: (0, 0)),
                  pl.BlockSpec((1, DK), lambda h, b: (0, 0)), st] + extra_specs,
        out_specs=[pl.BlockSpec((R, DK), lambda h, b: (b, h)), st],
        out_shape=[jax.ShapeDtypeStruct((B * GROUP, QK), bf16),
                   jax.ShapeDtypeStruct(S_all.shape, f32)],
        input_output_aliases={n_in: 1} if S_prev is not None else {},
        compiler_params=_cparams(("parallel", "parallel")),
        name="gdn_sample",
    )(U8, U8, U8, U8, U8, U8, conv_w, conv_w, conv_w, hc, norm_w.reshape(1, DK), S_all, *extra_ops)


def _mlstm_sample_kernel(q0, q1, k0, k1, v0, v1, o0, o1, gg_ref, gm_ref, gc_ref, nw_ref, c0_ref, n0_ref, m0_ref,
                         *rest):
    h_ref, c_ref, n_ref, m_ref = rest[-4:]
    h = pl.program_id(0)
    BB = c0_ref.shape[0]
    R = BB * GROUP
    head = lambda a, b: jnp.concatenate([a[...], b[...]], axis=1)[:, SHIFT_M:SHIFT_M + DK]
    q, k, v, og = head(q0, q1), head(k0, k1), head(v0, v1), head(o0, o1)
    i_pre, logf = _mlstm_gates(_gate_block(gg_ref, gm_ref), gc_ref)
    is_tok = _group_pos((R, 1)) >= TOK0
    i_pre = jnp.where(is_tok, _colsel(i_pre, 2 * HEADS + h), NEG)
    logf = jnp.where(is_tok, _colsel(logf, 3 * HEADS + h), 0.0)
    grp = lambda a: a.reshape(BB, GROUP, a.shape[-1])
    old = lambda ref: grp(ref[...])[:, TOK0 - 1:TOK0, :]
    hh, Cm, nr, m = _mlstm_chunk(grp(q), grp(k * (DK ** -0.5)), grp(v).astype(bf16),
                                 grp(i_pre), grp(logf), c0_ref[...], old(n0_ref), old(m0_ref)[:, :, 0:1])
    c_ref[...] = Cm
    n_ref[...] = jnp.broadcast_to(nr, (BB, GROUP, DK)).reshape(R, DK)
    m_ref[...] = jnp.broadcast_to(m, (BB, GROUP, DK)).reshape(R, DK)
    h_ref[...] = (_rms(hh.reshape(R, DK)) * nw_ref[h] * jax.nn.sigmoid(og)).astype(h_ref.dtype)


def _mlstm_sample(U8, C_all, d, C_prev, n0g, m0g, hc, norm_w, *, BB=SAMPLE_SEQS_PER_STEP):
    B = C_all.shape[1]
    R = BB * GROUP
    blk = lambda cb: pl.BlockSpec((R, DK), lambda h, b: (b, cb + h))
    pair = lambda k: [blk(CB_GATES_G + HEADS * k), blk(CB_GATES_G + HEADS * k + 1)]
    row = pl.BlockSpec((R, DK), lambda h, b: (b, h))
    st = pl.BlockSpec((None, BB, None, DK, DK), lambda h, b: (d, b, h, 0, 0))
    extra_specs, extra_ops, _ = _stacked_out(C_prev)
    n_in = 15
    return pl.pallas_call(
        _mlstm_sample_kernel,
        grid=(HEADS, B // BB),
        in_specs=pair(0) + pair(1) + pair(2) + pair(3) + [
                  pl.BlockSpec((R, DK), lambda h, b: (b, CB_GATES_G)),
                  pl.BlockSpec((R, DK), lambda h, b: (b, CB_GATES_M)),
                  pl.BlockSpec((8, DK), lambda h, b: (0, 0)),
                  pl.BlockSpec((HEADS, 1, DK), lambda h, b: (0, 0, 0)),
                  st, row, row] + extra_specs,
        out_specs=[row, st, row, row],
        out_shape=[jax.ShapeDtypeStruct((B * GROUP, QK), bf16),
                   jax.ShapeDtypeStruct(C_all.shape, f32),
                   jax.ShapeDtypeStruct((B * GROUP, QK), f32),
                   jax.ShapeDtypeStruct((B * GROUP, QK), f32)],
        input_output_aliases={n_in: 1} if C_prev is not None else {},
        compiler_params=_cparams(("parallel", "parallel")),
        name="mlstm_sample",
    )(*([U8] * 10), hc, norm_w.reshape(HEADS, 1, DK), C_all, n0g, m0g, *extra_ops)


def _lru_sample_kernel(ua_ref, ub_ref, ut_ref, h0_ref, cw_ref, cb_ref, wr_ref, br_ref, wi_ref, bi_ref, lam_ref,
                       o_ref, h_ref):
    x, y = _lru_xy(ua_ref, ub_ref, ut_ref)
    c = _conv_groups(x, cw_ref) + cb_ref[...]
    a, u = _lru_gates(c, wr_ref, br_ref, wi_ref, bi_ref, lam_ref)
    hs = h0_ref[...]
    pos = _group_pos(hs.shape)
    for t in range(GROUP - TOK0):
        hs = jnp.where(pos == TOK0 + t, a * pltpu.roll(hs, 1, axis=0) + u, hs)
    h_ref[...] = hs
    o_ref[...] = (hs * jax.nn.gelu(y)).astype(o_ref.dtype)


def _lru_sample(U8, h0g, cw, cb, wr, br, wi, bi, lam, *, R=128):
    W = LRU_W
    rows = U8.shape[0]
    vec = lambda: pl.BlockSpec((1, W), lambda i: (0, 0))
    return pl.pallas_call(
        _lru_sample_kernel,
        grid=(rows // R,),
        in_specs=[pl.BlockSpec((R, W), lambda i: (i, CB_L)),
                  pl.BlockSpec((R, W), lambda i: (i, CB_L + 1)),
                  pl.BlockSpec((R, DK), lambda i: (i, CB_L_TAIL)),
                  pl.BlockSpec((R, W), lambda i: (i, 0)),
                  pl.BlockSpec((CONV_W, W), lambda i: (0, 0)), vec(),
                  pl.BlockSpec((W, W), lambda i: (0, 0)), vec(),
                  pl.BlockSpec((W, W), lambda i: (0, 0)), vec(), vec()],
        out_specs=[pl.BlockSpec((R, W), lambda i: (i, 0)), pl.BlockSpec((R, W), lambda i: (i, 0))],
        out_shape=[jax.ShapeDtypeStruct((rows, W), bf16), jax.ShapeDtypeStruct((rows, W), f32)],
        compiler_params=_cparams(("parallel",)),
        name="lru_sample",
    )(U8, U8, U8, h0g, cw, cb.reshape(1, W), wr, br.reshape(1, W), wi, bi.reshape(1, W), lam.reshape(1, W))


def _transpose_w_in(w_in):
    wt = jnp.swapaxes(w_in, 1, 2).astype(bf16)
    return jnp.pad(wt, ((0, 0), (0, D_INP - D_IN), (0, 0)))


def _block_diag(w):
    nb, d, e = w.shape
    eye = jnp.eye(nb, dtype=w.dtype)
    return (w[:, :, None, :] * eye[:, None, :, None]).reshape(nb * d, nb * e)


def _gate_consts(a_log, dt_bias, i_bias, f_bias):
    row = lambda v, k: jnp.pad(v, (k * HEADS, DK - (k + 1) * HEADS))
    rows = [row(a_log, 1), row(dt_bias, 1), row(i_bias, 2), row(f_bias, 3)]
    return jnp.stack(rows + [jnp.zeros((DK,), f32)] * (8 - len(rows)))


def _state_row(state):
    B, W = state.shape
    return jnp.pad(state[:, None, :], ((0, 0), (TOK0 - 1, GROUP - TOK0), (0, 0))).reshape(B * GROUP, W)


def _tail_rows(U, B, T, col, width):
    return jnp.stack([lax.slice(U, ((b + 1) * T - (CONV_W - 1), col), ((b + 1) * T, col + width))
                      for b in range(B)])


def kernel(x_prompt, x_sample, state_gdn_S, state_gdn_conv, state_mlstm_C, state_mlstm_n, state_mlstm_m, state_lru_h, state_lru_conv, ffn1_wg, ffn1_wu, ffn1_wd, ln_g, ln_b, w_in, gdn_conv_w, gdn_A_log, gdn_dt_bias, gdn_norm_w, mlstm_i_bias, mlstm_f_bias, mlstm_norm_w, lru_conv_w, lru_conv_b, lru_wr, lru_br, lru_wi, lru_bi, lru_lambda, w_out, ffn2_wg, ffn2_wu, ffn2_wd):
    BP, TP, D = x_prompt.shape
    BS, TS, _ = x_sample.shape
    MP, MS = BP * TP, BS * TS
    ffn1 = (ffn1_wg, ffn1_wu, ffn1_wd)
    ffn2 = (ffn2_wg, ffn2_wu, ffn2_wd)
    w_in_t = _transpose_w_in(w_in)
    w_out_b = w_out.astype(bf16)

    xp, xs = x_prompt.reshape(MP, D), x_sample.reshape(MS, D)
    outs_p = [[] for _ in range(7)]
    outs_s = [[] for _ in range(7)]
    gS_s = mC_s = None
    for d in range(DEPTH):
        x1p, x1s = _ffn_ln_both(xp, xs, ffn1, d, ln_g[d, 0], ln_b[d, 0])
        U = _inproj(x1p, w_in_t, d)
        hc = _gate_consts(gdn_A_log[d], gdn_dt_bias[d], mlstm_i_bias[d], mlstm_f_bias[d])
        wr = _block_diag(lru_wr[d]).astype(bf16)
        wi = _block_diag(lru_wi[d]).astype(bf16)
        lru_w = (lru_conv_w[d], lru_conv_b[d], wr, lru_br[d], wi, lru_bi[d], lru_lambda[d])

        og_p, gS_p = _gdn_prompt(U, BP, TP, gdn_conv_w[d], hc, gdn_norm_w[d])
        hm_p, mC_p, mn_p, mm_p = _mlstm_prompt(U, BP, TP, hc, mlstm_norm_w[d])
        ol_p, lh_p = _lru_prompt(U, BP, TP, *lru_w)
        outs_p[0].append(gS_p)
        outs_p[1].append(_tail_rows(U, BP, TP, 0, CONV_CH))
        outs_p[2].append(mC_p)
        outs_p[3].append(mn_p.reshape(BP, HEADS, DK))
        outs_p[4].append(mm_p[:, :, 0, 0])
        outs_p[5].append(lh_p.reshape(BP, LRU_W))
        outs_p[6].append(_tail_rows(U, BP, TP, COL_LX, LRU_W))

        Us = _inproj(x1s, w_in_t, d).reshape(BS, TS, D_INP)
        hist = jnp.concatenate([
            state_gdn_conv[d], jnp.zeros((BS, 3, COL_LX - CONV_CH), f32),
            state_lru_conv[d], jnp.zeros((BS, 3, D_INP - COL_LX - LRU_W), f32)], axis=2)
        U8 = jnp.concatenate([jnp.zeros((BS, 1, D_INP), f32), hist, Us], axis=1).reshape(BS * GROUP, D_INP)
        og_s, gS_s = _gdn_sample(U8, state_gdn_S, d, gS_s, gdn_conv_w[d], hc, gdn_norm_w[d])
        hm_s, mC_s, n8, m8 = _mlstm_sample(
            U8, state_mlstm_C, d, mC_s, _state_row(state_mlstm_n[d].reshape(BS, QK)),
            _state_row(jnp.repeat(state_mlstm_m[d], DK, axis=1)), hc, mlstm_norm_w[d])
        ol_s, h8 = _lru_sample(U8, _state_row(state_lru_h[d]), *lru_w)
        toks = lambda a: a.reshape(BS, GROUP, -1)[:, TOK0:].reshape(MS, -1)
        last = lambda a: a.reshape(BS, GROUP, -1)[:, GROUP - 1]
        outs_s[1].append(Us[:, TS - 3:, 0:CONV_CH])
        outs_s[3].append(last(n8).reshape(BS, HEADS, DK))
        outs_s[4].append(last(m8).reshape(BS, HEADS, DK)[:, :, 0])
        outs_s[5].append(last(h8))
        outs_s[6].append(Us[:, TS - 3:, COL_LX:COL_LX + LRU_W])

        x2p = _outproj_ln((og_p, hm_p, ol_p), x1p, w_out_b, d, ln_g[d, 1], ln_b[d, 1])
        x2s = _outproj_ln((toks(og_s), toks(hm_s), toks(ol_s)), x1s, w_out_b, d, ln_g[d, 1], ln_b[d, 1])
        xp, xs = _ffn_ln_both(x2p, x2s, ffn2, d, ln_g[d, 2], ln_b[d, 2])

    y_prompt = xp.reshape(BP, TP, D)
    y_sample = xs.reshape(BS, TS, D)
    stack = lambda o: jnp.stack(o)
    return (y_prompt, y_sample,
            *[stack(o) for o in outs_p],
            gS_s, stack(outs_s[1]), mC_s, *[stack(outs_s[i]) for i in range(3, 7)])
```

```python
import functools
import math

import jax
import jax.numpy as jnp
from jax import lax
from jax.experimental import pallas as pl
from jax.experimental.pallas import tpu as pltpu

f32 = jnp.float32
bf16 = jnp.bfloat16

DEPTH = 2
D_MODEL = 2048
D_FF = 5632
HEADS = 6
DK = 128
LRU_W = 512
LRU_BLOCKS = 8
LRU_C = 8.0
CONV_W = 4
CHUNK = 64
ALPHA = (2 * DEPTH) ** 0.25
NORM_EPS = 1e-6
NEG = -1e30

QK = HEADS * DK
CONV_CH = 3 * QK
MIX_W = 4 * QK
CB_Q, CB_K, CB_V, CB_Z = 0, 6, 12, 18
CB_MQ, CB_MK, CB_MV, CB_MO = 24, 30, 36, 42
CB_GDN, CB_MLSTM = 0, 1
CB_LX, CB_LY = 12, 13
CB_GATES = 56
COL_LX = 6144
D_INP = 7296
GROUP = 8
TOK0 = 4
ROW_TILE = 512
SAMPLE_SEQS_PER_STEP = 64

VMEM_LIMIT = 60 * 1024 * 1024


def _cparams(sem):
    return pltpu.CompilerParams(dimension_semantics=sem, vmem_limit_bytes=VMEM_LIMIT)


def _silu(x):
    return x * jax.nn.sigmoid(x)


def _softplus(x):
    return jnp.maximum(x, 0.0) + jnp.log1p(jnp.exp(-jnp.abs(x)))


def _expm1(x):
    u = jnp.exp(x)
    um1 = u - 1.0
    lg = jnp.where(u == 1.0, 1.0, jnp.log(jnp.where(u == 0.0, 1.0, u)))
    return jnp.where(u == 1.0, x, jnp.where(u == 0.0, -1.0, um1 * x / lg))


def _layernorm(y, g, b):
    mu = jnp.mean(y, -1, keepdims=True)
    d = y - mu
    var = jnp.mean(d * d, -1, keepdims=True)
    return d * lax.rsqrt(var + NORM_EPS) * g + b


def _rms(x):
    return x * lax.rsqrt(jnp.mean(x * x, -1, keepdims=True) + NORM_EPS)


def _l2(x):
    return x * lax.rsqrt(jnp.sum(x * x, -1, keepdims=True) + NORM_EPS)


def _colsel(gates, idx):
    lane = lax.broadcasted_iota(jnp.int32, gates.shape, 1)
    return jnp.sum(jnp.where(lane == idx, gates, 0.0), axis=1, keepdims=True)


def _dot(a, b):
    return jnp.dot(a, b, preferred_element_type=f32)


def _bmm(a, b):
    return jnp.einsum('nij,njk->nik', a, b, preferred_element_type=f32)


def _bmm_nt(a, b):
    return jnp.einsum('nid,njd->nij', a, b, preferred_element_type=f32)


def _bmm_tn(a, b):
    return jnp.einsum('nck,ncv->nkv', a, b, preferred_element_type=f32)


def _ij(C):
    return (lax.broadcasted_iota(jnp.int32, (C, C), 0), lax.broadcasted_iota(jnp.int32, (C, C), 1))


def _cumsum_row(col3, ii, jj):
    return jnp.sum(jnp.where((ii <= jj)[None], col3, 0.0), axis=1, keepdims=True)


def _row2col(row3, ii, jj):
    return jnp.sum(jnp.where((ii == jj)[None], row3, 0.0), axis=2, keepdims=True)


def _col2row(col3, ii, jj):
    return jnp.sum(jnp.where((ii == jj)[None], col3, 0.0), axis=1, keepdims=True)


def _conv_carry(x, w_ref, xp_ref, tail):
    T = x.shape[0]
    xp_ref[0:8, :] = tail
    xp_ref[8:T + 8, :] = x
    y = w_ref[3:4, :] * x
    for k in range(CONV_W - 1):
        y = y + w_ref[k:k + 1, :] * xp_ref[pl.ds(5 + k, T), :]
    return y


def _conv_groups(x, w_ref):
    y = w_ref[3:4, :] * x
    for s in range(1, CONV_W):
        y = y + w_ref[3 - s:4 - s, :] * pltpu.roll(x, s, axis=0)
    return y


def _group_pos(shape):
    return lax.broadcasted_iota(jnp.int32, shape, 0) & (GROUP - 1)


def _ffn_ln_kernel(*refs, cast_weights, aliased):
    x_ref, wg_ref, wu_ref, wd_ref, g_ref, b_ref = refs[:6]
    n_in = 7 if aliased else 6
    o_ref = refs[n_in]
    xb_ref = refs[-1]
    j = pl.program_id(1)

    @pl.when(j == 0)
    def _():
        o_ref[...] = jnp.zeros_like(o_ref)
        xb_ref[...] = x_ref[...].astype(bf16)

    if cast_weights:
        wg, wu, wd = (r[...].astype(bf16) for r in (wg_ref, wu_ref, wd_ref))
        for w_out_ref, w in zip(refs[n_in + 1:n_in + 4], (wg, wu, wd)):
            w_out_ref[...] = w
    else:
        wg, wu, wd = wg_ref[...], wu_ref[...], wd_ref[...]
    xb = xb_ref[...]
    h = (_silu(_dot(xb, wg)) * _dot(xb, wu)).astype(bf16)
    o_ref[...] += _dot(h, wd)

    @pl.when(j == pl.num_programs(1) - 1)
    def _():
        y = ALPHA * x_ref[...] + 0.5 * o_ref[...]
        o_ref[...] = _layernorm(y, g_ref[...], b_ref[...])


def _ffn_ln(x, wg, wu, wd, g, b, *, tm, tf=512, row0=0, prev=None):
    M, D = x.shape
    F = wg.shape[1]
    rows = lambda i, j: (i + row0, 0)
    in_specs = [
        pl.BlockSpec((tm, D), rows),
        pl.BlockSpec((D, tf), lambda i, j: (0, j)),
        pl.BlockSpec((D, tf), lambda i, j: (0, j)),
        pl.BlockSpec((tf, D), lambda i, j: (j, 0)),
        pl.BlockSpec((1, D), lambda i, j: (0, 0)),
        pl.BlockSpec((1, D), lambda i, j: (0, 0)),
    ]
    operands = [x, wg, wu, wd, g.reshape(1, D), b.reshape(1, D)]
    if prev is not None:
        in_specs.append(pl.BlockSpec(memory_space=pl.ANY))
        operands.append(prev)
    return pl.pallas_call(
        functools.partial(_ffn_ln_kernel, cast_weights=False, aliased=prev is not None),
        grid=(M // tm - row0, F // tf),
        in_specs=in_specs,
        out_specs=pl.BlockSpec((tm, D), rows),
        out_shape=jax.ShapeDtypeStruct((M, D), f32),
        input_output_aliases={6: 0} if prev is not None else {},
        scratch_shapes=[pltpu.VMEM((tm, D), bf16)],
        compiler_params=_cparams(("parallel", "arbitrary")),
        name="ffn_ln",
    )(*operands)


def _ffn_ln_head(x, wg, wu, wd, d, g, b, *, tm, tf=256):
    M, D = x.shape
    F = wg.shape[2]
    return pl.pallas_call(
        functools.partial(_ffn_ln_kernel, cast_weights=True, aliased=False),
        grid=(1, F // tf),
        in_specs=[
            pl.BlockSpec((tm, D), lambda i, j: (0, 0)),
            pl.BlockSpec((None, D, tf), lambda i, j: (d, 0, j)),
            pl.BlockSpec((None, D, tf), lambda i, j: (d, 0, j)),
            pl.BlockSpec((None, tf, D), lambda i, j: (d, j, 0)),
            pl.BlockSpec((1, D), lambda i, j: (0, 0)),
            pl.BlockSpec((1, D), lambda i, j: (0, 0)),
        ],
        out_specs=[pl.BlockSpec((tm, D), lambda i, j: (0, 0)),
                   pl.BlockSpec((D, tf), lambda i, j: (0, j)),
                   pl.BlockSpec((D, tf), lambda i, j: (0, j)),
                   pl.BlockSpec((tf, D), lambda i, j: (j, 0))],
        out_shape=[jax.ShapeDtypeStruct((M, D), f32), jax.ShapeDtypeStruct((D, F), bf16),
                   jax.ShapeDtypeStruct((D, F), bf16), jax.ShapeDtypeStruct((F, D), bf16)],
        scratch_shapes=[pltpu.VMEM((tm, D), bf16)],
        compiler_params=_cparams(("arbitrary", "arbitrary")),
        name="ffn_ln_head",
    )(x, wg, wu, wd, g.reshape(1, D), b.reshape(1, D))


def _ffn_ln_both(xp, xs, w32, d, g, b):
    out, *wb = _ffn_ln_head(xp, *w32, d, g, b, tm=ROW_TILE)
    yp = _ffn_ln(xp, *wb, g, b, tm=ROW_TILE, row0=1, prev=out)
    ys = _ffn_ln(xs, *wb, g, b, tm=ROW_TILE)
    return yp, ys


def _inproj_kernel(x_ref, w_ref, o_ref):
    o_ref[...] = _dot(x_ref[...].astype(bf16), w_ref[...])


def _inproj(x, w, d, *, tm=ROW_TILE, tn=2432):
    M, D = x.shape
    N = w.shape[2]
    return pl.pallas_call(
        _inproj_kernel,
        grid=(N // tn, M // tm),
        in_specs=[pl.BlockSpec((tm, D), lambda n, i: (i, 0)),
                  pl.BlockSpec((None, D, tn), lambda n, i: (d, 0, n))],
        out_specs=pl.BlockSpec((tm, tn), lambda n, i: (i, n)),
        out_shape=jax.ShapeDtypeStruct((M, N), f32),
        compiler_params=_cparams(("parallel", "arbitrary")),
        name="in_proj",
    )(x, w)


def _outproj_ln_kernel(og_ref, hm_ref, ol_ref, x_ref, w_ref, g_ref, b_ref, o_ref):
    mix = (_dot(og_ref[...], w_ref[0:QK, :]) + _dot(hm_ref[...], w_ref[QK:2 * QK, :])
           + _dot(ol_ref[...], w_ref[2 * QK:, :]))
    o_ref[...] = _layernorm(ALPHA * x_ref[...] + mix, g_ref[...], b_ref[...])


def _outproj_ln(mix, x, w, d, g, b, *, tm=ROW_TILE):
    M, D = x.shape
    return pl.pallas_call(
        _outproj_ln_kernel,
        grid=(M // tm,),
        in_specs=[pl.BlockSpec((tm, wd), lambda i: (i, 0)) for wd in (QK, QK, LRU_W)]
        + [pl.BlockSpec((tm, D), lambda i: (i, 0)),
           pl.BlockSpec((None, D, D), lambda i: (d, 0, 0)),
           pl.BlockSpec((1, D), lambda i: (0, 0)),
           pl.BlockSpec((1, D), lambda i: (0, 0))],
        out_specs=pl.BlockSpec((tm, D), lambda i: (i, 0)),
        out_shape=jax.ShapeDtypeStruct((M, D), f32),
        compiler_params=_cparams(("parallel",)),
        name="out_proj_ln",
    )(*mix, x, w, g.reshape(1, D), b.reshape(1, D))


def _gdn_gates(gates, gc_ref):
    beta = jax.nn.sigmoid(gates)
    gl = -jnp.exp(gc_ref[0:1, :]) * _softplus(gates + gc_ref[1:2, :])
    return beta, gl


def _mlstm_gates(gates, gc_ref):
    i_pre = gates + gc_ref[2:3, :]
    logf = -_softplus(-(gates + gc_ref[3:4, :]))
    return i_pre, logf


def _gdn_chunk(q3, k3, v3, b3, g3, S):
    C = q3.shape[1]
    ii, jj = _ij(C)
    g_row = _cumsum_row(g3, ii, jj)
    g_col = _row2col(g_row, ii, jj)
    decay = jnp.exp(jnp.where((ii >= jj)[None], g_col - g_row, NEG))
    kb = k3.astype(bf16)
    kk = _bmm_nt(kb, kb)
    p = jnp.where((ii > jj)[None], -(b3 * kk * decay), 0.0)
    e = p
    for _ in range(int(math.log2(C)) - 1):
        pb = p.astype(bf16)
        p = _bmm(pb, pb)
        e = e + p + _bmm(e.astype(bf16), p.astype(bf16))
    eb = e.astype(bf16)
    e_g = jnp.exp(g_col)
    rv = b3 * v3
    rk = (b3 * e_g) * k3
    uv = rv + _bmm(eb, rv.astype(bf16))
    wks = (rk + _bmm(eb, rk.astype(bf16))).astype(bf16)
    qk = (_bmm_nt(q3.astype(bf16), kb) * decay).astype(bf16)
    g_last = g_row[:, :, C - 1:C]
    qd = (q3 * e_g).astype(bf16)
    kt = (k3 * jnp.exp(g_last - g_col)).astype(bf16)
    Sb = S.astype(bf16)
    Ub = (uv - _bmm(wks, Sb)).astype(bf16)
    o = _bmm(qd, Sb) + _bmm(qk, Ub)
    return o, jnp.exp(g_last) * S + _bmm_tn(kt, Ub)


def _mlstm_chunk(q3, k3, vb, i3, f3, Cm, nr, m):
    C = q3.shape[1]
    ii, jj = _ij(C)
    b_row = _cumsum_row(f3, ii, jj)
    b_col = _row2col(b_row, ii, jj)
    i_row = _col2row(i3, ii, jj)
    D = jnp.where((ii >= jj)[None], b_col - b_row + i_row, NEG)
    d_max = jnp.max(D, axis=2, keepdims=True)
    qb = q3.astype(bf16)
    qk = _bmm_nt(qb, k3.astype(bf16))
    m_t = jnp.maximum(b_col + m, d_max)
    inter = jnp.exp(b_col + m - m_t)
    Sw = jnp.exp(D - m_t) * qk
    num = inter * _bmm(qb, Cm.astype(bf16)) + _bmm(Sw.astype(bf16), vb)
    den = inter * jnp.sum(q3 * nr, -1, keepdims=True) + jnp.sum(Sw, -1, keepdims=True)
    hh = num / jnp.maximum(jnp.abs(den), jnp.exp(-m_t))
    m_new = m_t[:, C - 1:C]
    b_last = b_col[:, C - 1:C]
    kw = jnp.exp(b_last - b_col + i3 - m_new) * k3
    dec = jnp.exp(b_last + m - m_new)
    return (hh, dec * Cm + _bmm_tn(kw.astype(bf16), vb), dec * nr + jnp.sum(kw, axis=1, keepdims=True), m_new)


def _gdn_prompt_kernel(*refs, C, B):
    mains, gts = refs[0:2 * B:2], refs[1:2 * B:2]
    cw_ref, gc_ref, nw_ref, o_ref, s_ref, xp_ref, tail_ref = refs[2 * B:]
    n = pl.program_id(0)
    NC = B * HEADS

    @pl.when(n == 0)
    def _():
        tail_ref[...] = jnp.zeros_like(tail_ref)
        s_ref[...] = jnp.zeros_like(s_ref)

    qs, ks, vs, zs, bs, gs = [], [], [], [], [], []
    for b in range(B):
        x = mains[b][:, 0:CONV_CH]
        y = _silu(_conv_carry(x, cw_ref, xp_ref, tail_ref[b]))
        tail_ref[b] = x[C - 8:C]
        beta, gl = _gdn_gates(gts[b][...], gc_ref)
        for h in range(HEADS):
            qs.append(y[:, h * DK:(h + 1) * DK])
            ks.append(y[:, QK + h * DK:QK + (h + 1) * DK])
            vs.append(y[:, 2 * QK + h * DK:2 * QK + (h + 1) * DK])
            zs.append(mains[b][:, CONV_CH + h * DK:CONV_CH + (h + 1) * DK])
            bs.append(beta[:, h:h + 1])
            gs.append(gl[:, HEADS + h:HEADS + h + 1])
    q3 = _l2(jnp.stack(qs)) * (DK ** -0.5)
    k3 = _l2(jnp.stack(ks))
    v3 = jnp.stack(vs)
    b3 = jnp.stack(bs)
    g3 = jnp.stack(gs)
    o, S = _gdn_chunk(q3, k3, v3, b3, g3, s_ref[...].reshape(NC, DK, DK))
    s_ref[...] = S.reshape(B, HEADS, DK, DK)
    o = _rms(o) * nw_ref[...] * _silu(jnp.stack(zs))
    for b in range(B):
        for h in range(HEADS):
            o_ref[b, :, h * DK:(h + 1) * DK] = o[b * HEADS + h].astype(o_ref.dtype)


def _gdn_prompt(U, B, T, conv_w, hc, norm_w, *, C=CHUNK):
    N = T // C
    in_specs = []
    for b in range(B):
        in_specs.append(pl.BlockSpec((C, MIX_W), lambda n, b=b: (b * N + n, CB_GDN)))
        in_specs.append(pl.BlockSpec((C, DK), lambda n, b=b: (b * N + n, CB_GATES)))
    in_specs += [pl.BlockSpec((CONV_W, CONV_CH), lambda n: (0, 0)),
                 pl.BlockSpec((8, DK), lambda n: (0, 0)),
                 pl.BlockSpec((1, DK), lambda n: (0, 0))]
    o, S = pl.pallas_call(
        functools.partial(_gdn_prompt_kernel, C=C, B=B),
        grid=(N,),
        in_specs=in_specs,
        out_specs=[pl.BlockSpec((B, None, C, QK), lambda n: (0, n, 0, 0)),
                   pl.BlockSpec((B, HEADS, DK, DK), lambda n: (0, 0, 0, 0))],
        out_shape=[jax.ShapeDtypeStruct((B, N, C, QK), bf16),
                   jax.ShapeDtypeStruct((B, HEADS, DK, DK), f32)],
        scratch_shapes=[pltpu.VMEM((C + 8, CONV_CH), f32), pltpu.VMEM((B, 8, CONV_CH), f32)],
        compiler_params=_cparams(("arbitrary",)),
        name="gdn_prompt",
    )(*([U, U] * B), conv_w, hc, norm_w.reshape(1, DK))
    return o.reshape(B * T, QK), S


def _mlstm_prompt_kernel(*refs, C, B):
    mains, gts = refs[0:2 * B:2], refs[1:2 * B:2]
    gc_ref, nw_ref, h_ref, c_ref, n_ref, m_ref = refs[2 * B:]
    n = pl.program_id(0)
    NC = B * HEADS

    @pl.when(n == 0)
    def _():
        c_ref[...] = jnp.zeros_like(c_ref)
        n_ref[...] = jnp.zeros_like(n_ref)
        m_ref[...] = jnp.zeros_like(m_ref)

    qs, ks, vs, os_, is_, fs = [], [], [], [], [], []
    for b in range(B):
        i_pre, logf = _mlstm_gates(gts[b][...], gc_ref)
        for h in range(HEADS):
            qs.append(mains[b][:, h * DK:(h + 1) * DK])
            ks.append(mains[b][:, QK + h * DK:QK + (h + 1) * DK])
            vs.append(mains[b][:, 2 * QK + h * DK:2 * QK + (h + 1) * DK])
            os_.append(mains[b][:, 3 * QK + h * DK:3 * QK + (h + 1) * DK])
            is_.append(i_pre[:, 2 * HEADS + h:2 * HEADS + h + 1])
            fs.append(logf[:, 3 * HEADS + h:3 * HEADS + h + 1])
    q3 = jnp.stack(qs)
    k3 = jnp.stack(ks) * (DK ** -0.5)
    vb = jnp.stack(vs).astype(bf16)
    i3 = jnp.stack(is_)
    f3 = jnp.stack(fs)
    hh, Cm, nr, m = _mlstm_chunk(q3, k3, vb, i3, f3, c_ref[...].reshape(NC, DK, DK),
                                 n_ref[...].reshape(NC, 1, DK), m_ref[...].reshape(NC, 1, DK)[:, :, 0:1])
    nw = jnp.stack([nw_ref[h] for _ in range(B) for h in range(HEADS)])
    hh = _rms(hh) * nw * jax.nn.sigmoid(jnp.stack(os_))
    c_ref[...] = Cm.reshape(B, HEADS, DK, DK)
    n_ref[...] = nr.reshape(B, HEADS, 1, DK)
    m_ref[...] = jnp.broadcast_to(m, (NC, 1, DK)).reshape(B, HEADS, 1, DK)
    for b in range(B):
        for h in range(HEADS):
            h_ref[b, :, h * DK:(h + 1) * DK] = hh[b * HEADS + h].astype(h_ref.dtype)


def _mlstm_prompt(U, B, T, hc, norm_w, *, C=CHUNK):
    N = T // C
    in_specs = []
    for b in range(B):
        in_specs.append(pl.BlockSpec((C, MIX_W), lambda n, b=b: (b * N + n, CB_MLSTM)))
        in_specs.append(pl.BlockSpec((C, DK), lambda n, b=b: (b * N + n, CB_GATES)))
    in_specs += [pl.BlockSpec((8, DK), lambda n: (0, 0)),
                 pl.BlockSpec((HEADS, 1, DK), lambda n: (0, 0, 0))]
    st = lambda r: pl.BlockSpec((B, HEADS, r, DK), lambda n: (0, 0, 0, 0))
    h, Cm, nr, m = pl.pallas_call(
        functools.partial(_mlstm_prompt_kernel, C=C, B=B),
        grid=(N,),
        in_specs=in_specs,
        out_specs=[pl.BlockSpec((B, None, C, QK), lambda n: (0, n, 0, 0)), st(DK), st(1), st(1)],
        out_shape=[jax.ShapeDtypeStruct((B, N, C, QK), bf16),
                   jax.ShapeDtypeStruct((B, HEADS, DK, DK), f32),
                   jax.ShapeDtypeStruct((B, HEADS, 1, DK), f32),
                   jax.ShapeDtypeStruct((B, HEADS, 1, DK), f32)],
        compiler_params=_cparams(("arbitrary",)),
        name="mlstm_prompt",
    )(*([U, U] * B), hc, norm_w.reshape(HEADS, 1, DK))
    return h.reshape(B * T, QK), Cm, nr, m


def _lru_gates(c, wr_ref, br_ref, wi_ref, bi_ref, lam_ref):
    cb = c.astype(bf16)
    r = jax.nn.sigmoid(_dot(cb, wr_ref[...]) + br_ref[...])
    ig = jax.nn.sigmoid(_dot(cb, wi_ref[...]) + bi_ref[...])
    log_a = -LRU_C * r * _softplus(-lam_ref[...])
    a = jnp.exp(log_a)
    u = jnp.sqrt(-_expm1(2.0 * log_a)) * (ig * c)
    return a, u


def _lru_prompt_kernel(x_ref, y_ref, cw_ref, cb_ref, wr_ref, br_ref, wi_ref, bi_ref, lam_ref,
                       o_ref, hl_ref, xp_ref, tail_ref, hc_ref):
    tc = pl.program_id(1)
    Tc, W = x_ref.shape

    @pl.when(tc == 0)
    def _():
        tail_ref[...] = jnp.zeros_like(tail_ref)
        hc_ref[...] = jnp.zeros_like(hc_ref)

    x = x_ref[...]
    c = _conv_carry(x, cw_ref, xp_ref, tail_ref[...]) + cb_ref[...]
    tail_ref[...] = x[Tc - 8:Tc]
    a, u = _lru_gates(c, wr_ref, br_ref, wi_ref, bi_ref, lam_ref)
    row = lax.broadcasted_iota(jnp.int32, (Tc, W), 0)
    d = 1
    while d < Tc:
        keep = row >= d
        a_sh = jnp.where(keep, pltpu.roll(a, d, axis=0), 1.0)
        u_sh = jnp.where(keep, pltpu.roll(u, d, axis=0), 0.0)
        u = a * u_sh + u
        a = a * a_sh
        d *= 2
    hs = u + a * hc_ref[0:1, :]
    last = hs[Tc - 1:Tc]
    hc_ref[...] = jnp.broadcast_to(last, hc_ref.shape)
    hl_ref[...] = last
    o_ref[...] = (hs * jax.nn.gelu(y_ref[...])).astype(o_ref.dtype)


def _lru_prompt(U, B, T, cw, cb, wr, br, wi, bi, lam, *, Tc=512):
    W = LRU_W
    nt = T // Tc
    vec = lambda: pl.BlockSpec((1, W), lambda b, t: (0, 0))
    return pl.pallas_call(
        _lru_prompt_kernel,
        grid=(B, nt),
        in_specs=[pl.BlockSpec((Tc, W), lambda b, t: (b * nt + t, CB_LX)),
                  pl.BlockSpec((Tc, W), lambda b, t: (b * nt + t, CB_LY)),
                  pl.BlockSpec((CONV_W, W), lambda b, t: (0, 0)), vec(),
                  pl.BlockSpec((W, W), lambda b, t: (0, 0)), vec(),
                  pl.BlockSpec((W, W), lambda b, t: (0, 0)), vec(), vec()],
        out_specs=[pl.BlockSpec((Tc, W), lambda b, t: (b * nt + t, 0)),
                   pl.BlockSpec((None, 1, W), lambda b, t: (b, 0, 0))],
        out_shape=[jax.ShapeDtypeStruct((B * T, W), bf16),
                   jax.ShapeDtypeStruct((B, 1, W), f32)],
        scratch_shapes=[pltpu.VMEM((Tc + 8, W), f32), pltpu.VMEM((8, W), f32), pltpu.VMEM((8, W), f32)],
        compiler_params=_cparams(("parallel", "arbitrary")),
        name="lru_prompt",
    )(U, U, cw, cb.reshape(1, W), wr, br.reshape(1, W), wi, bi.reshape(1, W), lam.reshape(1, W))


def _gdn_sample_kernel(q_ref, k_ref, v_ref, z_ref, gt_ref, wq_ref, wk_ref, wv_ref, gc_ref, nw_ref, s0_ref,
                       *rest):
    o_ref, s_ref = rest[-2:]
    h = pl.program_id(0)
    BB = s0_ref.shape[0]
    R = BB * GROUP
    q = _l2(_silu(_conv_groups(q_ref[...], wq_ref))) * (DK ** -0.5)
    k = _l2(_silu(_conv_groups(k_ref[...], wk_ref)))
    v = _silu(_conv_groups(v_ref[...], wv_ref))
    beta, gl = _gdn_gates(gt_ref[...], gc_ref)
    is_tok = _group_pos((R, 1)) >= TOK0
    beta = jnp.where(is_tok, _colsel(beta, h), 0.0)
    gl = jnp.where(is_tok, _colsel(gl, HEADS + h), 0.0)
    grp = lambda a: a.reshape(BB, GROUP, a.shape[-1])
    o, S = _gdn_chunk(grp(q), grp(k), grp(v), grp(beta), grp(gl), s0_ref[...])
    s_ref[...] = S
    o_ref[...] = (_rms(o.reshape(R, DK)) * nw_ref[...] * _silu(z_ref[...])).astype(o_ref.dtype)


def _stacked_out(prev):
    if prev is None:
        return [], [], {}
    return [pl.BlockSpec(memory_space=pl.ANY)], [prev], None


def _gdn_sample(U8, S_all, d, S_prev, conv_w, hc, norm_w, *, BB=SAMPLE_SEQS_PER_STEP):
    B = S_all.shape[1]
    R = BB * GROUP
    tok = lambda cb: pl.BlockSpec((R, DK), lambda h, b: (b, cb + h))
    cw = lambda cb: pl.BlockSpec((CONV_W, DK), lambda h, b: (0, cb + h))
    st = pl.BlockSpec((None, BB, None, DK, DK), lambda h, b: (d, b, h, 0, 0))
    extra_specs, extra_ops, _ = _stacked_out(S_prev)
    n_in = 11
    return pl.pallas_call(
        _gdn_sample_kernel,
        grid=(HEADS, B // BB),
        in_specs=[tok(CB_Q), tok(CB_K), tok(CB_V), tok(CB_Z),
                  pl.BlockSpec((R, DK), lambda h, b: (b, CB_GATES)),
                  cw(CB_Q), cw(CB_K), cw(CB_V),
                  pl.BlockSpec((8, DK), lambda h, b: (0, 0)),
                  pl.BlockSpec((1, DK), lambda h, b: (0, 0)), st] + extra_specs,
        out_specs=[pl.BlockSpec((R, DK), lambda h, b: (b, h)), st],
        out_shape=[jax.ShapeDtypeStruct((B * GROUP, QK), bf16),
                   jax.ShapeDtypeStruct(S_all.shape, f32)],
        input_output_aliases={n_in: 1} if S_prev is not None else {},
        compiler_params=_cparams(("parallel", "parallel")),
        name="gdn_sample",
    )(U8, U8, U8, U8, U8, conv_w, conv_w, conv_w, hc, norm_w.reshape(1, DK), S_all, *extra_ops)


def _mlstm_sample_kernel(q_ref, k_ref, v_ref, og_ref, gt_ref, gc_ref, nw_ref, c0_ref, n0_ref, m0_ref,
                         *rest):
    h_ref, c_ref, n_ref, m_ref = rest[-4:]
    h = pl.program_id(0)
    BB = c0_ref.shape[0]
    R = BB * GROUP
    i_pre, logf = _mlstm_gates(gt_ref[...], gc_ref)
    is_tok = _group_pos((R, 1)) >= TOK0
    i_pre = jnp.where(is_tok, _colsel(i_pre, 2 * HEADS + h), NEG)
    logf = jnp.where(is_tok, _colsel(logf, 3 * HEADS + h), 0.0)
    grp = lambda a: a.reshape(BB, GROUP, a.shape[-1])
    old = lambda ref: grp(ref[...])[:, TOK0 - 1:TOK0, :]
    hh, Cm, nr, m = _mlstm_chunk(grp(q_ref[...]), grp(k_ref[...] * (DK ** -0.5)), grp(v_ref[...]).astype(bf16),
                                 grp(i_pre), grp(logf), c0_ref[...], old(n0_ref), old(m0_ref)[:, :, 0:1])
    c_ref[...] = Cm
    n_ref[...] = jnp.broadcast_to(nr, (BB, GROUP, DK)).reshape(R, DK)
    m_ref[...] = jnp.broadcast_to(m, (BB, GROUP, DK)).reshape(R, DK)
    h_ref[...] = (_rms(hh.reshape(R, DK)) * nw_ref[h] * jax.nn.sigmoid(og_ref[...])).astype(h_ref.dtype)


def _mlstm_sample(U8, C_all, d, C_prev, n0g, m0g, hc, norm_w, *, BB=SAMPLE_SEQS_PER_STEP):
    B = C_all.shape[1]
    R = BB * GROUP
    tok = lambda cb: pl.BlockSpec((R, DK), lambda h, b: (b, cb + h))
    row = pl.BlockSpec((R, DK), lambda h, b: (b, h))
    st = pl.BlockSpec((None, BB, None, DK, DK), lambda h, b: (d, b, h, 0, 0))
    extra_specs, extra_ops, _ = _stacked_out(C_prev)
    n_in = 10
    return pl.pallas_call(
        _mlstm_sample_kernel,
        grid=(HEADS, B // BB),
        in_specs=[tok(CB_MQ), tok(CB_MK), tok(CB_MV), tok(CB_MO),
                  pl.BlockSpec((R, DK), lambda h, b: (b, CB_GATES)),
                  pl.BlockSpec((8, DK), lambda h, b: (0, 0)),
                  pl.BlockSpec((HEADS, 1, DK), lambda h, b: (0, 0, 0)),
                  st, row, row] + extra_specs,
        out_specs=[row, st, row, row],
        out_shape=[jax.ShapeDtypeStruct((B * GROUP, QK), bf16),
                   jax.ShapeDtypeStruct(C_all.shape, f32),
                   jax.ShapeDtypeStruct((B * GROUP, QK), f32),
                   jax.ShapeDtypeStruct((B * GROUP, QK), f32)],
        input_output_aliases={n_in: 1} if C_prev is not None else {},
        compiler_params=_cparams(("parallel", "parallel")),
        name="mlstm_sample",
    )(U8, U8, U8, U8, U8, hc, norm_w.reshape(HEADS, 1, DK), C_all, n0g, m0g, *extra_ops)


def _lru_sample_kernel(x_ref, y_ref, h0_ref, cw_ref, cb_ref, wr_ref, br_ref, wi_ref, bi_ref, lam_ref,
                       o_ref, h_ref):
    c = _conv_groups(x_ref[...], cw_ref) + cb_ref[...]
    a, u = _lru_gates(c, wr_ref, br_ref, wi_ref, bi_ref, lam_ref)
    hs = h0_ref[...]
    pos = _group_pos(hs.shape)
    for t in range(GROUP - TOK0):
        hs = jnp.where(pos == TOK0 + t, a * pltpu.roll(hs, 1, axis=0) + u, hs)
    h_ref[...] = hs
    o_ref[...] = (hs * jax.nn.gelu(y_ref[...])).astype(o_ref.dtype)


def _lru_sample(U8, h0g, cw, cb, wr, br, wi, bi, lam, *, R=128):
    W = LRU_W
    rows = U8.shape[0]
    vec = lambda: pl.BlockSpec((1, W), lambda i: (0, 0))
    return pl.pallas_call(
        _lru_sample_kernel,
        grid=(rows // R,),
        in_specs=[pl.BlockSpec((R, W), lambda i: (i, CB_LX)),
                  pl.BlockSpec((R, W), lambda i: (i, CB_LY)),
                  pl.BlockSpec((R, W), lambda i: (i, 0)),
                  pl.BlockSpec((CONV_W, W), lambda i: (0, 0)), vec(),
                  pl.BlockSpec((W, W), lambda i: (0, 0)), vec(),
                  pl.BlockSpec((W, W), lambda i: (0, 0)), vec(), vec()],
        out_specs=[pl.BlockSpec((R, W), lambda i: (i, 0)), pl.BlockSpec((R, W), lambda i: (i, 0))],
        out_shape=[jax.ShapeDtypeStruct((rows, W), bf16), jax.ShapeDtypeStruct((rows, W), f32)],
        compiler_params=_cparams(("parallel",)),
        name="lru_sample",
    )(U8, U8, h0g, cw, cb.reshape(1, W), wr, br.reshape(1, W), wi, bi.reshape(1, W), lam.reshape(1, W))


def _permute_w_in(w_in):
    s = {}
    off = 0
    for name, size in (("qkv", CONV_CH), ("z", QK), ("gb", HEADS), ("ga", HEADS), ("mq", QK), ("mk", QK),
                       ("mv", QK), ("mo", QK), ("mi", HEADS), ("mf", HEADS), ("lx", LRU_W), ("ly", LRU_W)):
        s[name] = w_in[..., off:off + size]
        off += size
    pad = jnp.zeros(w_in.shape[:-1] + (DK - 4 * HEADS,), w_in.dtype)
    return jnp.concatenate([s["qkv"], s["z"], s["mq"], s["mk"], s["mv"], s["mo"], s["lx"], s["ly"],
                            s["gb"], s["ga"], s["mi"], s["mf"], pad], axis=-1)


def _block_diag(w):
    nb, d, e = w.shape
    eye = jnp.eye(nb, dtype=w.dtype)
    return (w[:, :, None, :] * eye[:, None, :, None]).reshape(nb * d, nb * e)


def _gate_consts(a_log, dt_bias, i_bias, f_bias):
    row = lambda v, k: jnp.pad(v, (k * HEADS, DK - (k + 1) * HEADS))
    rows = [row(a_log, 1), row(dt_bias, 1), row(i_bias, 2), row(f_bias, 3)]
    return jnp.stack(rows + [jnp.zeros((DK,), f32)] * (8 - len(rows)))


def _state_row(state):
    B, W = state.shape
    return jnp.pad(state[:, None, :], ((0, 0), (TOK0 - 1, GROUP - TOK0), (0, 0))).reshape(B * GROUP, W)


def _tail_rows(U, B, T, col, width):
    return jnp.stack([lax.slice(U, ((b + 1) * T - (CONV_W - 1), col), ((b + 1) * T, col + width))
                      for b in range(B)])


def kernel(x_prompt, x_sample, state_gdn_S, state_gdn_conv, state_mlstm_C, state_mlstm_n, state_mlstm_m, state_lru_h, state_lru_conv, ffn1_wg, ffn1_wu, ffn1_wd, ln_g, ln_b, w_in, gdn_conv_w, gdn_A_log, gdn_dt_bias, gdn_norm_w, mlstm_i_bias, mlstm_f_bias, mlstm_norm_w, lru_conv_w, lru_conv_b, lru_wr, lru_br, lru_wi, lru_bi, lru_lambda, w_out, ffn2_wg, ffn2_wu, ffn2_wd):
    BP, TP, D = x_prompt.shape
    BS, TS, _ = x_sample.shape
    MP, MS = BP * TP, BS * TS
    ffn1 = (ffn1_wg, ffn1_wu, ffn1_wd)
    ffn2 = (ffn2_wg, ffn2_wu, ffn2_wd)
    w_in_p = _permute_w_in(w_in.astype(bf16))
    w_out_b = w_out.astype(bf16)

    xp, xs = x_prompt.reshape(MP, D), x_sample.reshape(MS, D)
    outs_p = [[] for _ in range(7)]
    outs_s = [[] for _ in range(7)]
    gS_s = mC_s = None
    for d in range(DEPTH):
        x1p, x1s = _ffn_ln_both(xp, xs, ffn1, d, ln_g[d, 0], ln_b[d, 0])
        U = _inproj(x1p, w_in_p, d)
        hc = _gate_consts(gdn_A_log[d], gdn_dt_bias[d], mlstm_i_bias[d], mlstm_f_bias[d])
        wr = _block_diag(lru_wr[d]).astype(bf16)
        wi = _block_diag(lru_wi[d]).astype(bf16)
        lru_w = (lru_conv_w[d], lru_conv_b[d], wr, lru_br[d], wi, lru_bi[d], lru_lambda[d])

        og_p, gS_p = _gdn_prompt(U, BP, TP, gdn_conv_w[d], hc, gdn_norm_w[d])
        hm_p, mC_p, mn_p, mm_p = _mlstm_prompt(U, BP, TP, hc, mlstm_norm_w[d])
        ol_p, lh_p = _lru_prompt(U, BP, TP, *lru_w)
        outs_p[0].append(gS_p)
        outs_p[1].append(_tail_rows(U, BP, TP, 0, CONV_CH))
        outs_p[2].append(mC_p)
        outs_p[3].append(mn_p.reshape(BP, HEADS, DK))
        outs_p[4].append(mm_p[:, :, 0, 0])
        outs_p[5].append(lh_p.reshape(BP, LRU_W))
        outs_p[6].append(_tail_rows(U, BP, TP, COL_LX, LRU_W))

        Us = _inproj(x1s, w_in_p, d).reshape(BS, TS, D_INP)
        hist = jnp.concatenate([
            state_gdn_conv[d], jnp.zeros((BS, 3, COL_LX - CONV_CH), f32),
            state_lru_conv[d], jnp.zeros((BS, 3, D_INP - COL_LX - LRU_W), f32)], axis=2)
        U8 = jnp.concatenate([jnp.zeros((BS, 1, D_INP), f32), hist, Us], axis=1).reshape(BS * GROUP, D_INP)
        og_s, gS_s = _gdn_sample(U8, state_gdn_S, d, gS_s, gdn_conv_w[d], hc, gdn_norm_w[d])
        hm_s, mC_s, n8, m8 = _mlstm_sample(
            U8, state_mlstm_C, d, mC_s, _state_row(state_mlstm_n[d].reshape(BS, QK)),
            _state_row(jnp.repeat(state_mlstm_m[d], DK, axis=1)), hc, mlstm_norm_w[d])
        ol_s, h8 = _lru_sample(U8, _state_row(state_lru_h[d]), *lru_w)
        toks = lambda a: a.reshape(BS, GROUP, -1)[:, TOK0:].reshape(MS, -1)
        last = lambda a: a.reshape(BS, GROUP, -1)[:, GROUP - 1]
        outs_s[1].append(Us[:, TS - 3:, 0:CONV_CH])
        outs_s[3].append(last(n8).reshape(BS, HEADS, DK))
        outs_s[4].append(last(m8).reshape(BS, HEADS, DK)[:, :, 0])
        outs_s[5].append(last(h8))
        outs_s[6].append(Us[:, TS - 3:, COL_LX:COL_LX + LRU_W])

        x2p = _outproj_ln((og_p, hm_p, ol_p), x1p, w_out_b, d, ln_g[d, 1], ln_b[d, 1])
        x2s = _outproj_ln((toks(og_s), toks(hm_s), toks(ol_s)), x1s, w_out_b, d, ln_g[d, 1], ln_b[d, 1])
        xp, xs = _ffn_ln_both(x2p, x2s, ffn2, d, ln_g[d, 2], ln_b[d, 2])

    y_prompt = xp.reshape(BP, TP, D)
    y_sample = xs.reshape(BS, TS, D)
    stack = lambda o: jnp.stack(o)
    return (y_prompt, y_sample,
            *[stack(o) for o in outs_p],
            gS_s, stack(outs_s[1]), mC_s, *[stack(outs_s[i]) for i in range(3, 7)])
```

```python
import functools
import math

import jax
import jax.numpy as jnp
from jax import lax
from jax.experimental import pallas as pl
from jax.experimental.pallas import tpu as pltpu

f32 = jnp.float32
bf16 = jnp.bfloat16

DEPTH = 2
D_MODEL = 2048
D_FF = 5632
HEADS = 6
DK = 128
LRU_W = 512
LRU_BLOCKS = 8
LRU_C = 8.0
CONV_W = 4
CHUNK = 64
ALPHA = (2 * DEPTH) ** 0.25
NORM_EPS = 1e-6
NEG = -1e30

QK = HEADS * DK
CONV_CH = 3 * QK
MIX_W = 4 * QK
CB_Q, CB_K, CB_V, CB_Z = 0, 6, 12, 18
CB_MQ, CB_MK, CB_MV, CB_MO = 24, 30, 36, 42
CB_GDN, CB_MLSTM = 0, 1
CB_LX, CB_LY = 12, 13
CB_GATES = 56
COL_LX = 6144
D_INP = 7296
GROUP = 8
TOK0 = 4
ROW_TILE = 512
HEAD_ROW_BLOCKS = 2
SAMPLE_SEQS_PER_STEP = 64

VMEM_LIMIT = 60 * 1024 * 1024


def _cparams(sem):
    return pltpu.CompilerParams(dimension_semantics=sem, vmem_limit_bytes=VMEM_LIMIT)


def _silu(x):
    return x * jax.nn.sigmoid(x)


def _softplus(x):
    return jnp.maximum(x, 0.0) + jnp.log1p(jnp.exp(-jnp.abs(x)))


def _expm1(x):
    u = jnp.exp(x)
    um1 = u - 1.0
    lg = jnp.where(u == 1.0, 1.0, jnp.log(jnp.where(u == 0.0, 1.0, u)))
    return jnp.where(u == 1.0, x, jnp.where(u == 0.0, -1.0, um1 * x / lg))


def _layernorm(y, g, b):
    mu = jnp.mean(y, -1, keepdims=True)
    d = y - mu
    var = jnp.mean(d * d, -1, keepdims=True)
    return d * lax.rsqrt(var + NORM_EPS) * g + b


def _rms(x):
    return x * lax.rsqrt(jnp.mean(x * x, -1, keepdims=True) + NORM_EPS)


def _l2(x):
    return x * lax.rsqrt(jnp.sum(x * x, -1, keepdims=True) + NORM_EPS)


def _colsel(gates, idx):
    lane = lax.broadcasted_iota(jnp.int32, gates.shape, 1)
    return jnp.sum(jnp.where(lane == idx, gates, 0.0), axis=1, keepdims=True)


def _dot(a, b):
    return jnp.dot(a, b, preferred_element_type=f32)


def _bmm(a, b):
    return jnp.einsum('nij,njk->nik', a, b, preferred_element_type=f32)


def _bmm_nt(a, b):
    return jnp.einsum('nid,njd->nij', a, b, preferred_element_type=f32)


def _bmm_tn(a, b):
    return jnp.einsum('nck,ncv->nkv', a, b, preferred_element_type=f32)


def _ij(C):
    return (lax.broadcasted_iota(jnp.int32, (C, C), 0), lax.broadcasted_iota(jnp.int32, (C, C), 1))


def _cumsum_row(col3, ii, jj):
    return jnp.sum(jnp.where((ii <= jj)[None], col3, 0.0), axis=1, keepdims=True)


def _row2col(row3, ii, jj):
    return jnp.sum(jnp.where((ii == jj)[None], row3, 0.0), axis=2, keepdims=True)


def _col2row(col3, ii, jj):
    return jnp.sum(jnp.where((ii == jj)[None], col3, 0.0), axis=1, keepdims=True)


def _conv_carry(x, w_ref, xp_ref, tail):
    T = x.shape[0]
    xp_ref[0:8, :] = tail
    xp_ref[8:T + 8, :] = x
    y = w_ref[3:4, :] * x
    for k in range(CONV_W - 1):
        y = y + w_ref[k:k + 1, :] * xp_ref[pl.ds(5 + k, T), :]
    return y


def _conv_groups(x, w_ref):
    y = w_ref[3:4, :] * x
    for s in range(1, CONV_W):
        y = y + w_ref[3 - s:4 - s, :] * pltpu.roll(x, s, axis=0)
    return y


def _group_pos(shape):
    return lax.broadcasted_iota(jnp.int32, shape, 0) & (GROUP - 1)


def _ffn_ln_kernel(*refs, cast_weights, aliased):
    x_ref, wg_ref, wu_ref, wd_ref, g_ref, b_ref = refs[:6]
    n_in = 7 if aliased else 6
    o_ref = refs[n_in]
    xb_ref = refs[-1]
    j = pl.program_id(1)

    @pl.when(j == 0)
    def _():
        o_ref[...] = jnp.zeros_like(o_ref)
        xb_ref[...] = x_ref[...].astype(bf16)

    if cast_weights:
        wg, wu, wd = (r[...].astype(bf16) for r in (wg_ref, wu_ref, wd_ref))
        for w_out_ref, w in zip(refs[n_in + 1:n_in + 4], (wg, wu, wd)):
            w_out_ref[...] = w
    else:
        wg, wu, wd = wg_ref[...], wu_ref[...], wd_ref[...]
    xb = xb_ref[...]
    h = (_silu(_dot(xb, wg)) * _dot(xb, wu)).astype(bf16)
    o_ref[...] += _dot(h, wd)

    @pl.when(j == pl.num_programs(1) - 1)
    def _():
        y = ALPHA * x_ref[...] + 0.5 * o_ref[...]
        o_ref[...] = _layernorm(y, g_ref[...], b_ref[...])


def _ffn_ln(x, wg, wu, wd, g, b, *, tm, tf=512, row0=0, prev=None):
    M, D = x.shape
    F = wg.shape[1]
    rows = lambda i, j: (i + row0, 0)
    in_specs = [
        pl.BlockSpec((tm, D), rows),
        pl.BlockSpec((D, tf), lambda i, j: (0, j)),
        pl.BlockSpec((D, tf), lambda i, j: (0, j)),
        pl.BlockSpec((tf, D), lambda i, j: (j, 0)),
        pl.BlockSpec((1, D), lambda i, j: (0, 0)),
        pl.BlockSpec((1, D), lambda i, j: (0, 0)),
    ]
    operands = [x, wg, wu, wd, g.reshape(1, D), b.reshape(1, D)]
    if prev is not None:
        in_specs.append(pl.BlockSpec(memory_space=pl.ANY))
        operands.append(prev)
    return pl.pallas_call(
        functools.partial(_ffn_ln_kernel, cast_weights=False, aliased=prev is not None),
        grid=(M // tm - row0, F // tf),
        in_specs=in_specs,
        out_specs=pl.BlockSpec((tm, D), rows),
        out_shape=jax.ShapeDtypeStruct((M, D), f32),
        input_output_aliases={6: 0} if prev is not None else {},
        scratch_shapes=[pltpu.VMEM((tm, D), bf16)],
        compiler_params=_cparams(("parallel", "arbitrary")),
        name="ffn_ln",
    )(*operands)


def _ffn_ln_head(x, wg, wu, wd, d, g, b, *, tm, tf=256):
    M, D = x.shape
    F = wg.shape[2]
    return pl.pallas_call(
        functools.partial(_ffn_ln_kernel, cast_weights=True, aliased=False),
        grid=(1, F // tf),
        in_specs=[
            pl.BlockSpec((tm, D), lambda i, j: (0, 0), pipeline_mode=pl.Buffered(1)),
            pl.BlockSpec((None, D, tf), lambda i, j: (d, 0, j)),
            pl.BlockSpec((None, D, tf), lambda i, j: (d, 0, j)),
            pl.BlockSpec((None, tf, D), lambda i, j: (d, j, 0)),
            pl.BlockSpec((1, D), lambda i, j: (0, 0)),
            pl.BlockSpec((1, D), lambda i, j: (0, 0)),
        ],
        out_specs=[pl.BlockSpec((tm, D), lambda i, j: (0, 0)),
                   pl.BlockSpec((D, tf), lambda i, j: (0, j)),
                   pl.BlockSpec((D, tf), lambda i, j: (0, j)),
                   pl.BlockSpec((tf, D), lambda i, j: (j, 0))],
        out_shape=[jax.ShapeDtypeStruct((M, D), f32), jax.ShapeDtypeStruct((D, F), bf16),
                   jax.ShapeDtypeStruct((D, F), bf16), jax.ShapeDtypeStruct((F, D), bf16)],
        scratch_shapes=[pltpu.VMEM((tm, D), bf16)],
        compiler_params=_cparams(("arbitrary", "arbitrary")),
        name="ffn_ln_head",
    )(x, wg, wu, wd, g.reshape(1, D), b.reshape(1, D))


def _ffn_ln_both(xp, xs, w32, d, g, b):
    out, *wb = _ffn_ln_head(xp, *w32, d, g, b, tm=HEAD_ROW_BLOCKS * ROW_TILE)
    yp = _ffn_ln(xp, *wb, g, b, tm=ROW_TILE, row0=HEAD_ROW_BLOCKS, prev=out)
    ys = _ffn_ln(xs, *wb, g, b, tm=ROW_TILE)
    return yp, ys


def _inproj_kernel(x_ref, w_ref, o_ref):
    o_ref[...] = _dot(x_ref[...].astype(bf16), w_ref[...])


def _inproj(x, w, d, *, tm=ROW_TILE, tn=2432):
    M, D = x.shape
    N = w.shape[2]
    return pl.pallas_call(
        _inproj_kernel,
        grid=(N // tn, M // tm),
        in_specs=[pl.BlockSpec((tm, D), lambda n, i: (i, 0)),
                  pl.BlockSpec((None, D, tn), lambda n, i: (d, 0, n))],
        out_specs=pl.BlockSpec((tm, tn), lambda n, i: (i, n)),
        out_shape=jax.ShapeDtypeStruct((M, N), f32),
        compiler_params=_cparams(("parallel", "arbitrary")),
        name="in_proj",
    )(x, w)


def _outproj_ln_kernel(og_ref, hm_ref, ol_ref, x_ref, w_ref, g_ref, b_ref, o_ref):
    mix = (_dot(og_ref[...], w_ref[0:QK, :]) + _dot(hm_ref[...], w_ref[QK:2 * QK, :])
           + _dot(ol_ref[...], w_ref[2 * QK:, :]))
    o_ref[...] = _layernorm(ALPHA * x_ref[...] + mix, g_ref[...], b_ref[...])


def _outproj_ln(mix, x, w, d, g, b, *, tm=ROW_TILE):
    M, D = x.shape
    return pl.pallas_call(
        _outproj_ln_kernel,
        grid=(M // tm,),
        in_specs=[pl.BlockSpec((tm, wd), lambda i: (i, 0)) for wd in (QK, QK, LRU_W)]
        + [pl.BlockSpec((tm, D), lambda i: (i, 0)),
           pl.BlockSpec((None, D, D), lambda i: (d, 0, 0)),
           pl.BlockSpec((1, D), lambda i: (0, 0)),
           pl.BlockSpec((1, D), lambda i: (0, 0))],
        out_specs=pl.BlockSpec((tm, D), lambda i: (i, 0)),
        out_shape=jax.ShapeDtypeStruct((M, D), f32),
        compiler_params=_cparams(("parallel",)),
        name="out_proj_ln",
    )(*mix, x, w, g.reshape(1, D), b.reshape(1, D))


def _gdn_gates(gates, gc_ref):
    beta = jax.nn.sigmoid(gates)
    gl = -jnp.exp(gc_ref[0:1, :]) * _softplus(gates + gc_ref[1:2, :])
    return beta, gl


def _mlstm_gates(gates, gc_ref):
    i_pre = gates + gc_ref[2:3, :]
    logf = -_softplus(-(gates + gc_ref[3:4, :]))
    return i_pre, logf


def _gdn_chunk(q3, k3, v3, b3, g3, S):
    C = q3.shape[1]
    ii, jj = _ij(C)
    g_row = _cumsum_row(g3, ii, jj)
    g_col = _row2col(g_row, ii, jj)
    decay = jnp.exp(jnp.where((ii >= jj)[None], g_col - g_row, NEG))
    kb = k3.astype(bf16)
    kk = _bmm_nt(kb, kb)
    p = jnp.where((ii > jj)[None], -(b3 * kk * decay), 0.0)
    e = p
    for _ in range(int(math.log2(C)) - 1):
        pb = p.astype(bf16)
        p = _bmm(pb, pb)
        e = e + p + _bmm(e.astype(bf16), p.astype(bf16))
    eb = e.astype(bf16)
    e_g = jnp.exp(g_col)
    rv = b3 * v3
    rk = (b3 * e_g) * k3
    uv = rv + _bmm(eb, rv.astype(bf16))
    wks = (rk + _bmm(eb, rk.astype(bf16))).astype(bf16)
    qk = (_bmm_nt(q3.astype(bf16), kb) * decay).astype(bf16)
    g_last = g_row[:, :, C - 1:C]
    qd = (q3 * e_g).astype(bf16)
    kt = (k3 * jnp.exp(g_last - g_col)).astype(bf16)
    Sb = S.astype(bf16)
    Ub = (uv - _bmm(wks, Sb)).astype(bf16)
    o = _bmm(qd, Sb) + _bmm(qk, Ub)
    return o, jnp.exp(g_last) * S + _bmm_tn(kt, Ub)


def _mlstm_chunk(q3, k3, vb, i3, f3, Cm, nr, m):
    C = q3.shape[1]
    ii, jj = _ij(C)
    b_row = _cumsum_row(f3, ii, jj)
    b_col = _row2col(b_row, ii, jj)
    i_row = _col2row(i3, ii, jj)
    D = jnp.where((ii >= jj)[None], b_col - b_row + i_row, NEG)
    d_max = jnp.max(D, axis=2, keepdims=True)
    qb = q3.astype(bf16)
    qk = _bmm_nt(qb, k3.astype(bf16))
    m_t = jnp.maximum(b_col + m, d_max)
    inter = jnp.exp(b_col + m - m_t)
    Sw = jnp.exp(D - m_t) * qk
    num = inter * _bmm(qb, Cm.astype(bf16)) + _bmm(Sw.astype(bf16), vb)
    den = inter * jnp.sum(q3 * nr, -1, keepdims=True) + jnp.sum(Sw, -1, keepdims=True)
    hh = num / jnp.maximum(jnp.abs(den), jnp.exp(-m_t))
    m_new = m_t[:, C - 1:C]
    b_last = b_col[:, C - 1:C]
    kw = jnp.exp(b_last - b_col + i3 - m_new) * k3
    dec = jnp.exp(b_last + m - m_new)
    return (hh, dec * Cm + _bmm_tn(kw.astype(bf16), vb), dec * nr + jnp.sum(kw, axis=1, keepdims=True), m_new)


def _gdn_prompt_kernel(*refs, C, B):
    mains, gts = refs[0:2 * B:2], refs[1:2 * B:2]
    cw_ref, gc_ref, nw_ref, o_ref, s_ref, xp_ref, tail_ref = refs[2 * B:]
    n = pl.program_id(0)
    NC = B * HEADS

    @pl.when(n == 0)
    def _():
        tail_ref[...] = jnp.zeros_like(tail_ref)
        s_ref[...] = jnp.zeros_like(s_ref)

    qs, ks, vs, zs, bs, gs = [], [], [], [], [], []
    for b in range(B):
        x = mains[b][:, 0:CONV_CH]
        y = _silu(_conv_carry(x, cw_ref, xp_ref, tail_ref[b]))
        tail_ref[b] = x[C - 8:C]
        beta, gl = _gdn_gates(gts[b][...], gc_ref)
        for h in range(HEADS):
            qs.append(y[:, h * DK:(h + 1) * DK])
            ks.append(y[:, QK + h * DK:QK + (h + 1) * DK])
            vs.append(y[:, 2 * QK + h * DK:2 * QK + (h + 1) * DK])
            zs.append(mains[b][:, CONV_CH + h * DK:CONV_CH + (h + 1) * DK])
            bs.append(beta[:, h:h + 1])
            gs.append(gl[:, HEADS + h:HEADS + h + 1])
    q3 = _l2(jnp.stack(qs)) * (DK ** -0.5)
    k3 = _l2(jnp.stack(ks))
    v3 = jnp.stack(vs)
    b3 = jnp.stack(bs)
    g3 = jnp.stack(gs)
    o, S = _gdn_chunk(q3, k3, v3, b3, g3, s_ref[...].reshape(NC, DK, DK))
    s_ref[...] = S.reshape(B, HEADS, DK, DK)
    o = _rms(o) * nw_ref[...] * _silu(jnp.stack(zs))
    for b in range(B):
        for h in range(HEADS):
            o_ref[b, :, h * DK:(h + 1) * DK] = o[b * HEADS + h].astype(o_ref.dtype)


def _gdn_prompt(U, B, T, conv_w, hc, norm_w, *, C=CHUNK):
    N = T // C
    in_specs = []
    for b in range(B):
        in_specs.append(pl.BlockSpec((C, MIX_W), lambda n, b=b: (b * N + n, CB_GDN)))
        in_specs.append(pl.BlockSpec((C, DK), lambda n, b=b: (b * N + n, CB_GATES)))
    in_specs += [pl.BlockSpec((CONV_W, CONV_CH), lambda n: (0, 0)),
                 pl.BlockSpec((8, DK), lambda n: (0, 0)),
                 pl.BlockSpec((1, DK), lambda n: (0, 0))]
    o, S = pl.pallas_call(
        functools.partial(_gdn_prompt_kernel, C=C, B=B),
        grid=(N,),
        in_specs=in_specs,
        out_specs=[pl.BlockSpec((B, None, C, QK), lambda n: (0, n, 0, 0)),
                   pl.BlockSpec((B, HEADS, DK, DK), lambda n: (0, 0, 0, 0))],
        out_shape=[jax.ShapeDtypeStruct((B, N, C, QK), bf16),
                   jax.ShapeDtypeStruct((B, HEADS, DK, DK), f32)],
        scratch_shapes=[pltpu.VMEM((C + 8, CONV_CH), f32), pltpu.VMEM((B, 8, CONV_CH), f32)],
        compiler_params=_cparams(("arbitrary",)),
        name="gdn_prompt",
    )(*([U, U] * B), conv_w, hc, norm_w.reshape(1, DK))
    return o.reshape(B * T, QK), S


def _mlstm_prompt_kernel(*refs, C, B):
    mains, gts = refs[0:2 * B:2], refs[1:2 * B:2]
    gc_ref, nw_ref, h_ref, c_ref, n_ref, m_ref = refs[2 * B:]
    n = pl.program_id(0)
    NC = B * HEADS

    @pl.when(n == 0)
    def _():
        c_ref[...] = jnp.zeros_like(c_ref)
        n_ref[...] = jnp.zeros_like(n_ref)
        m_ref[...] = jnp.zeros_like(m_ref)

    qs, ks, vs, os_, is_, fs = [], [], [], [], [], []
    for b in range(B):
        i_pre, logf = _mlstm_gates(gts[b][...], gc_ref)
        for h in range(HEADS):
            qs.append(mains[b][:, h * DK:(h + 1) * DK])
            ks.append(mains[b][:, QK + h * DK:QK + (h + 1) * DK])
            vs.append(mains[b][:, 2 * QK + h * DK:2 * QK + (h + 1) * DK])
            os_.append(mains[b][:, 3 * QK + h * DK:3 * QK + (h + 1) * DK])
            is_.append(i_pre[:, 2 * HEADS + h:2 * HEADS + h + 1])
            fs.append(logf[:, 3 * HEADS + h:3 * HEADS + h + 1])
    q3 = jnp.stack(qs)
    k3 = jnp.stack(ks) * (DK ** -0.5)
    vb = jnp.stack(vs).astype(bf16)
    i3 = jnp.stack(is_)
    f3 = jnp.stack(fs)
    hh, Cm, nr, m = _mlstm_chunk(q3, k3, vb, i3, f3, c_ref[...].reshape(NC, DK, DK),
                                 n_ref[...].reshape(NC, 1, DK), m_ref[...].reshape(NC, 1, DK)[:, :, 0:1])
    nw = jnp.stack([nw_ref[h] for _ in range(B) for h in range(HEADS)])
    hh = _rms(hh) * nw * jax.nn.sigmoid(jnp.stack(os_))
    c_ref[...] = Cm.reshape(B, HEADS, DK, DK)
    n_ref[...] = nr.reshape(B, HEADS, 1, DK)
    m_ref[...] = jnp.broadcast_to(m, (NC, 1, DK)).reshape(B, HEADS, 1, DK)
    for b in range(B):
        for h in range(HEADS):
            h_ref[b, :, h * DK:(h + 1) * DK] = hh[b * HEADS + h].astype(h_ref.dtype)


def _mlstm_prompt(U, B, T, hc, norm_w, *, C=CHUNK):
    N = T // C
    in_specs = []
    for b in range(B):
        in_specs.append(pl.BlockSpec((C, MIX_W), lambda n, b=b: (b * N + n, CB_MLSTM)))
        in_specs.append(pl.BlockSpec((C, DK), lambda n, b=b: (b * N + n, CB_GATES)))
    in_specs += [pl.BlockSpec((8, DK), lambda n: (0, 0)),
                 pl.BlockSpec((HEADS, 1, DK), lambda n: (0, 0, 0))]
    st = lambda r: pl.BlockSpec((B, HEADS, r, DK), lambda n: (0, 0, 0, 0))
    h, Cm, nr, m = pl.pallas_call(
        functools.partial(_mlstm_prompt_kernel, C=C, B=B),
        grid=(N,),
        in_specs=in_specs,
        out_specs=[pl.BlockSpec((B, None, C, QK), lambda n: (0, n, 0, 0)), st(DK), st(1), st(1)],
        out_shape=[jax.ShapeDtypeStruct((B, N, C, QK), bf16),
                   jax.ShapeDtypeStruct((B, HEADS, DK, DK), f32),
                   jax.ShapeDtypeStruct((B, HEADS, 1, DK), f32),
                   jax.ShapeDtypeStruct((B, HEADS, 1, DK), f32)],
        compiler_params=_cparams(("arbitrary",)),
        name="mlstm_prompt",
    )(*([U, U] * B), hc, norm_w.reshape(HEADS, 1, DK))
    return h.reshape(B * T, QK), Cm, nr, m


def _lru_gates(c, wr_ref, br_ref, wi_ref, bi_ref, lam_ref):
    cb = c.astype(bf16)
    r = jax.nn.sigmoid(_dot(cb, wr_ref[...]) + br_ref[...])
    ig = jax.nn.sigmoid(_dot(cb, wi_ref[...]) + bi_ref[...])
    log_a = -LRU_C * r * _softplus(-lam_ref[...])
    a = jnp.exp(log_a)
    u = jnp.sqrt(-_expm1(2.0 * log_a)) * (ig * c)
    return a, u


def _lru_prompt_kernel(x_ref, y_ref, cw_ref, cb_ref, wr_ref, br_ref, wi_ref, bi_ref, lam_ref,
                       o_ref, hl_ref, xp_ref, tail_ref, hc_ref):
    tc = pl.program_id(1)
    Tc, W = x_ref.shape

    @pl.when(tc == 0)
    def _():
        tail_ref[...] = jnp.zeros_like(tail_ref)
        hc_ref[...] = jnp.zeros_like(hc_ref)

    x = x_ref[...]
    c = _conv_carry(x, cw_ref, xp_ref, tail_ref[...]) + cb_ref[...]
    tail_ref[...] = x[Tc - 8:Tc]
    a, u = _lru_gates(c, wr_ref, br_ref, wi_ref, bi_ref, lam_ref)
    row = lax.broadcasted_iota(jnp.int32, (Tc, W), 0)
    d = 1
    while d < Tc:
        keep = row >= d
        a_sh = jnp.where(keep, pltpu.roll(a, d, axis=0), 1.0)
        u_sh = jnp.where(keep, pltpu.roll(u, d, axis=0), 0.0)
        u = a * u_sh + u
        a = a * a_sh
        d *= 2
    hs = u + a * hc_ref[0:1, :]
    last = hs[Tc - 1:Tc]
    hc_ref[...] = jnp.broadcast_to(last, hc_ref.shape)
    hl_ref[...] = last
    o_ref[...] = (hs * jax.nn.gelu(y_ref[...])).astype(o_ref.dtype)


def _lru_prompt(U, B, T, cw, cb, wr, br, wi, bi, lam, *, Tc=512):
    W = LRU_W
    nt = T // Tc
    vec = lambda: pl.BlockSpec((1, W), lambda b, t: (0, 0))
    return pl.pallas_call(
        _lru_prompt_kernel,
        grid=(B, nt),
        in_specs=[pl.BlockSpec((Tc, W), lambda b, t: (b * nt + t, CB_LX)),
                  pl.BlockSpec((Tc, W), lambda b, t: (b * nt + t, CB_LY)),
                  pl.BlockSpec((CONV_W, W), lambda b, t: (0, 0)), vec(),
                  pl.BlockSpec((W, W), lambda b, t: (0, 0)), vec(),
                  pl.BlockSpec((W, W), lambda b, t: (0, 0)), vec(), vec()],
        out_specs=[pl.BlockSpec((Tc, W), lambda b, t: (b * nt + t, 0)),
                   pl.BlockSpec((None, 1, W), lambda b, t: (b, 0, 0))],
        out_shape=[jax.ShapeDtypeStruct((B * T, W), bf16),
                   jax.ShapeDtypeStruct((B, 1, W), f32)],
        scratch_shapes=[pltpu.VMEM((Tc + 8, W), f32), pltpu.VMEM((8, W), f32), pltpu.VMEM((8, W), f32)],
        compiler_params=_cparams(("parallel", "arbitrary")),
        name="lru_prompt",
    )(U, U, cw, cb.reshape(1, W), wr, br.reshape(1, W), wi, bi.reshape(1, W), lam.reshape(1, W))


def _gdn_sample_kernel(q_ref, k_ref, v_ref, z_ref, gt_ref, wq_ref, wk_ref, wv_ref, gc_ref, nw_ref, s0_ref,
                       *rest):
    o_ref, s_ref = rest[-2:]
    h = pl.program_id(0)
    BB = s0_ref.shape[0]
    R = BB * GROUP
    q = _l2(_silu(_conv_groups(q_ref[...], wq_ref))) * (DK ** -0.5)
    k = _l2(_silu(_conv_groups(k_ref[...], wk_ref)))
    v = _silu(_conv_groups(v_ref[...], wv_ref))
    beta, gl = _gdn_gates(gt_ref[...], gc_ref)
    is_tok = _group_pos((R, 1)) >= TOK0
    beta = jnp.where(is_tok, _colsel(beta, h), 0.0)
    gl = jnp.where(is_tok, _colsel(gl, HEADS + h), 0.0)
    grp = lambda a: a.reshape(BB, GROUP, a.shape[-1])
    o, S = _gdn_chunk(grp(q), grp(k), grp(v), grp(beta), grp(gl), s0_ref[...])
    s_ref[...] = S
    o_ref[...] = (_rms(o.reshape(R, DK)) * nw_ref[...] * _silu(z_ref[...])).astype(o_ref.dtype)


def _stacked_out(prev):
    if prev is None:
        return [], [], {}
    return [pl.BlockSpec(memory_space=pl.ANY)], [prev], None


def _gdn_sample(U8, S_all, d, S_prev, conv_w, hc, norm_w, *, BB=SAMPLE_SEQS_PER_STEP):
    B = S_all.shape[1]
    R = BB * GROUP
    tok = lambda cb: pl.BlockSpec((R, DK), lambda h, b: (b, cb + h))
    cw = lambda cb: pl.BlockSpec((CONV_W, DK), lambda h, b: (0, cb + h))
    st = pl.BlockSpec((None, BB, None, DK, DK), lambda h, b: (d, b, h, 0, 0))
    extra_specs, extra_ops, _ = _stacked_out(S_prev)
    n_in = 11
    return pl.pallas_call(
        _gdn_sample_kernel,
        grid=(HEADS, B // BB),
        in_specs=[tok(CB_Q), tok(CB_K), tok(CB_V), tok(CB_Z),
                  pl.BlockSpec((R, DK), lambda h, b: (b, CB_GATES)),
                  cw(CB_Q), cw(CB_K), cw(CB_V),
                  pl.BlockSpec((8, DK), lambda h, b: (0, 0)),
                  pl.BlockSpec((1, DK), lambda h, b: (0, 0)), st] + extra_specs,
        out_specs=[pl.BlockSpec((R, DK), lambda h, b: (b, h)), st],
        out_shape=[jax.ShapeDtypeStruct((B * GROUP, QK), bf16),
                   jax.ShapeDtypeStruct(S_all.shape, f32)],
        input_output_aliases={n_in: 1} if S_prev is not None else {},
        compiler_params=_cparams(("parallel", "parallel")),
        name="gdn_sample",
    )(U8, U8, U8, U8, U8, conv_w, conv_w, conv_w, hc, norm_w.reshape(1, DK), S_all, *extra_ops)


def _mlstm_sample_kernel(q_ref, k_ref, v_ref, og_ref, gt_ref, gc_ref, nw_ref, c0_ref, n0_ref, m0_ref,
                         *rest):
    h_ref, c_ref, n_ref, m_ref = rest[-4:]
    h = pl.program_id(0)
    BB = c0_ref.shape[0]
    R = BB * GROUP
    i_pre, logf = _mlstm_gates(gt_ref[...], gc_ref)
    is_tok = _group_pos((R, 1)) >= TOK0
    i_pre = jnp.where(is_tok, _colsel(i_pre, 2 * HEADS + h), NEG)
    logf = jnp.where(is_tok, _colsel(logf, 3 * HEADS + h), 0.0)
    grp = lambda a: a.reshape(BB, GROUP, a.shape[-1])
    old = lambda ref: grp(ref[...])[:, TOK0 - 1:TOK0, :]
    hh, Cm, nr, m = _mlstm_chunk(grp(q_ref[...]), grp(k_ref[...] * (DK ** -0.5)), grp(v_ref[...]).astype(bf16),
                                 grp(i_pre), grp(logf), c0_ref[...], old(n0_ref), old(m0_ref)[:, :, 0:1])
    c_ref[...] = Cm
    n_ref[...] = jnp.broadcast_to(nr, (BB, GROUP, DK)).reshape(R, DK)
    m_ref[...] = jnp.broadcast_to(m, (BB, GROUP, DK)).reshape(R, DK)
    h_ref[...] = (_rms(hh.reshape(R, DK)) * nw_ref[h] * jax.nn.sigmoid(og_ref[...])).astype(h_ref.dtype)


def _mlstm_sample(U8, C_all, d, C_prev, n0g, m0g, hc, norm_w, *, BB=SAMPLE_SEQS_PER_STEP):
    B = C_all.shape[1]
    R = BB * GROUP
    tok = lambda cb: pl.BlockSpec((R, DK), lambda h, b: (b, cb + h))
    row = pl.BlockSpec((R, DK), lambda h, b: (b, h))
    st = pl.BlockSpec((None, BB, None, DK, DK), lambda h, b: (d, b, h, 0, 0))
    extra_specs, extra_ops, _ = _stacked_out(C_prev)
    n_in = 10
    return pl.pallas_call(
        _mlstm_sample_kernel,
        grid=(HEADS, B // BB),
        in_specs=[tok(CB_MQ), tok(CB_MK), tok(CB_MV), tok(CB_MO),
                  pl.BlockSpec((R, DK), lambda h, b: (b, CB_GATES)),
                  pl.BlockSpec((8, DK), lambda h, b: (0, 0)),
                  pl.BlockSpec((HEADS, 1, DK), lambda h, b: (0, 0, 0)),
                  st, row, row] + extra_specs,
        out_specs=[row, st, row, row],
        out_shape=[jax.ShapeDtypeStruct((B * GROUP, QK), bf16),
                   jax.ShapeDtypeStruct(C_all.shape, f32),
                   jax.ShapeDtypeStruct((B * GROUP, QK), f32),
                   jax.ShapeDtypeStruct((B * GROUP, QK), f32)],
        input_output_aliases={n_in: 1} if C_prev is not None else {},
        compiler_params=_cparams(("parallel", "parallel")),
        name="mlstm_sample",
    )(U8, U8, U8, U8, U8, hc, norm_w.reshape(HEADS, 1, DK), C_all, n0g, m0g, *extra_ops)


def _lru_sample_kernel(x_ref, y_ref, h0_ref, cw_ref, cb_ref, wr_ref, br_ref, wi_ref, bi_ref, lam_ref,
                       o_ref, h_ref):
    c = _conv_groups(x_ref[...], cw_ref) + cb_ref[...]
    a, u = _lru_gates(c, wr_ref, br_ref, wi_ref, bi_ref, lam_ref)
    hs = h0_ref[...]
    pos = _group_pos(hs.shape)
    for t in range(GROUP - TOK0):
        hs = jnp.where(pos == TOK0 + t, a * pltpu.roll(hs, 1, axis=0) + u, hs)
    h_ref[...] = hs
    o_ref[...] = (hs * jax.nn.gelu(y_ref[...])).astype(o_ref.dtype)


def _lru_sample(U8, h0g, cw, cb, wr, br, wi, bi, lam, *, R=128):
    W = LRU_W
    rows = U8.shape[0]
    vec = lambda: pl.BlockSpec((1, W), lambda i: (0, 0))
    return pl.pallas_call(
        _lru_sample_kernel,
        grid=(rows // R,),
        in_specs=[pl.BlockSpec((R, W), lambda i: (i, CB_LX)),
                  pl.BlockSpec((R, W), lambda i: (i, CB_LY)),
                  pl.BlockSpec((R, W), lambda i: (i, 0)),
                  pl.BlockSpec((CONV_W, W), lambda i: (0, 0)), vec(),
                  pl.BlockSpec((W, W), lambda i: (0, 0)), vec(),
                  pl.BlockSpec((W, W), lambda i: (0, 0)), vec(), vec()],
        out_specs=[pl.BlockSpec((R, W), lambda i: (i, 0)), pl.BlockSpec((R, W), lambda i: (i, 0))],
        out_shape=[jax.ShapeDtypeStruct((rows, W), bf16), jax.ShapeDtypeStruct((rows, W), f32)],
        compiler_params=_cparams(("parallel",)),
        name="lru_sample",
    )(U8, U8, h0g, cw, cb.reshape(1, W), wr, br.reshape(1, W), wi, bi.reshape(1, W), lam.reshape(1, W))


def _permute_w_in(w_in):
    s = {}
    off = 0
    for name, size in (("qkv", CONV_CH), ("z", QK), ("gb", HEADS), ("ga", HEADS), ("mq", QK), ("mk", QK),
                       ("mv", QK), ("mo", QK), ("mi", HEADS), ("mf", HEADS), ("lx", LRU_W), ("ly", LRU_W)):
        s[name] = w_in[..., off:off + size]
        off += size
    pad = jnp.zeros(w_in.shape[:-1] + (DK - 4 * HEADS,), w_in.dtype)
    return jnp.concatenate([s["qkv"], s["z"], s["mq"], s["mk"], s["mv"], s["mo"], s["lx"], s["ly"],
                            s["gb"], s["ga"], s["mi"], s["mf"], pad], axis=-1)


def _block_diag(w):
    nb, d, e = w.shape
    eye = jnp.eye(nb, dtype=w.dtype)
    return (w[:, :, None, :] * eye[:, None, :, None]).reshape(nb * d, nb * e)


def _gate_consts(a_log, dt_bias, i_bias, f_bias):
    row = lambda v, k: jnp.pad(v, (k * HEADS, DK - (k + 1) * HEADS))
    rows = [row(a_log, 1), row(dt_bias, 1), row(i_bias, 2), row(f_bias, 3)]
    return jnp.stack(rows + [jnp.zeros((DK,), f32)] * (8 - len(rows)))


def _state_row(state):
    B, W = state.shape
    return jnp.pad(state[:, None, :], ((0, 0), (TOK0 - 1, GROUP - TOK0), (0, 0))).reshape(B * GROUP, W)


def _tail_rows(U, B, T, col, width):
    return jnp.stack([lax.slice(U, ((b + 1) * T - (CONV_W - 1), col), ((b + 1) * T, col + width))
                      for b in range(B)])


def kernel(x_prompt, x_sample, state_gdn_S, state_gdn_conv, state_mlstm_C, state_mlstm_n, state_mlstm_m, state_lru_h, state_lru_conv, ffn1_wg, ffn1_wu, ffn1_wd, ln_g, ln_b, w_in, gdn_conv_w, gdn_A_log, gdn_dt_bias, gdn_norm_w, mlstm_i_bias, mlstm_f_bias, mlstm_norm_w, lru_conv_w, lru_conv_b, lru_wr, lru_br, lru_wi, lru_bi, lru_lambda, w_out, ffn2_wg, ffn2_wu, ffn2_wd):
    BP, TP, D = x_prompt.shape
    BS, TS, _ = x_sample.shape
    MP, MS = BP * TP, BS * TS
    ffn1 = (ffn1_wg, ffn1_wu, ffn1_wd)
    ffn2 = (ffn2_wg, ffn2_wu, ffn2_wd)
    w_in_p = _permute_w_in(w_in.astype(bf16))
    w_out_b = w_out.astype(bf16)

    xp, xs = x_prompt.reshape(MP, D), x_sample.reshape(MS, D)
    outs_p = [[] for _ in range(7)]
    outs_s = [[] for _ in range(7)]
    gS_s = mC_s = None
    for d in range(DEPTH):
        x1p, x1s = _ffn_ln_both(xp, xs, ffn1, d, ln_g[d, 0], ln_b[d, 0])
        U = _inproj(x1p, w_in_p, d)
        hc = _gate_consts(gdn_A_log[d], gdn_dt_bias[d], mlstm_i_bias[d], mlstm_f_bias[d])
        wr = _block_diag(lru_wr[d]).astype(bf16)
        wi = _block_diag(lru_wi[d]).astype(bf16)
        lru_w = (lru_conv_w[d], lru_conv_b[d], wr, lru_br[d], wi, lru_bi[d], lru_lambda[d])

        og_p, gS_p = _gdn_prompt(U, BP, TP, gdn_conv_w[d], hc, gdn_norm_w[d])
        hm_p, mC_p, mn_p, mm_p = _mlstm_prompt(U, BP, TP, hc, mlstm_norm_w[d])
        ol_p, lh_p = _lru_prompt(U, BP, TP, *lru_w)
        outs_p[0].append(gS_p)
        outs_p[1].append(_tail_rows(U, BP, TP, 0, CONV_CH))
        outs_p[2].append(mC_p)
        outs_p[3].append(mn_p.reshape(BP, HEADS, DK))
        outs_p[4].append(mm_p[:, :, 0, 0])
        outs_p[5].append(lh_p.reshape(BP, LRU_W))
        outs_p[6].append(_tail_rows(U, BP, TP, COL_LX, LRU_W))

        Us = _inproj(x1s, w_in_p, d).reshape(BS, TS, D_INP)
        hist = jnp.concatenate([
            state_gdn_conv[d], jnp.zeros((BS, 3, COL_LX - CONV_CH), f32),
            state_lru_conv[d], jnp.zeros((BS, 3, D_INP - COL_LX - LRU_W), f32)], axis=2)
        U8 = jnp.concatenate([jnp.zeros((BS, 1, D_INP), f32), hist, Us], axis=1).reshape(BS * GROUP, D_INP)
        og_s, gS_s = _gdn_sample(U8, state_gdn_S, d, gS_s, gdn_conv_w[d], hc, gdn_norm_w[d])
        hm_s, mC_s, n8, m8 = _mlstm_sample(
            U8, state_mlstm_C, d, mC_s, _state_row(state_mlstm_n[d].reshape(BS, QK)),
            _state_row(jnp.repeat(state_mlstm_m[d], DK, axis=1)), hc, mlstm_norm_w[d])
        ol_s, h8 = _lru_sample(U8, _state_row(state_lru_h[d]), *lru_w)
        toks = lambda a: a.reshape(BS, GROUP, -1)[:, TOK0:].reshape(MS, -1)
        last = lambda a: a.reshape(BS, GROUP, -1)[:, GROUP - 1]
        outs_s[1].append(Us[:, TS - 3:, 0:CONV_CH])
        outs_s[3].append(last(n8).reshape(BS, HEADS, DK))
        outs_s[4].append(last(m8).reshape(BS, HEADS, DK)[:, :, 0])
        outs_s[5].append(last(h8))
        outs_s[6].append(Us[:, TS - 3:, COL_LX:COL_LX + LRU_W])

        x2p = _outproj_ln((og_p, hm_p, ol_p), x1p, w_out_b, d, ln_g[d, 1], ln_b[d, 1])
        x2s = _outproj_ln((toks(og_s), toks(hm_s), toks(ol_s)), x1s, w_out_b, d, ln_g[d, 1], ln_b[d, 1])
        xp, xs = _ffn_ln_both(x2p, x2s, ffn2, d, ln_g[d, 2], ln_b[d, 2])

    y_prompt = xp.reshape(BP, TP, D)
    y_sample = xs.reshape(BS, TS, D)
    stack = lambda o: jnp.stack(o)
    return (y_prompt, y_sample,
            *[stack(o) for o in outs_p],
            gS_s, stack(outs_s[1]), mC_s, *[stack(outs_s[i]) for i in range(3, 7)])
```

```python
import functools
import math

import jax
import jax.numpy as jnp
from jax import lax
from jax.experimental import pallas as pl
from jax.experimental.pallas import tpu as pltpu

f32 = jnp.float32
bf16 = jnp.bfloat16

DEPTH = 2
D_MODEL = 2048
D_FF = 5632
HEADS = 6
DK = 128
LRU_W = 512
LRU_BLOCKS = 8
LRU_C = 8.0
CONV_W = 4
CHUNK = 64
ALPHA = (2 * DEPTH) ** 0.25
NORM_EPS = 1e-6
NEG = -1e30

QK = HEADS * DK
CONV_CH = 3 * QK
MIX_W = 4 * QK
CB_Q, CB_K, CB_V, CB_Z = 0, 6, 12, 18
CB_MQ, CB_MK, CB_MV, CB_MO = 24, 30, 36, 42
CB_GDN, CB_MLSTM = 0, 1
CB_LX, CB_LY = 12, 13
CB_GATES = 56
COL_LX = 6144
D_INP = 7296
GROUP = 8
TOK0 = 4
ROW_TILE = 512
HEAD_ROW_BLOCKS = 2
PROMPT_SEQS_PER_GROUP = 2
SAMPLE_SEQS_PER_STEP = 64

VMEM_LIMIT = 60 * 1024 * 1024


def _cparams(sem):
    return pltpu.CompilerParams(dimension_semantics=sem, vmem_limit_bytes=VMEM_LIMIT)


def _silu(x):
    return x * jax.nn.sigmoid(x)


def _softplus(x):
    return jnp.maximum(x, 0.0) + jnp.log1p(jnp.exp(-jnp.abs(x)))


def _expm1(x):
    u = jnp.exp(x)
    um1 = u - 1.0
    lg = jnp.where(u == 1.0, 1.0, jnp.log(jnp.where(u == 0.0, 1.0, u)))
    return jnp.where(u == 1.0, x, jnp.where(u == 0.0, -1.0, um1 * x / lg))


def _layernorm(y, g, b):
    mu = jnp.mean(y, -1, keepdims=True)
    d = y - mu
    var = jnp.mean(d * d, -1, keepdims=True)
    return d * lax.rsqrt(var + NORM_EPS) * g + b


def _rms(x):
    return x * lax.rsqrt(jnp.mean(x * x, -1, keepdims=True) + NORM_EPS)


def _l2(x):
    return x * lax.rsqrt(jnp.sum(x * x, -1, keepdims=True) + NORM_EPS)


def _colsel(gates, idx):
    lane = lax.broadcasted_iota(jnp.int32, gates.shape, 1)
    return jnp.sum(jnp.where(lane == idx, gates, 0.0), axis=1, keepdims=True)


def _dot(a, b):
    return jnp.dot(a, b, preferred_element_type=f32)


def _bmm(a, b):
    return jnp.einsum('nij,njk->nik', a, b, preferred_element_type=f32)


def _bmm_nt(a, b):
    return jnp.einsum('nid,njd->nij', a, b, preferred_element_type=f32)


def _bmm_tn(a, b):
    return jnp.einsum('nck,ncv->nkv', a, b, preferred_element_type=f32)


def _ij(C):
    return (lax.broadcasted_iota(jnp.int32, (C, C), 0), lax.broadcasted_iota(jnp.int32, (C, C), 1))


def _cumsum_row(col3, ii, jj):
    return jnp.sum(jnp.where((ii <= jj)[None], col3, 0.0), axis=1, keepdims=True)


def _row2col(row3, ii, jj):
    return jnp.sum(jnp.where((ii == jj)[None], row3, 0.0), axis=2, keepdims=True)


def _col2row(col3, ii, jj):
    return jnp.sum(jnp.where((ii == jj)[None], col3, 0.0), axis=1, keepdims=True)


def _conv_carry(x, w_ref, xp_ref, tail):
    T = x.shape[0]
    xp_ref[0:8, :] = tail
    xp_ref[8:T + 8, :] = x
    y = w_ref[3:4, :] * x
    for k in range(CONV_W - 1):
        y = y + w_ref[k:k + 1, :] * xp_ref[pl.ds(5 + k, T), :]
    return y


def _conv_groups(x, w_ref):
    y = w_ref[3:4, :] * x
    for s in range(1, CONV_W):
        y = y + w_ref[3 - s:4 - s, :] * pltpu.roll(x, s, axis=0)
    return y


def _group_pos(shape):
    return lax.broadcasted_iota(jnp.int32, shape, 0) & (GROUP - 1)


def _ffn_ln_kernel(*refs, cast_weights, aliased):
    x_ref, wg_ref, wu_ref, wd_ref, g_ref, b_ref = refs[:6]
    n_in = 7 if aliased else 6
    o_ref = refs[n_in]
    xb_ref = refs[-1]
    j = pl.program_id(1)

    @pl.when(j == 0)
    def _():
        o_ref[...] = jnp.zeros_like(o_ref)
        xb_ref[...] = x_ref[...].astype(bf16)

    if cast_weights:
        wg, wu, wd = (r[...].astype(bf16) for r in (wg_ref, wu_ref, wd_ref))
        for w_out_ref, w in zip(refs[n_in + 1:n_in + 4], (wg, wu, wd)):
            w_out_ref[...] = w
    else:
        wg, wu, wd = wg_ref[...], wu_ref[...], wd_ref[...]
    xb = xb_ref[...]
    h = (_silu(_dot(xb, wg)) * _dot(xb, wu)).astype(bf16)
    o_ref[...] += _dot(h, wd)

    @pl.when(j == pl.num_programs(1) - 1)
    def _():
        y = ALPHA * x_ref[...] + 0.5 * o_ref[...]
        o_ref[...] = _layernorm(y, g_ref[...], b_ref[...])


def _ffn_ln(x, wg, wu, wd, g, b, *, tm, tf=512, row0=0, prev=None):
    M, D = x.shape
    F = wg.shape[1]
    rows = lambda i, j: (i + row0, 0)
    in_specs = [
        pl.BlockSpec((tm, D), rows),
        pl.BlockSpec((D, tf), lambda i, j: (0, j)),
        pl.BlockSpec((D, tf), lambda i, j: (0, j)),
        pl.BlockSpec((tf, D), lambda i, j: (j, 0)),
        pl.BlockSpec((1, D), lambda i, j: (0, 0)),
        pl.BlockSpec((1, D), lambda i, j: (0, 0)),
    ]
    operands = [x, wg, wu, wd, g.reshape(1, D), b.reshape(1, D)]
    if prev is not None:
        in_specs.append(pl.BlockSpec(memory_space=pl.ANY))
        operands.append(prev)
    return pl.pallas_call(
        functools.partial(_ffn_ln_kernel, cast_weights=False, aliased=prev is not None),
        grid=(M // tm - row0, F // tf),
        in_specs=in_specs,
        out_specs=pl.BlockSpec((tm, D), rows),
        out_shape=jax.ShapeDtypeStruct((M, D), f32),
        input_output_aliases={6: 0} if prev is not None else {},
        scratch_shapes=[pltpu.VMEM((tm, D), bf16)],
        compiler_params=_cparams(("parallel", "arbitrary")),
        name="ffn_ln",
    )(*operands)


def _ffn_ln_head(x, wg, wu, wd, d, g, b, *, tm, tf=256):
    M, D = x.shape
    F = wg.shape[2]
    return pl.pallas_call(
        functools.partial(_ffn_ln_kernel, cast_weights=True, aliased=False),
        grid=(1, F // tf),
        in_specs=[
            pl.BlockSpec((tm, D), lambda i, j: (0, 0), pipeline_mode=pl.Buffered(1)),
            pl.BlockSpec((None, D, tf), lambda i, j: (d, 0, j)),
            pl.BlockSpec((None, D, tf), lambda i, j: (d, 0, j)),
            pl.BlockSpec((None, tf, D), lambda i, j: (d, j, 0)),
            pl.BlockSpec((1, D), lambda i, j: (0, 0)),
            pl.BlockSpec((1, D), lambda i, j: (0, 0)),
        ],
        out_specs=[pl.BlockSpec((tm, D), lambda i, j: (0, 0)),
                   pl.BlockSpec((D, tf), lambda i, j: (0, j)),
                   pl.BlockSpec((D, tf), lambda i, j: (0, j)),
                   pl.BlockSpec((tf, D), lambda i, j: (j, 0))],
        out_shape=[jax.ShapeDtypeStruct((M, D), f32), jax.ShapeDtypeStruct((D, F), bf16),
                   jax.ShapeDtypeStruct((D, F), bf16), jax.ShapeDtypeStruct((F, D), bf16)],
        scratch_shapes=[pltpu.VMEM((tm, D), bf16)],
        compiler_params=_cparams(("arbitrary", "arbitrary")),
        name="ffn_ln_head",
    )(x, wg, wu, wd, g.reshape(1, D), b.reshape(1, D))


def _ffn_ln_both(xp, xs, w32, d, g, b):
    out, *wb = _ffn_ln_head(xp, *w32, d, g, b, tm=HEAD_ROW_BLOCKS * ROW_TILE)
    yp = _ffn_ln(xp, *wb, g, b, tm=ROW_TILE, row0=HEAD_ROW_BLOCKS, prev=out)
    ys = _ffn_ln(xs, *wb, g, b, tm=ROW_TILE)
    return yp, ys


def _inproj_kernel(x_ref, w_ref, o_ref):
    o_ref[...] = _dot(x_ref[...].astype(bf16), w_ref[...])


def _inproj(x, w, d, *, tm=ROW_TILE, tn=2432):
    M, D = x.shape
    N = w.shape[2]
    return pl.pallas_call(
        _inproj_kernel,
        grid=(N // tn, M // tm),
        in_specs=[pl.BlockSpec((tm, D), lambda n, i: (i, 0)),
                  pl.BlockSpec((None, D, tn), lambda n, i: (d, 0, n))],
        out_specs=pl.BlockSpec((tm, tn), lambda n, i: (i, n)),
        out_shape=jax.ShapeDtypeStruct((M, N), f32),
        compiler_params=_cparams(("parallel", "arbitrary")),
        name="in_proj",
    )(x, w)


def _outproj_ln_kernel(og_ref, hm_ref, ol_ref, x_ref, w_ref, g_ref, b_ref, o_ref):
    mix = (_dot(og_ref[...], w_ref[0:QK, :]) + _dot(hm_ref[...], w_ref[QK:2 * QK, :])
           + _dot(ol_ref[...], w_ref[2 * QK:, :]))
    o_ref[...] = _layernorm(ALPHA * x_ref[...] + mix, g_ref[...], b_ref[...])


def _outproj_ln(mix, x, w, d, g, b, *, tm=ROW_TILE):
    M, D = x.shape
    return pl.pallas_call(
        _outproj_ln_kernel,
        grid=(M // tm,),
        in_specs=[pl.BlockSpec((tm, wd), lambda i: (i, 0)) for wd in (QK, QK, LRU_W)]
        + [pl.BlockSpec((tm, D), lambda i: (i, 0)),
           pl.BlockSpec((None, D, D), lambda i: (d, 0, 0)),
           pl.BlockSpec((1, D), lambda i: (0, 0)),
           pl.BlockSpec((1, D), lambda i: (0, 0))],
        out_specs=pl.BlockSpec((tm, D), lambda i: (i, 0)),
        out_shape=jax.ShapeDtypeStruct((M, D), f32),
        compiler_params=_cparams(("parallel",)),
        name="out_proj_ln",
    )(*mix, x, w, g.reshape(1, D), b.reshape(1, D))


def _gdn_gates(gates, gc_ref):
    beta = jax.nn.sigmoid(gates)
    gl = -jnp.exp(gc_ref[0:1, :]) * _softplus(gates + gc_ref[1:2, :])
    return beta, gl


def _mlstm_gates(gates, gc_ref):
    i_pre = gates + gc_ref[2:3, :]
    logf = -_softplus(-(gates + gc_ref[3:4, :]))
    return i_pre, logf


def _gdn_chunk(q3, k3, v3, b3, g3, S):
    C = q3.shape[1]
    ii, jj = _ij(C)
    g_row = _cumsum_row(g3, ii, jj)
    g_col = _row2col(g_row, ii, jj)
    decay = jnp.exp(jnp.where((ii >= jj)[None], g_col - g_row, NEG))
    kb = k3.astype(bf16)
    kk = _bmm_nt(kb, kb)
    p = jnp.where((ii > jj)[None], -(b3 * kk * decay), 0.0)
    e = p
    for _ in range(int(math.log2(C)) - 1):
        pb = p.astype(bf16)
        p = _bmm(pb, pb)
        e = e + p + _bmm(e.astype(bf16), p.astype(bf16))
    eb = e.astype(bf16)
    e_g = jnp.exp(g_col)
    rv = b3 * v3
    rk = (b3 * e_g) * k3
    uv = rv + _bmm(eb, rv.astype(bf16))
    wks = (rk + _bmm(eb, rk.astype(bf16))).astype(bf16)
    qk = (_bmm_nt(q3.astype(bf16), kb) * decay).astype(bf16)
    g_last = g_row[:, :, C - 1:C]
    qd = (q3 * e_g).astype(bf16)
    kt = (k3 * jnp.exp(g_last - g_col)).astype(bf16)
    Sb = S.astype(bf16)
    Ub = (uv - _bmm(wks, Sb)).astype(bf16)
    o = _bmm(qd, Sb) + _bmm(qk, Ub)
    return o, jnp.exp(g_last) * S + _bmm_tn(kt, Ub)


def _mlstm_chunk(q3, k3, vb, i3, f3, Cm, nr, m):
    C = q3.shape[1]
    ii, jj = _ij(C)
    b_row = _cumsum_row(f3, ii, jj)
    b_col = _row2col(b_row, ii, jj)
    i_row = _col2row(i3, ii, jj)
    D = jnp.where((ii >= jj)[None], b_col - b_row + i_row, NEG)
    d_max = jnp.max(D, axis=2, keepdims=True)
    qb = q3.astype(bf16)
    qk = _bmm_nt(qb, k3.astype(bf16))
    m_t = jnp.maximum(b_col + m, d_max)
    inter = jnp.exp(b_col + m - m_t)
    Sw = jnp.exp(D - m_t) * qk
    num = inter * _bmm(qb, Cm.astype(bf16)) + _bmm(Sw.astype(bf16), vb)
    den = inter * jnp.sum(q3 * nr, -1, keepdims=True) + jnp.sum(Sw, -1, keepdims=True)
    hh = num / jnp.maximum(jnp.abs(den), jnp.exp(-m_t))
    m_new = m_t[:, C - 1:C]
    b_last = b_col[:, C - 1:C]
    kw = jnp.exp(b_last - b_col + i3 - m_new) * k3
    dec = jnp.exp(b_last + m - m_new)
    return (hh, dec * Cm + _bmm_tn(kw.astype(bf16), vb), dec * nr + jnp.sum(kw, axis=1, keepdims=True), m_new)


def _gdn_prompt_kernel(*refs, C, B):
    mains, gts = refs[0:2 * B:2], refs[1:2 * B:2]
    cw_ref, gc_ref, nw_ref, o_ref, s_ref, xp_ref, tail_ref = refs[2 * B:]
    n = pl.program_id(0)
    NC = B * HEADS

    @pl.when(n == 0)
    def _():
        tail_ref[...] = jnp.zeros_like(tail_ref)
        s_ref[...] = jnp.zeros_like(s_ref)

    qs, ks, vs, zs, bs, gs = [], [], [], [], [], []
    for b in range(B):
        x = mains[b][:, 0:CONV_CH]
        y = _silu(_conv_carry(x, cw_ref, xp_ref, tail_ref[b]))
        tail_ref[b] = x[C - 8:C]
        beta, gl = _gdn_gates(gts[b][...], gc_ref)
        for h in range(HEADS):
            qs.append(y[:, h * DK:(h + 1) * DK])
            ks.append(y[:, QK + h * DK:QK + (h + 1) * DK])
            vs.append(y[:, 2 * QK + h * DK:2 * QK + (h + 1) * DK])
            zs.append(mains[b][:, CONV_CH + h * DK:CONV_CH + (h + 1) * DK])
            bs.append(beta[:, h:h + 1])
            gs.append(gl[:, HEADS + h:HEADS + h + 1])
    GB = PROMPT_SEQS_PER_GROUP
    for b0 in range(0, B, GB):
        sl = slice(b0 * HEADS, (b0 + GB) * HEADS)
        q3 = _l2(jnp.stack(qs[sl])) * (DK ** -0.5)
        k3 = _l2(jnp.stack(ks[sl]))
        v3 = jnp.stack(vs[sl])
        b3 = jnp.stack(bs[sl])
        g3 = jnp.stack(gs[sl])
        o, S = _gdn_chunk(q3, k3, v3, b3, g3, s_ref[b0:b0 + GB].reshape(GB * HEADS, DK, DK))
        s_ref[b0:b0 + GB] = S.reshape(GB, HEADS, DK, DK)
        o = _rms(o) * nw_ref[...] * _silu(jnp.stack(zs[sl]))
        for b in range(GB):
            for h in range(HEADS):
                o_ref[b0 + b, :, h * DK:(h + 1) * DK] = o[b * HEADS + h].astype(o_ref.dtype)


def _gdn_prompt(U, B, T, conv_w, hc, norm_w, *, C=CHUNK):
    N = T // C
    in_specs = []
    for b in range(B):
        in_specs.append(pl.BlockSpec((C, MIX_W), lambda n, b=b: (b * N + n, CB_GDN)))
        in_specs.append(pl.BlockSpec((C, DK), lambda n, b=b: (b * N + n, CB_GATES)))
    in_specs += [pl.BlockSpec((CONV_W, CONV_CH), lambda n: (0, 0)),
                 pl.BlockSpec((8, DK), lambda n: (0, 0)),
                 pl.BlockSpec((1, DK), lambda n: (0, 0))]
    o, S = pl.pallas_call(
        functools.partial(_gdn_prompt_kernel, C=C, B=B),
        grid=(N,),
        in_specs=in_specs,
        out_specs=[pl.BlockSpec((B, None, C, QK), lambda n: (0, n, 0, 0)),
                   pl.BlockSpec((B, HEADS, DK, DK), lambda n: (0, 0, 0, 0))],
        out_shape=[jax.ShapeDtypeStruct((B, N, C, QK), bf16),
                   jax.ShapeDtypeStruct((B, HEADS, DK, DK), f32)],
        scratch_shapes=[pltpu.VMEM((C + 8, CONV_CH), f32), pltpu.VMEM((B, 8, CONV_CH), f32)],
        compiler_params=_cparams(("arbitrary",)),
        name="gdn_prompt",
    )(*([U, U] * B), conv_w, hc, norm_w.reshape(1, DK))
    return o.reshape(B * T, QK), S


def _mlstm_prompt_kernel(*refs, C, B):
    mains, gts = refs[0:2 * B:2], refs[1:2 * B:2]
    gc_ref, nw_ref, h_ref, c_ref, n_ref, m_ref = refs[2 * B:]
    n = pl.program_id(0)
    NC = B * HEADS

    @pl.when(n == 0)
    def _():
        c_ref[...] = jnp.zeros_like(c_ref)
        n_ref[...] = jnp.zeros_like(n_ref)
        m_ref[...] = jnp.zeros_like(m_ref)

    qs, ks, vs, os_, is_, fs = [], [], [], [], [], []
    for b in range(B):
        i_pre, logf = _mlstm_gates(gts[b][...], gc_ref)
        for h in range(HEADS):
            qs.append(mains[b][:, h * DK:(h + 1) * DK])
            ks.append(mains[b][:, QK + h * DK:QK + (h + 1) * DK])
            vs.append(mains[b][:, 2 * QK + h * DK:2 * QK + (h + 1) * DK])
            os_.append(mains[b][:, 3 * QK + h * DK:3 * QK + (h + 1) * DK])
            is_.append(i_pre[:, 2 * HEADS + h:2 * HEADS + h + 1])
            fs.append(logf[:, 3 * HEADS + h:3 * HEADS + h + 1])
    q3 = jnp.stack(qs)
    k3 = jnp.stack(ks) * (DK ** -0.5)
    vb = jnp.stack(vs).astype(bf16)
    i3 = jnp.stack(is_)
    f3 = jnp.stack(fs)
    hh, Cm, nr, m = _mlstm_chunk(q3, k3, vb, i3, f3, c_ref[...].reshape(NC, DK, DK),
                                 n_ref[...].reshape(NC, 1, DK), m_ref[...].reshape(NC, 1, DK)[:, :, 0:1])
    nw = jnp.stack([nw_ref[h] for _ in range(B) for h in range(HEADS)])
    hh = _rms(hh) * nw * jax.nn.sigmoid(jnp.stack(os_))
    c_ref[...] = Cm.reshape(B, HEADS, DK, DK)
    n_ref[...] = nr.reshape(B, HEADS, 1, DK)
    m_ref[...] = jnp.broadcast_to(m, (NC, 1, DK)).reshape(B, HEADS, 1, DK)
    for b in range(B):
        for h in range(HEADS):
            h_ref[b, :, h * DK:(h + 1) * DK] = hh[b * HEADS + h].astype(h_ref.dtype)


def _mlstm_prompt(U, B, T, hc, norm_w, *, C=CHUNK):
    N = T // C
    in_specs = []
    for b in range(B):
        in_specs.append(pl.BlockSpec((C, MIX_W), lambda n, b=b: (b * N + n, CB_MLSTM)))
        in_specs.append(pl.BlockSpec((C, DK), lambda n, b=b: (b * N + n, CB_GATES)))
    in_specs += [pl.BlockSpec((8, DK), lambda n: (0, 0)),
                 pl.BlockSpec((HEADS, 1, DK), lambda n: (0, 0, 0))]
    st = lambda r: pl.BlockSpec((B, HEADS, r, DK), lambda n: (0, 0, 0, 0))
    h, Cm, nr, m = pl.pallas_call(
        functools.partial(_mlstm_prompt_kernel, C=C, B=B),
        grid=(N,),
        in_specs=in_specs,
        out_specs=[pl.BlockSpec((B, None, C, QK), lambda n: (0, n, 0, 0)), st(DK), st(1), st(1)],
        out_shape=[jax.ShapeDtypeStruct((B, N, C, QK), bf16),
                   jax.ShapeDtypeStruct((B, HEADS, DK, DK), f32),
                   jax.ShapeDtypeStruct((B, HEADS, 1, DK), f32),
                   jax.ShapeDtypeStruct((B, HEADS, 1, DK), f32)],
        compiler_params=_cparams(("arbitrary",)),
        name="mlstm_prompt",
    )(*([U, U] * B), hc, norm_w.reshape(HEADS, 1, DK))
    return h.reshape(B * T, QK), Cm, nr, m


def _lru_gates(c, wr_ref, br_ref, wi_ref, bi_ref, lam_ref):
    cb = c.astype(bf16)
    r = jax.nn.sigmoid(_dot(cb, wr_ref[...]) + br_ref[...])
    ig = jax.nn.sigmoid(_dot(cb, wi_ref[...]) + bi_ref[...])
    log_a = -LRU_C * r * _softplus(-lam_ref[...])
    a = jnp.exp(log_a)
    u = jnp.sqrt(-_expm1(2.0 * log_a)) * (ig * c)
    return a, u


def _lru_prompt_kernel(x_ref, y_ref, cw_ref, cb_ref, wr_ref, br_ref, wi_ref, bi_ref, lam_ref,
                       o_ref, hl_ref, xp_ref, tail_ref, hc_ref):
    tc = pl.program_id(1)
    Tc, W = x_ref.shape

    @pl.when(tc == 0)
    def _():
        tail_ref[...] = jnp.zeros_like(tail_ref)
        hc_ref[...] = jnp.zeros_like(hc_ref)

    x = x_ref[...]
    c = _conv_carry(x, cw_ref, xp_ref, tail_ref[...]) + cb_ref[...]
    tail_ref[...] = x[Tc - 8:Tc]
    a, u = _lru_gates(c, wr_ref, br_ref, wi_ref, bi_ref, lam_ref)
    row = lax.broadcasted_iota(jnp.int32, (Tc, W), 0)
    d = 1
    while d < Tc:
        keep = row >= d
        a_sh = jnp.where(keep, pltpu.roll(a, d, axis=0), 1.0)
        u_sh = jnp.where(keep, pltpu.roll(u, d, axis=0), 0.0)
        u = a * u_sh + u
        a = a * a_sh
        d *= 2
    hs = u + a * hc_ref[0:1, :]
    last = hs[Tc - 1:Tc]
    hc_ref[...] = jnp.broadcast_to(last, hc_ref.shape)
    hl_ref[...] = last
    o_ref[...] = (hs * jax.nn.gelu(y_ref[...])).astype(o_ref.dtype)


def _lru_prompt(U, B, T, cw, cb, wr, br, wi, bi, lam, *, Tc=512):
    W = LRU_W
    nt = T // Tc
    vec = lambda: pl.BlockSpec((1, W), lambda b, t: (0, 0))
    return pl.pallas_call(
        _lru_prompt_kernel,
        grid=(B, nt),
        in_specs=[pl.BlockSpec((Tc, W), lambda b, t: (b * nt + t, CB_LX)),
                  pl.BlockSpec((Tc, W), lambda b, t: (b * nt + t, CB_LY)),
                  pl.BlockSpec((CONV_W, W), lambda b, t: (0, 0)), vec(),
                  pl.BlockSpec((W, W), lambda b, t: (0, 0)), vec(),
                  pl.BlockSpec((W, W), lambda b, t: (0, 0)), vec(), vec()],
        out_specs=[pl.BlockSpec((Tc, W), lambda b, t: (b * nt + t, 0)),
                   pl.BlockSpec((None, 1, W), lambda b, t: (b, 0, 0))],
        out_shape=[jax.ShapeDtypeStruct((B * T, W), bf16),
                   jax.ShapeDtypeStruct((B, 1, W), f32)],
        scratch_shapes=[pltpu.VMEM((Tc + 8, W), f32), pltpu.VMEM((8, W), f32), pltpu.VMEM((8, W), f32)],
        compiler_params=_cparams(("parallel", "arbitrary")),
        name="lru_prompt",
    )(U, U, cw, cb.reshape(1, W), wr, br.reshape(1, W), wi, bi.reshape(1, W), lam.reshape(1, W))


def _gdn_sample_kernel(q_ref, k_ref, v_ref, z_ref, gt_ref, wq_ref, wk_ref, wv_ref, gc_ref, nw_ref, s0_ref,
                       *rest):
    o_ref, s_ref = rest[-2:]
    h = pl.program_id(0)
    BB = s0_ref.shape[0]
    R = BB * GROUP
    q = _l2(_silu(_conv_groups(q_ref[...], wq_ref))) * (DK ** -0.5)
    k = _l2(_silu(_conv_groups(k_ref[...], wk_ref)))
    v = _silu(_conv_groups(v_ref[...], wv_ref))
    beta, gl = _gdn_gates(gt_ref[...], gc_ref)
    is_tok = _group_pos((R, 1)) >= TOK0
    beta = jnp.where(is_tok, _colsel(beta, h), 0.0)
    gl = jnp.where(is_tok, _colsel(gl, HEADS + h), 0.0)
    grp = lambda a: a.reshape(BB, GROUP, a.shape[-1])
    o, S = _gdn_chunk(grp(q), grp(k), grp(v), grp(beta), grp(gl), s0_ref[...])
    s_ref[...] = S
    o_ref[...] = (_rms(o.reshape(R, DK)) * nw_ref[...] * _silu(z_ref[...])).astype(o_ref.dtype)


def _stacked_out(prev):
    if prev is None:
        return [], [], {}
    return [pl.BlockSpec(memory_space=pl.ANY)], [prev], None


def _gdn_sample(U8, S_all, d, S_prev, conv_w, hc, norm_w, *, BB=SAMPLE_SEQS_PER_STEP):
    B = S_all.shape[1]
    R = BB * GROUP
    tok = lambda cb: pl.BlockSpec((R, DK), lambda h, b: (b, cb + h))
    cw = lambda cb: pl.BlockSpec((CONV_W, DK), lambda h, b: (0, cb + h))
    st = pl.BlockSpec((None, BB, None, DK, DK), lambda h, b: (d, b, h, 0, 0))
    extra_specs, extra_ops, _ = _stacked_out(S_prev)
    n_in = 11
    return pl.pallas_call(
        _gdn_sample_kernel,
        grid=(HEADS, B // BB),
        in_specs=[tok(CB_Q), tok(CB_K), tok(CB_V), tok(CB_Z),
                  pl.BlockSpec((R, DK), lambda h, b: (b, CB_GATES)),
                  cw(CB_Q), cw(CB_K), cw(CB_V),
                  pl.BlockSpec((8, DK), lambda h, b: (0, 0)),
                  pl.BlockSpec((1, DK), lambda h, b: (0, 0)), st] + extra_specs,
        out_specs=[pl.BlockSpec((R, DK), lambda h, b: (b, h)), st],
        out_shape=[jax.ShapeDtypeStruct((B * GROUP, QK), bf16),
                   jax.ShapeDtypeStruct(S_all.shape, f32)],
        input_output_aliases={n_in: 1} if S_prev is not None else {},
        compiler_params=_cparams(("parallel", "parallel")),
        name="gdn_sample",
    )(U8, U8, U8, U8, U8, conv_w, conv_w, conv_w, hc, norm_w.reshape(1, DK), S_all, *extra_ops)


def _mlstm_sample_kernel(q_ref, k_ref, v_ref, og_ref, gt_ref, gc_ref, nw_ref, c0_ref, n0_ref, m0_ref,
                         *rest):
    h_ref, c_ref, n_ref, m_ref = rest[-4:]
    h = pl.program_id(0)
    BB = c0_ref.shape[0]
    R = BB * GROUP
    i_pre, logf = _mlstm_gates(gt_ref[...], gc_ref)
    is_tok = _group_pos((R, 1)) >= TOK0
    i_pre = jnp.where(is_tok, _colsel(i_pre, 2 * HEADS + h), NEG)
    logf = jnp.where(is_tok, _colsel(logf, 3 * HEADS + h), 0.0)
    grp = lambda a: a.reshape(BB, GROUP, a.shape[-1])
    old = lambda ref: grp(ref[...])[:, TOK0 - 1:TOK0, :]
    hh, Cm, nr, m = _mlstm_chunk(grp(q_ref[...]), grp(k_ref[...] * (DK ** -0.5)), grp(v_ref[...]).astype(bf16),
                                 grp(i_pre), grp(logf), c0_ref[...], old(n0_ref), old(m0_ref)[:, :, 0:1])
    c_ref[...] = Cm
    n_ref[...] = jnp.broadcast_to(nr, (BB, GROUP, DK)).reshape(R, DK)
    m_ref[...] = jnp.broadcast_to(m, (BB, GROUP, DK)).reshape(R, DK)
    h_ref[...] = (_rms(hh.reshape(R, DK)) * nw_ref[h] * jax.nn.sigmoid(og_ref[...])).astype(h_ref.dtype)


def _mlstm_sample(U8, C_all, d, C_prev, n0g, m0g, hc, norm_w, *, BB=SAMPLE_SEQS_PER_STEP):
    B = C_all.shape[1]
    R = BB * GROUP
    tok = lambda cb: pl.BlockSpec((R, DK), lambda h, b: (b, cb + h))
    row = pl.BlockSpec((R, DK), lambda h, b: (b, h))
    st = pl.BlockSpec((None, BB, None, DK, DK), lambda h, b: (d, b, h, 0, 0))
    extra_specs, extra_ops, _ = _stacked_out(C_prev)
    n_in = 10
    return pl.pallas_call(
        _mlstm_sample_kernel,
        grid=(HEADS, B // BB),
        in_specs=[tok(CB_MQ), tok(CB_MK), tok(CB_MV), tok(CB_MO),
                  pl.BlockSpec((R, DK), lambda h, b: (b, CB_GATES)),
                  pl.BlockSpec((8, DK), lambda h, b: (0, 0)),
                  pl.BlockSpec((HEADS, 1, DK), lambda h, b: (0, 0, 0)),
                  st, row, row] + extra_specs,
        out_specs=[row, st, row, row],
        out_shape=[jax.ShapeDtypeStruct((B * GROUP, QK), bf16),
                   jax.ShapeDtypeStruct(C_all.shape, f32),
                   jax.ShapeDtypeStruct((B * GROUP, QK), f32),
                   jax.ShapeDtypeStruct((B * GROUP, QK), f32)],
        input_output_aliases={n_in: 1} if C_prev is not None else {},
        compiler_params=_cparams(("parallel", "parallel")),
        name="mlstm_sample",
    )(U8, U8, U8, U8, U8, hc, norm_w.reshape(HEADS, 1, DK), C_all, n0g, m0g, *extra_ops)


def _lru_sample_kernel(x_ref, y_ref, h0_ref, cw_ref, cb_ref, wr_ref, br_ref, wi_ref, bi_ref, lam_ref,
                       o_ref, h_ref):
    c = _conv_groups(x_ref[...], cw_ref) + cb_ref[...]
    a, u = _lru_gates(c, wr_ref, br_ref, wi_ref, bi_ref, lam_ref)
    hs = h0_ref[...]
    pos = _group_pos(hs.shape)
    for t in range(GROUP - TOK0):
        hs = jnp.where(pos == TOK0 + t, a * pltpu.roll(hs, 1, axis=0) + u, hs)
    h_ref[...] = hs
    o_ref[...] = (hs * jax.nn.gelu(y_ref[...])).astype(o_ref.dtype)


def _lru_sample(U8, h0g, cw, cb, wr, br, wi, bi, lam, *, R=128):
    W = LRU_W
    rows = U8.shape[0]
    vec = lambda: pl.BlockSpec((1, W), lambda i: (0, 0))
    return pl.pallas_call(
        _lru_sample_kernel,
        grid=(rows // R,),
        in_specs=[pl.BlockSpec((R, W), lambda i: (i, CB_LX)),
                  pl.BlockSpec((R, W), lambda i: (i, CB_LY)),
                  pl.BlockSpec((R, W), lambda i: (i, 0)),
                  pl.BlockSpec((CONV_W, W), lambda i: (0, 0)), vec(),
                  pl.BlockSpec((W, W), lambda i: (0, 0)), vec(),
                  pl.BlockSpec((W, W), lambda i: (0, 0)), vec(), vec()],
        out_specs=[pl.BlockSpec((R, W), lambda i: (i, 0)), pl.BlockSpec((R, W), lambda i: (i, 0))],
        out_shape=[jax.ShapeDtypeStruct((rows, W), bf16), jax.ShapeDtypeStruct((rows, W), f32)],
        compiler_params=_cparams(("parallel",)),
        name="lru_sample",
    )(U8, U8, h0g, cw, cb.reshape(1, W), wr, br.reshape(1, W), wi, bi.reshape(1, W), lam.reshape(1, W))


def _permute_w_in(w_in):
    s = {}
    off = 0
    for name, size in (("qkv", CONV_CH), ("z", QK), ("gb", HEADS), ("ga", HEADS), ("mq", QK), ("mk", QK),
                       ("mv", QK), ("mo", QK), ("mi", HEADS), ("mf", HEADS), ("lx", LRU_W), ("ly", LRU_W)):
        s[name] = w_in[..., off:off + size]
        off += size
    pad = jnp.zeros(w_in.shape[:-1] + (DK - 4 * HEADS,), w_in.dtype)
    return jnp.concatenate([s["qkv"], s["z"], s["mq"], s["mk"], s["mv"], s["mo"], s["lx"], s["ly"],
                            s["gb"], s["ga"], s["mi"], s["mf"], pad], axis=-1)


def _block_diag(w):
    nb, d, e = w.shape
    eye = jnp.eye(nb, dtype=w.dtype)
    return (w[:, :, None, :] * eye[:, None, :, None]).reshape(nb * d, nb * e)


def _gate_consts(a_log, dt_bias, i_bias, f_bias):
    row = lambda v, k: jnp.pad(v, (k * HEADS, DK - (k + 1) * HEADS))
    rows = [row(a_log, 1), row(dt_bias, 1), row(i_bias, 2), row(f_bias, 3)]
    return jnp.stack(rows + [jnp.zeros((DK,), f32)] * (8 - len(rows)))


def _state_row(state):
    B, W = state.shape
    return jnp.pad(state[:, None, :], ((0, 0), (TOK0 - 1, GROUP - TOK0), (0, 0))).reshape(B * GROUP, W)


def _tail_rows(U, B, T, col, width):
    return jnp.stack([lax.slice(U, ((b + 1) * T - (CONV_W - 1), col), ((b + 1) * T, col + width))
                      for b in range(B)])


def kernel(x_prompt, x_sample, state_gdn_S, state_gdn_conv, state_mlstm_C, state_mlstm_n, state_mlstm_m, state_lru_h, state_lru_conv, ffn1_wg, ffn1_wu, ffn1_wd, ln_g, ln_b, w_in, gdn_conv_w, gdn_A_log, gdn_dt_bias, gdn_norm_w, mlstm_i_bias, mlstm_f_bias, mlstm_norm_w, lru_conv_w, lru_conv_b, lru_wr, lru_br, lru_wi, lru_bi, lru_lambda, w_out, ffn2_wg, ffn2_wu, ffn2_wd):
    BP, TP, D = x_prompt.shape
    BS, TS, _ = x_sample.shape
    MP, MS = BP * TP, BS * TS
    ffn1 = (ffn1_wg, ffn1_wu, ffn1_wd)
    ffn2 = (ffn2_wg, ffn2_wu, ffn2_wd)
    w_in_p = _permute_w_in(w_in.astype(bf16))
    w_out_b = w_out.astype(bf16)

    xp, xs = x_prompt.reshape(MP, D), x_sample.reshape(MS, D)
    outs_p = [[] for _ in range(7)]
    outs_s = [[] for _ in range(7)]
    gS_s = mC_s = None
    for d in range(DEPTH):
        x1p, x1s = _ffn_ln_both(xp, xs, ffn1, d, ln_g[d, 0], ln_b[d, 0])
        U = _inproj(x1p, w_in_p, d)
        hc = _gate_consts(gdn_A_log[d], gdn_dt_bias[d], mlstm_i_bias[d], mlstm_f_bias[d])
        wr = _block_diag(lru_wr[d]).astype(bf16)
        wi = _block_diag(lru_wi[d]).astype(bf16)
        lru_w = (lru_conv_w[d], lru_conv_b[d], wr, lru_br[d], wi, lru_bi[d], lru_lambda[d])

        og_p, gS_p = _gdn_prompt(U, BP, TP, gdn_conv_w[d], hc, gdn_norm_w[d])
        hm_p, mC_p, mn_p, mm_p = _mlstm_prompt(U, BP, TP, hc, mlstm_norm_w[d])
        ol_p, lh_p = _lru_prompt(U, BP, TP, *lru_w)
        outs_p[0].append(gS_p)
        outs_p[1].append(_tail_rows(U, BP, TP, 0, CONV_CH))
        outs_p[2].append(mC_p)
        outs_p[3].append(mn_p.reshape(BP, HEADS, DK))
        outs_p[4].append(mm_p[:, :, 0, 0])
        outs_p[5].append(lh_p.reshape(BP, LRU_W))
        outs_p[6].append(_tail_rows(U, BP, TP, COL_LX, LRU_W))

        Us = _inproj(x1s, w_in_p, d).reshape(BS, TS, D_INP)
        hist = jnp.concatenate([
            state_gdn_conv[d], jnp.zeros((BS, 3, COL_LX - CONV_CH), f32),
            state_lru_conv[d], jnp.zeros((BS, 3, D_INP - COL_LX - LRU_W), f32)], axis=2)
        U8 = jnp.concatenate([jnp.zeros((BS, 1, D_INP), f32), hist, Us], axis=1).reshape(BS * GROUP, D_INP)
        og_s, gS_s = _gdn_sample(U8, state_gdn_S, d, gS_s, gdn_conv_w[d], hc, gdn_norm_w[d])
        hm_s, mC_s, n8, m8 = _mlstm_sample(
            U8, state_mlstm_C, d, mC_s, _state_row(state_mlstm_n[d].reshape(BS, QK)),
            _state_row(jnp.repeat(state_mlstm_m[d], DK, axis=1)), hc, mlstm_norm_w[d])
        ol_s, h8 = _lru_sample(U8, _state_row(state_lru_h[d]), *lru_w)
        toks = lambda a: a.reshape(BS, GROUP, -1)[:, TOK0:].reshape(MS, -1)
        last = lambda a: a.reshape(BS, GROUP, -1)[:, GROUP - 1]
        outs_s[1].append(Us[:, TS - 3:, 0:CONV_CH])
        outs_s[3].append(last(n8).reshape(BS, HEADS, DK))
        outs_s[4].append(last(m8).reshape(BS, HEADS, DK)[:, :, 0])
        outs_s[5].append(last(h8))
        outs_s[6].append(Us[:, TS - 3:, COL_LX:COL_LX + LRU_W])

        x2p = _outproj_ln((og_p, hm_p, ol_p), x1p, w_out_b, d, ln_g[d, 1], ln_b[d, 1])
        x2s = _outproj_ln((toks(og_s), toks(hm_s), toks(ol_s)), x1s, w_out_b, d, ln_g[d, 1], ln_b[d, 1])
        xp, xs = _ffn_ln_both(x2p, x2s, ffn2, d, ln_g[d, 2], ln_b[d, 2])

    y_prompt = xp.reshape(BP, TP, D)
    y_sample = xs.reshape(BS, TS, D)
    stack = lambda o: jnp.stack(o)
    return (y_prompt, y_sample,
            *[stack(o) for o in outs_p],
            gS_s, stack(outs_s[1]), mC_s, *[stack(outs_s[i]) for i in range(3, 7)])
```

```python
import functools
import math

import jax
import jax.numpy as jnp
from jax import lax
from jax.experimental import pallas as pl
from jax.experimental.pallas import tpu as pltpu

f32 = jnp.float32
bf16 = jnp.bfloat16

DEPTH = 2
D_MODEL = 2048
D_FF = 5632
HEADS = 6
DK = 128
LRU_W = 512
LRU_BLOCKS = 8
LRU_C = 8.0
CONV_W = 4
CHUNK = 64
ALPHA = (2 * DEPTH) ** 0.25
NORM_EPS = 1e-6
NEG = -1e30

QK = HEADS * DK
CONV_CH = 3 * QK
MIX_W = 4 * QK
CB_Q, CB_K, CB_V, CB_Z = 0, 6, 12, 18
CB_MQ, CB_MK, CB_MV, CB_MO = 24, 30, 36, 42
CB_GDN, CB_MLSTM = 0, 1
CB_LX, CB_LY = 12, 13
CB_GATES = 56
COL_LX = 6144
D_INP = 7296
GROUP = 8
TOK0 = 4
ROW_TILE = 512
HEAD_ROW_BLOCKS = 2
INPROJ_TN = 2432
SAMPLE_SEQS_PER_STEP = 64

VMEM_LIMIT = 60 * 1024 * 1024


def _cparams(sem):
    return pltpu.CompilerParams(dimension_semantics=sem, vmem_limit_bytes=VMEM_LIMIT)


def _silu(x):
    return x * jax.nn.sigmoid(x)


def _softplus(x):
    return jnp.maximum(x, 0.0) + jnp.log1p(jnp.exp(-jnp.abs(x)))


def _expm1(x):
    u = jnp.exp(x)
    um1 = u - 1.0
    lg = jnp.where(u == 1.0, 1.0, jnp.log(jnp.where(u == 0.0, 1.0, u)))
    return jnp.where(u == 1.0, x, jnp.where(u == 0.0, -1.0, um1 * x / lg))


def _layernorm(y, g, b):
    mu = jnp.mean(y, -1, keepdims=True)
    d = y - mu
    var = jnp.mean(d * d, -1, keepdims=True)
    return d * lax.rsqrt(var + NORM_EPS) * g + b


def _rms(x):
    return x * lax.rsqrt(jnp.mean(x * x, -1, keepdims=True) + NORM_EPS)


def _l2(x):
    return x * lax.rsqrt(jnp.sum(x * x, -1, keepdims=True) + NORM_EPS)


def _colsel(gates, idx):
    lane = lax.broadcasted_iota(jnp.int32, gates.shape, 1)
    return jnp.sum(jnp.where(lane == idx, gates, 0.0), axis=1, keepdims=True)


def _dot(a, b):
    return jnp.dot(a, b, preferred_element_type=f32)


def _bmm(a, b):
    return jnp.einsum('nij,njk->nik', a, b, preferred_element_type=f32)


def _bmm_nt(a, b):
    return jnp.einsum('nid,njd->nij', a, b, preferred_element_type=f32)


def _bmm_tn(a, b):
    return jnp.einsum('nck,ncv->nkv', a, b, preferred_element_type=f32)


def _ij(C):
    return (lax.broadcasted_iota(jnp.int32, (C, C), 0), lax.broadcasted_iota(jnp.int32, (C, C), 1))


def _cumsum_row(col3, ii, jj):
    return jnp.sum(jnp.where((ii <= jj)[None], col3, 0.0), axis=1, keepdims=True)


def _row2col(row3, ii, jj):
    return jnp.sum(jnp.where((ii == jj)[None], row3, 0.0), axis=2, keepdims=True)


def _col2row(col3, ii, jj):
    return jnp.sum(jnp.where((ii == jj)[None], col3, 0.0), axis=1, keepdims=True)


def _conv_carry(x, w_ref, xp_ref, tail):
    T = x.shape[0]
    xp_ref[0:8, :] = tail
    xp_ref[8:T + 8, :] = x
    y = w_ref[3:4, :] * x
    for k in range(CONV_W - 1):
        y = y + w_ref[k:k + 1, :] * xp_ref[pl.ds(5 + k, T), :]
    return y


def _conv_groups(x, w_ref):
    y = w_ref[3:4, :] * x
    for s in range(1, CONV_W):
        y = y + w_ref[3 - s:4 - s, :] * pltpu.roll(x, s, axis=0)
    return y


def _group_pos(shape):
    return lax.broadcasted_iota(jnp.int32, shape, 0) & (GROUP - 1)


def _ffn_ln_kernel(*refs, cast_weights, aliased):
    x_ref, wg_ref, wu_ref, wd_ref, g_ref, b_ref = refs[:6]
    n_in = 7 if aliased else 6
    o_ref = refs[n_in]
    xb_ref = refs[-1]
    j = pl.program_id(1)

    @pl.when(j == 0)
    def _():
        o_ref[...] = jnp.zeros_like(o_ref)
        xb_ref[...] = x_ref[...].astype(bf16)

    if cast_weights:
        wg, wu, wd = (r[...].astype(bf16) for r in (wg_ref, wu_ref, wd_ref))
        for w_out_ref, w in zip(refs[n_in + 1:n_in + 4], (wg, wu, wd)):
            w_out_ref[...] = w
    else:
        wg, wu, wd = wg_ref[...], wu_ref[...], wd_ref[...]
    xb = xb_ref[...]
    h = (_silu(_dot(xb, wg)) * _dot(xb, wu)).astype(bf16)
    o_ref[...] += _dot(h, wd)

    @pl.when(j == pl.num_programs(1) - 1)
    def _():
        y = ALPHA * x_ref[...] + 0.5 * o_ref[...]
        o_ref[...] = _layernorm(y, g_ref[...], b_ref[...])


def _ffn_ln(x, wg, wu, wd, g, b, *, tm, tf=512, row0=0, prev=None):
    M, D = x.shape
    F = wg.shape[1]
    rows = lambda i, j: (i + row0, 0)
    in_specs = [
        pl.BlockSpec((tm, D), rows),
        pl.BlockSpec((D, tf), lambda i, j: (0, j)),
        pl.BlockSpec((D, tf), lambda i, j: (0, j)),
        pl.BlockSpec((tf, D), lambda i, j: (j, 0)),
        pl.BlockSpec((1, D), lambda i, j: (0, 0)),
        pl.BlockSpec((1, D), lambda i, j: (0, 0)),
    ]
    operands = [x, wg, wu, wd, g.reshape(1, D), b.reshape(1, D)]
    if prev is not None:
        in_specs.append(pl.BlockSpec(memory_space=pl.ANY))
        operands.append(prev)
    return pl.pallas_call(
        functools.partial(_ffn_ln_kernel, cast_weights=False, aliased=prev is not None),
        grid=(M // tm - row0, F // tf),
        in_specs=in_specs,
        out_specs=pl.BlockSpec((tm, D), rows),
        out_shape=jax.ShapeDtypeStruct((M, D), f32),
        input_output_aliases={6: 0} if prev is not None else {},
        scratch_shapes=[pltpu.VMEM((tm, D), bf16)],
        compiler_params=_cparams(("parallel", "arbitrary")),
        name="ffn_ln",
    )(*operands)


def _ffn_ln_head(x, wg, wu, wd, d, g, b, *, tm, tf=256):
    M, D = x.shape
    F = wg.shape[2]
    return pl.pallas_call(
        functools.partial(_ffn_ln_kernel, cast_weights=True, aliased=False),
        grid=(1, F // tf),
        in_specs=[
            pl.BlockSpec((tm, D), lambda i, j: (0, 0), pipeline_mode=pl.Buffered(1)),
            pl.BlockSpec((None, D, tf), lambda i, j: (d, 0, j)),
            pl.BlockSpec((None, D, tf), lambda i, j: (d, 0, j)),
            pl.BlockSpec((None, tf, D), lambda i, j: (d, j, 0)),
            pl.BlockSpec((1, D), lambda i, j: (0, 0)),
            pl.BlockSpec((1, D), lambda i, j: (0, 0)),
        ],
        out_specs=[pl.BlockSpec((tm, D), lambda i, j: (0, 0)),
                   pl.BlockSpec((D, tf), lambda i, j: (0, j)),
                   pl.BlockSpec((D, tf), lambda i, j: (0, j)),
                   pl.BlockSpec((tf, D), lambda i, j: (j, 0))],
        out_shape=[jax.ShapeDtypeStruct((M, D), f32), jax.ShapeDtypeStruct((D, F), bf16),
                   jax.ShapeDtypeStruct((D, F), bf16), jax.ShapeDtypeStruct((F, D), bf16)],
        scratch_shapes=[pltpu.VMEM((tm, D), bf16)],
        compiler_params=_cparams(("arbitrary", "arbitrary")),
        name="ffn_ln_head",
    )(x, wg, wu, wd, g.reshape(1, D), b.reshape(1, D))


def _ffn_ln_both(xp, xs, w32, d, g, b):
    out, *wb = _ffn_ln_head(xp, *w32, d, g, b, tm=HEAD_ROW_BLOCKS * ROW_TILE)
    yp = _ffn_ln(xp, *wb, g, b, tm=ROW_TILE, row0=HEAD_ROW_BLOCKS, prev=out)
    ys = _ffn_ln(xs, *wb, g, b, tm=ROW_TILE)
    return yp, ys


def _inproj_kernel(x_ref, w0_ref, w_ref, o_ref):
    n = pl.program_id(0)
    xb = x_ref[...].astype(bf16)

    @pl.when(n == 0)
    def _():
        o_ref[...] = _dot(xb, w0_ref[...])

    @pl.when(n > 0)
    def _():
        o_ref[...] = _dot(xb, w_ref[...])


def _inproj(x, w_first, w_rest, d, *, tm=ROW_TILE, tn=INPROJ_TN):
    M, D = x.shape
    N = tn + w_rest.shape[2]
    return pl.pallas_call(
        _inproj_kernel,
        grid=(N // tn, M // tm),
        in_specs=[pl.BlockSpec((tm, D), lambda n, i: (i, 0)),
                  pl.BlockSpec((None, D, tn), lambda n, i: (d, 0, 0), pipeline_mode=pl.Buffered(1)),
                  pl.BlockSpec((None, D, tn), lambda n, i: (d, 0, jnp.maximum(n - 1, 0)))],
        out_specs=pl.BlockSpec((tm, tn), lambda n, i: (i, n)),
        out_shape=jax.ShapeDtypeStruct((M, N), f32),
        compiler_params=_cparams(("arbitrary", "arbitrary")),
        name="in_proj",
    )(x, w_first, w_rest)


def _outproj_ln_kernel(og_ref, hm_ref, ol_ref, x_ref, w_ref, g_ref, b_ref, o_ref):
    mix = (_dot(og_ref[...], w_ref[0:QK, :]) + _dot(hm_ref[...], w_ref[QK:2 * QK, :])
           + _dot(ol_ref[...], w_ref[2 * QK:, :]))
    o_ref[...] = _layernorm(ALPHA * x_ref[...] + mix, g_ref[...], b_ref[...])


def _outproj_ln(mix, x, w, d, g, b, *, tm=ROW_TILE):
    M, D = x.shape
    return pl.pallas_call(
        _outproj_ln_kernel,
        grid=(M // tm,),
        in_specs=[pl.BlockSpec((tm, wd), lambda i: (i, 0)) for wd in (QK, QK, LRU_W)]
        + [pl.BlockSpec((tm, D), lambda i: (i, 0)),
           pl.BlockSpec((None, D, D), lambda i: (d, 0, 0)),
           pl.BlockSpec((1, D), lambda i: (0, 0)),
           pl.BlockSpec((1, D), lambda i: (0, 0))],
        out_specs=pl.BlockSpec((tm, D), lambda i: (i, 0)),
        out_shape=jax.ShapeDtypeStruct((M, D), f32),
        compiler_params=_cparams(("parallel",)),
        name="out_proj_ln",
    )(*mix, x, w, g.reshape(1, D), b.reshape(1, D))


def _gdn_gates(gates, gc_ref):
    beta = jax.nn.sigmoid(gates)
    gl = -jnp.exp(gc_ref[0:1, :]) * _softplus(gates + gc_ref[1:2, :])
    return beta, gl


def _mlstm_gates(gates, gc_ref):
    i_pre = gates + gc_ref[2:3, :]
    logf = -_softplus(-(gates + gc_ref[3:4, :]))
    return i_pre, logf


def _gdn_chunk(q3, k3, v3, b3, g3, S):
    C = q3.shape[1]
    ii, jj = _ij(C)
    g_row = _cumsum_row(g3, ii, jj)
    g_col = _row2col(g_row, ii, jj)
    decay = jnp.exp(jnp.where((ii >= jj)[None], g_col - g_row, NEG))
    kb = k3.astype(bf16)
    kk = _bmm_nt(kb, kb)
    p = jnp.where((ii > jj)[None], -(b3 * kk * decay), 0.0)
    e = p
    for _ in range(int(math.log2(C)) - 1):
        pb = p.astype(bf16)
        p = _bmm(pb, pb)
        e = e + p + _bmm(e.astype(bf16), p.astype(bf16))
    eb = e.astype(bf16)
    e_g = jnp.exp(g_col)
    rv = b3 * v3
    rk = (b3 * e_g) * k3
    uv = rv + _bmm(eb, rv.astype(bf16))
    wks = (rk + _bmm(eb, rk.astype(bf16))).astype(bf16)
    qk = (_bmm_nt(q3.astype(bf16), kb) * decay).astype(bf16)
    g_last = g_row[:, :, C - 1:C]
    qd = (q3 * e_g).astype(bf16)
    kt = (k3 * jnp.exp(g_last - g_col)).astype(bf16)
    Sb = S.astype(bf16)
    Ub = (uv - _bmm(wks, Sb)).astype(bf16)
    o = _bmm(qd, Sb) + _bmm(qk, Ub)
    return o, jnp.exp(g_last) * S + _bmm_tn(kt, Ub)


def _mlstm_chunk(q3, k3, vb, i3, f3, Cm, nr, m):
    C = q3.shape[1]
    ii, jj = _ij(C)
    b_row = _cumsum_row(f3, ii, jj)
    b_col = _row2col(b_row, ii, jj)
    i_row = _col2row(i3, ii, jj)
    D = jnp.where((ii >= jj)[None], b_col - b_row + i_row, NEG)
    d_max = jnp.max(D, axis=2, keepdims=True)
    qb = q3.astype(bf16)
    qk = _bmm_nt(qb, k3.astype(bf16))
    m_t = jnp.maximum(b_col + m, d_max)
    inter = jnp.exp(b_col + m - m_t)
    Sw = jnp.exp(D - m_t) * qk
    num = inter * _bmm(qb, Cm.astype(bf16)) + _bmm(Sw.astype(bf16), vb)
    den = inter * jnp.sum(q3 * nr, -1, keepdims=True) + jnp.sum(Sw, -1, keepdims=True)
    hh = num / jnp.maximum(jnp.abs(den), jnp.exp(-m_t))
    m_new = m_t[:, C - 1:C]
    b_last = b_col[:, C - 1:C]
    kw = jnp.exp(b_last - b_col + i3 - m_new) * k3
    dec = jnp.exp(b_last + m - m_new)
    return (hh, dec * Cm + _bmm_tn(kw.astype(bf16), vb), dec * nr + jnp.sum(kw, axis=1, keepdims=True), m_new)


def _gdn_prompt_kernel(*refs, C, B):
    mains, gts = refs[0:2 * B:2], refs[1:2 * B:2]
    cw_ref, gc_ref, nw_ref, o_ref, s_ref, xp_ref, tail_ref = refs[2 * B:]
    n = pl.program_id(0)
    NC = B * HEADS

    @pl.when(n == 0)
    def _():
        tail_ref[...] = jnp.zeros_like(tail_ref)
        s_ref[...] = jnp.zeros_like(s_ref)

    qs, ks, vs, zs, bs, gs = [], [], [], [], [], []
    for b in range(B):
        x = mains[b][:, 0:CONV_CH]
        y = _silu(_conv_carry(x, cw_ref, xp_ref, tail_ref[b]))
        tail_ref[b] = x[C - 8:C]
        beta, gl = _gdn_gates(gts[b][...], gc_ref)
        for h in range(HEADS):
            qs.append(y[:, h * DK:(h + 1) * DK])
            ks.append(y[:, QK + h * DK:QK + (h + 1) * DK])
            vs.append(y[:, 2 * QK + h * DK:2 * QK + (h + 1) * DK])
            zs.append(mains[b][:, CONV_CH + h * DK:CONV_CH + (h + 1) * DK])
            bs.append(beta[:, h:h + 1])
            gs.append(gl[:, HEADS + h:HEADS + h + 1])
    q3 = _l2(jnp.stack(qs)) * (DK ** -0.5)
    k3 = _l2(jnp.stack(ks))
    v3 = jnp.stack(vs)
    b3 = jnp.stack(bs)
    g3 = jnp.stack(gs)
    o, S = _gdn_chunk(q3, k3, v3, b3, g3, s_ref[...].reshape(NC, DK, DK))
    s_ref[...] = S.reshape(B, HEADS, DK, DK)
    o = _rms(o) * nw_ref[...] * _silu(jnp.stack(zs))
    for b in range(B):
        for h in range(HEADS):
            o_ref[b, :, h * DK:(h + 1) * DK] = o[b * HEADS + h].astype(o_ref.dtype)


def _gdn_prompt(U, B, T, conv_w, hc, norm_w, *, C=CHUNK):
    N = T // C
    in_specs = []
    for b in range(B):
        in_specs.append(pl.BlockSpec((C, MIX_W), lambda n, b=b: (b * N + n, CB_GDN)))
        in_specs.append(pl.BlockSpec((C, DK), lambda n, b=b: (b * N + n, CB_GATES)))
    in_specs += [pl.BlockSpec((CONV_W, CONV_CH), lambda n: (0, 0)),
                 pl.BlockSpec((8, DK), lambda n: (0, 0)),
                 pl.BlockSpec((1, DK), lambda n: (0, 0))]
    o, S = pl.pallas_call(
        functools.partial(_gdn_prompt_kernel, C=C, B=B),
        grid=(N,),
        in_specs=in_specs,
        out_specs=[pl.BlockSpec((B, None, C, QK), lambda n: (0, n, 0, 0)),
                   pl.BlockSpec((B, HEADS, DK, DK), lambda n: (0, 0, 0, 0))],
        out_shape=[jax.ShapeDtypeStruct((B, N, C, QK), bf16),
                   jax.ShapeDtypeStruct((B, HEADS, DK, DK), f32)],
        scratch_shapes=[pltpu.VMEM((C + 8, CONV_CH), f32), pltpu.VMEM((B, 8, CONV_CH), f32)],
        compiler_params=_cparams(("arbitrary",)),
        name="gdn_prompt",
    )(*([U, U] * B), conv_w, hc, norm_w.reshape(1, DK))
    return o.reshape(B * T, QK), S


def _mlstm_prompt_kernel(*refs, C, B):
    mains, gts = refs[0:2 * B:2], refs[1:2 * B:2]
    gc_ref, nw_ref, h_ref, c_ref, n_ref, m_ref = refs[2 * B:]
    n = pl.program_id(0)
    NC = B * HEADS

    @pl.when(n == 0)
    def _():
        c_ref[...] = jnp.zeros_like(c_ref)
        n_ref[...] = jnp.zeros_like(n_ref)
        m_ref[...] = jnp.zeros_like(m_ref)

    qs, ks, vs, os_, is_, fs = [], [], [], [], [], []
    for b in range(B):
        i_pre, logf = _mlstm_gates(gts[b][...], gc_ref)
        for h in range(HEADS):
            qs.append(mains[b][:, h * DK:(h + 1) * DK])
            ks.append(mains[b][:, QK + h * DK:QK + (h + 1) * DK])
            vs.append(mains[b][:, 2 * QK + h * DK:2 * QK + (h + 1) * DK])
            os_.append(mains[b][:, 3 * QK + h * DK:3 * QK + (h + 1) * DK])
            is_.append(i_pre[:, 2 * HEADS + h:2 * HEADS + h + 1])
            fs.append(logf[:, 3 * HEADS + h:3 * HEADS + h + 1])
    q3 = jnp.stack(qs)
    k3 = jnp.stack(ks) * (DK ** -0.5)
    vb = jnp.stack(vs).astype(bf16)
    i3 = jnp.stack(is_)
    f3 = jnp.stack(fs)
    hh, Cm, nr, m = _mlstm_chunk(q3, k3, vb, i3, f3, c_ref[...].reshape(NC, DK, DK),
                                 n_ref[...].reshape(NC, 1, DK), m_ref[...].reshape(NC, 1, DK)[:, :, 0:1])
    nw = jnp.stack([nw_ref[h] for _ in range(B) for h in range(HEADS)])
    hh = _rms(hh) * nw * jax.nn.sigmoid(jnp.stack(os_))
    c_ref[...] = Cm.reshape(B, HEADS, DK, DK)
    n_ref[...] = nr.reshape(B, HEADS, 1, DK)
    m_ref[...] = jnp.broadcast_to(m, (NC, 1, DK)).reshape(B, HEADS, 1, DK)
    for b in range(B):
        for h in range(HEADS):
            h_ref[b, :, h * DK:(h + 1) * DK] = hh[b * HEADS + h].astype(h_ref.dtype)


def _mlstm_prompt(U, B, T, hc, norm_w, *, C=CHUNK):
    N = T // C
    in_specs = []
    for b in range(B):
        in_specs.append(pl.BlockSpec((C, MIX_W), lambda n, b=b: (b * N + n, CB_MLSTM)))
        in_specs.append(pl.BlockSpec((C, DK), lambda n, b=b: (b * N + n, CB_GATES)))
    in_specs += [pl.BlockSpec((8, DK), lambda n: (0, 0)),
                 pl.BlockSpec((HEADS, 1, DK), lambda n: (0, 0, 0))]
    st = lambda r: pl.BlockSpec((B, HEADS, r, DK), lambda n: (0, 0, 0, 0))
    h, Cm, nr, m = pl.pallas_call(
        functools.partial(_mlstm_prompt_kernel, C=C, B=B),
        grid=(N,),
        in_specs=in_specs,
        out_specs=[pl.BlockSpec((B, None, C, QK), lambda n: (0, n, 0, 0)), st(DK), st(1), st(1)],
        out_shape=[jax.ShapeDtypeStruct((B, N, C, QK), bf16),
                   jax.ShapeDtypeStruct((B, HEADS, DK, DK), f32),
                   jax.ShapeDtypeStruct((B, HEADS, 1, DK), f32),
                   jax.ShapeDtypeStruct((B, HEADS, 1, DK), f32)],
        compiler_params=_cparams(("arbitrary",)),
        name="mlstm_prompt",
    )(*([U, U] * B), hc, norm_w.reshape(HEADS, 1, DK))
    return h.reshape(B * T, QK), Cm, nr, m


def _lru_gates(c, wr_ref, br_ref, wi_ref, bi_ref, lam_ref):
    cb = c.astype(bf16)
    r = jax.nn.sigmoid(_dot(cb, wr_ref[...]) + br_ref[...])
    ig = jax.nn.sigmoid(_dot(cb, wi_ref[...]) + bi_ref[...])
    log_a = -LRU_C * r * _softplus(-lam_ref[...])
    a = jnp.exp(log_a)
    u = jnp.sqrt(-_expm1(2.0 * log_a)) * (ig * c)
    return a, u


def _lru_prompt_kernel(x_ref, y_ref, cw_ref, cb_ref, wr_ref, br_ref, wi_ref, bi_ref, lam_ref,
                       o_ref, hl_ref, xp_ref, tail_ref, hc_ref):
    tc = pl.program_id(1)
    Tc, W = x_ref.shape

    @pl.when(tc == 0)
    def _():
        tail_ref[...] = jnp.zeros_like(tail_ref)
        hc_ref[...] = jnp.zeros_like(hc_ref)

    x = x_ref[...]
    c = _conv_carry(x, cw_ref, xp_ref, tail_ref[...]) + cb_ref[...]
    tail_ref[...] = x[Tc - 8:Tc]
    a, u = _lru_gates(c, wr_ref, br_ref, wi_ref, bi_ref, lam_ref)
    row = lax.broadcasted_iota(jnp.int32, (Tc, W), 0)
    d = 1
    while d < Tc:
        keep = row >= d
        a_sh = jnp.where(keep, pltpu.roll(a, d, axis=0), 1.0)
        u_sh = jnp.where(keep, pltpu.roll(u, d, axis=0), 0.0)
        u = a * u_sh + u
        a = a * a_sh
        d *= 2
    hs = u + a * hc_ref[0:1, :]
    last = hs[Tc - 1:Tc]
    hc_ref[...] = jnp.broadcast_to(last, hc_ref.shape)
    hl_ref[...] = last
    o_ref[...] = (hs * jax.nn.gelu(y_ref[...])).astype(o_ref.dtype)


def _lru_prompt(U, B, T, cw, cb, wr, br, wi, bi, lam, *, Tc=512):
    W = LRU_W
    nt = T // Tc
    vec = lambda: pl.BlockSpec((1, W), lambda b, t: (0, 0))
    return pl.pallas_call(
        _lru_prompt_kernel,
        grid=(B, nt),
        in_specs=[pl.BlockSpec((Tc, W), lambda b, t: (b * nt + t, CB_LX)),
                  pl.BlockSpec((Tc, W), lambda b, t: (b * nt + t, CB_LY)),
                  pl.BlockSpec((CONV_W, W), lambda b, t: (0, 0)), vec(),
                  pl.BlockSpec((W, W), lambda b, t: (0, 0)), vec(),
                  pl.BlockSpec((W, W), lambda b, t: (0, 0)), vec(), vec()],
        out_specs=[pl.BlockSpec((Tc, W), lambda b, t: (b * nt + t, 0)),
                   pl.BlockSpec((None, 1, W), lambda b, t: (b, 0, 0))],
        out_shape=[jax.ShapeDtypeStruct((B * T, W), bf16),
                   jax.ShapeDtypeStruct((B, 1, W), f32)],
        scratch_shapes=[pltpu.VMEM((Tc + 8, W), f32), pltpu.VMEM((8, W), f32), pltpu.VMEM((8, W), f32)],
        compiler_params=_cparams(("parallel", "arbitrary")),
        name="lru_prompt",
    )(U, U, cw, cb.reshape(1, W), wr, br.reshape(1, W), wi, bi.reshape(1, W), lam.reshape(1, W))


def _gdn_sample_kernel(q_ref, k_ref, v_ref, z_ref, gt_ref, wq_ref, wk_ref, wv_ref, gc_ref, nw_ref, s0_ref,
                       *rest):
    o_ref, s_ref = rest[-2:]
    h = pl.program_id(0)
    BB = s0_ref.shape[0]
    R = BB * GROUP
    q = _l2(_silu(_conv_groups(q_ref[...], wq_ref))) * (DK ** -0.5)
    k = _l2(_silu(_conv_groups(k_ref[...], wk_ref)))
    v = _silu(_conv_groups(v_ref[...], wv_ref))
    beta, gl = _gdn_gates(gt_ref[...], gc_ref)
    is_tok = _group_pos((R, 1)) >= TOK0
    beta = jnp.where(is_tok, _colsel(beta, h), 0.0)
    gl = jnp.where(is_tok, _colsel(gl, HEADS + h), 0.0)
    grp = lambda a: a.reshape(BB, GROUP, a.shape[-1])
    o, S = _gdn_chunk(grp(q), grp(k), grp(v), grp(beta), grp(gl), s0_ref[...])
    s_ref[...] = S
    o_ref[...] = (_rms(o.reshape(R, DK)) * nw_ref[...] * _silu(z_ref[...])).astype(o_ref.dtype)


def _stacked_out(prev):
    if prev is None:
        return [], [], {}
    return [pl.BlockSpec(memory_space=pl.ANY)], [prev], None


def _gdn_sample(U8, S_all, d, S_prev, conv_w, hc, norm_w, *, BB=SAMPLE_SEQS_PER_STEP):
    B = S_all.shape[1]
    R = BB * GROUP
    tok = lambda cb: pl.BlockSpec((R, DK), lambda h, b: (b, cb + h))
    cw = lambda cb: pl.BlockSpec((CONV_W, DK), lambda h, b: (0, cb + h))
    st = pl.BlockSpec((None, BB, None, DK, DK), lambda h, b: (d, b, h, 0, 0))
    extra_specs, extra_ops, _ = _stacked_out(S_prev)
    n_in = 11
    return pl.pallas_call(
        _gdn_sample_kernel,
        grid=(HEADS, B // BB),
        in_specs=[tok(CB_Q), tok(CB_K), tok(CB_V), tok(CB_Z),
                  pl.BlockSpec((R, DK), lambda h, b: (b, CB_GATES)),
                  cw(CB_Q), cw(CB_K), cw(CB_V),
                  pl.BlockSpec((8, DK), lambda h, b: (0, 0)),
                  pl.BlockSpec((1, DK), lambda h, b: (0, 0)), st] + extra_specs,
        out_specs=[pl.BlockSpec((R, DK), lambda h, b: (b, h)), st],
        out_shape=[jax.ShapeDtypeStruct((B * GROUP, QK), bf16),
                   jax.ShapeDtypeStruct(S_all.shape, f32)],
        input_output_aliases={n_in: 1} if S_prev is not None else {},
        compiler_params=_cparams(("parallel", "parallel")),
        name="gdn_sample",
    )(U8, U8, U8, U8, U8, conv_w, conv_w, conv_w, hc, norm_w.reshape(1, DK), S_all, *extra_ops)


def _mlstm_sample_kernel(q_ref, k_ref, v_ref, og_ref, gt_ref, gc_ref, nw_ref, c0_ref, n0_ref, m0_ref,
                         *rest):
    h_ref, c_ref, n_ref, m_ref = rest[-4:]
    h = pl.program_id(0)
    BB = c0_ref.shape[0]
    R = BB * GROUP
    i_pre, logf = _mlstm_gates(gt_ref[...], gc_ref)
    is_tok = _group_pos((R, 1)) >= TOK0
    i_pre = jnp.where(is_tok, _colsel(i_pre, 2 * HEADS + h), NEG)
    logf = jnp.where(is_tok, _colsel(logf, 3 * HEADS + h), 0.0)
    grp = lambda a: a.reshape(BB, GROUP, a.shape[-1])
    old = lambda ref: grp(ref[...])[:, TOK0 - 1:TOK0, :]
    hh, Cm, nr, m = _mlstm_chunk(grp(q_ref[...]), grp(k_ref[...] * (DK ** -0.5)), grp(v_ref[...]).astype(bf16),
                                 grp(i_pre), grp(logf), c0_ref[...], old(n0_ref), old(m0_ref)[:, :, 0:1])
    c_ref[...] = Cm
    n_ref[...] = jnp.broadcast_to(nr, (BB, GROUP, DK)).reshape(R, DK)
    m_ref[...] = jnp.broadcast_to(m, (BB, GROUP, DK)).reshape(R, DK)
    h_ref[...] = (_rms(hh.reshape(R, DK)) * nw_ref[h] * jax.nn.sigmoid(og_ref[...])).astype(h_ref.dtype)


def _mlstm_sample(U8, C_all, d, C_prev, n0g, m0g, hc, norm_w, *, BB=SAMPLE_SEQS_PER_STEP):
    B = C_all.shape[1]
    R = BB * GROUP
    tok = lambda cb: pl.BlockSpec((R, DK), lambda h, b: (b, cb + h))
    row = pl.BlockSpec((R, DK), lambda h, b: (b, h))
    st = pl.BlockSpec((None, BB, None, DK, DK), lambda h, b: (d, b, h, 0, 0))
    extra_specs, extra_ops, _ = _stacked_out(C_prev)
    n_in = 10
    return pl.pallas_call(
        _mlstm_sample_kernel,
        grid=(HEADS, B // BB),
        in_specs=[tok(CB_MQ), tok(CB_MK), tok(CB_MV), tok(CB_MO),
                  pl.BlockSpec((R, DK), lambda h, b: (b, CB_GATES)),
                  pl.BlockSpec((8, DK), lambda h, b: (0, 0)),
                  pl.BlockSpec((HEADS, 1, DK), lambda h, b: (0, 0, 0)),
                  st, row, row] + extra_specs,
        out_specs=[row, st, row, row],
        out_shape=[jax.ShapeDtypeStruct((B * GROUP, QK), bf16),
                   jax.ShapeDtypeStruct(C_all.shape, f32),
                   jax.ShapeDtypeStruct((B * GROUP, QK), f32),
                   jax.ShapeDtypeStruct((B * GROUP, QK), f32)],
        input_output_aliases={n_in: 1} if C_prev is not None else {},
        compiler_params=_cparams(("parallel", "parallel")),
        name="mlstm_sample",
    )(U8, U8, U8, U8, U8, hc, norm_w.reshape(HEADS, 1, DK), C_all, n0g, m0g, *extra_ops)


def _lru_sample_kernel(x_ref, y_ref, h0_ref, cw_ref, cb_ref, wr_ref, br_ref, wi_ref, bi_ref, lam_ref,
                       o_ref, h_ref):
    c = _conv_groups(x_ref[...], cw_ref) + cb_ref[...]
    a, u = _lru_gates(c, wr_ref, br_ref, wi_ref, bi_ref, lam_ref)
    hs = h0_ref[...]
    pos = _group_pos(hs.shape)
    for t in range(GROUP - TOK0):
        hs = jnp.where(pos == TOK0 + t, a * pltpu.roll(hs, 1, axis=0) + u, hs)
    h_ref[...] = hs
    o_ref[...] = (hs * jax.nn.gelu(y_ref[...])).astype(o_ref.dtype)


def _lru_sample(U8, h0g, cw, cb, wr, br, wi, bi, lam, *, R=128):
    W = LRU_W
    rows = U8.shape[0]
    vec = lambda: pl.BlockSpec((1, W), lambda i: (0, 0))
    return pl.pallas_call(
        _lru_sample_kernel,
        grid=(rows // R,),
        in_specs=[pl.BlockSpec((R, W), lambda i: (i, CB_LX)),
                  pl.BlockSpec((R, W), lambda i: (i, CB_LY)),
                  pl.BlockSpec((R, W), lambda i: (i, 0)),
                  pl.BlockSpec((CONV_W, W), lambda i: (0, 0)), vec(),
                  pl.BlockSpec((W, W), lambda i: (0, 0)), vec(),
                  pl.BlockSpec((W, W), lambda i: (0, 0)), vec(), vec()],
        out_specs=[pl.BlockSpec((R, W), lambda i: (i, 0)), pl.BlockSpec((R, W), lambda i: (i, 0))],
        out_shape=[jax.ShapeDtypeStruct((rows, W), bf16), jax.ShapeDtypeStruct((rows, W), f32)],
        compiler_params=_cparams(("parallel",)),
        name="lru_sample",
    )(U8, U8, h0g, cw, cb.reshape(1, W), wr, br.reshape(1, W), wi, bi.reshape(1, W), lam.reshape(1, W))


def _permute_w_in(w_in):
    s = {}
    off = 0
    for name, size in (("qkv", CONV_CH), ("z", QK), ("gb", HEADS), ("ga", HEADS), ("mq", QK), ("mk", QK),
                       ("mv", QK), ("mo", QK), ("mi", HEADS), ("mf", HEADS), ("lx", LRU_W), ("ly", LRU_W)):
        s[name] = w_in[..., off:off + size]
        off += size
    pad = jnp.zeros(w_in.shape[:-1] + (DK - 4 * HEADS,), w_in.dtype)
    skip = INPROJ_TN - CONV_CH
    return jnp.concatenate([s["z"][..., skip:], s["mq"], s["mk"], s["mv"], s["mo"], s["lx"], s["ly"],
                            s["gb"], s["ga"], s["mi"], s["mf"], pad], axis=-1)


def _block_diag(w):
    nb, d, e = w.shape
    eye = jnp.eye(nb, dtype=w.dtype)
    return (w[:, :, None, :] * eye[:, None, :, None]).reshape(nb * d, nb * e)


def _gate_consts(a_log, dt_bias, i_bias, f_bias):
    row = lambda v, k: jnp.pad(v, (k * HEADS, DK - (k + 1) * HEADS))
    rows = [row(a_log, 1), row(dt_bias, 1), row(i_bias, 2), row(f_bias, 3)]
    return jnp.stack(rows + [jnp.zeros((DK,), f32)] * (8 - len(rows)))


def _state_row(state):
    B, W = state.shape
    return jnp.pad(state[:, None, :], ((0, 0), (TOK0 - 1, GROUP - TOK0), (0, 0))).reshape(B * GROUP, W)


def _tail_rows(U, B, T, col, width):
    return jnp.stack([lax.slice(U, ((b + 1) * T - (CONV_W - 1), col), ((b + 1) * T, col + width))
                      for b in range(B)])


def kernel(x_prompt, x_sample, state_gdn_S, state_gdn_conv, state_mlstm_C, state_mlstm_n, state_mlstm_m, state_lru_h, state_lru_conv, ffn1_wg, ffn1_wu, ffn1_wd, ln_g, ln_b, w_in, gdn_conv_w, gdn_A_log, gdn_dt_bias, gdn_norm_w, mlstm_i_bias, mlstm_f_bias, mlstm_norm_w, lru_conv_w, lru_conv_b, lru_wr, lru_br, lru_wi, lru_bi, lru_lambda, w_out, ffn2_wg, ffn2_wu, ffn2_wd):
    BP, TP, D = x_prompt.shape
    BS, TS, _ = x_sample.shape
    MP, MS = BP * TP, BS * TS
    ffn1 = (ffn1_wg, ffn1_wu, ffn1_wd)
    ffn2 = (ffn2_wg, ffn2_wu, ffn2_wd)
    w_in_b = w_in.astype(bf16)
    w_in_p = _permute_w_in(w_in_b)
    w_out_b = w_out.astype(bf16)

    xp, xs = x_prompt.reshape(MP, D), x_sample.reshape(MS, D)
    outs_p = [[] for _ in range(7)]
    outs_s = [[] for _ in range(7)]
    gS_s = mC_s = None
    for d in range(DEPTH):
        x1p, x1s = _ffn_ln_both(xp, xs, ffn1, d, ln_g[d, 0], ln_b[d, 0])
        U = _inproj(x1p, w_in_b, w_in_p, d)
        hc = _gate_consts(gdn_A_log[d], gdn_dt_bias[d], mlstm_i_bias[d], mlstm_f_bias[d])
        wr = _block_diag(lru_wr[d]).astype(bf16)
        wi = _block_diag(lru_wi[d]).astype(bf16)
        lru_w = (lru_conv_w[d], lru_conv_b[d], wr, lru_br[d], wi, lru_bi[d], lru_lambda[d])

        og_p, gS_p = _gdn_prompt(U, BP, TP, gdn_conv_w[d], hc, gdn_norm_w[d])
        hm_p, mC_p, mn_p, mm_p = _mlstm_prompt(U, BP, TP, hc, mlstm_norm_w[d])
        ol_p, lh_p = _lru_prompt(U, BP, TP, *lru_w)
        outs_p[0].append(gS_p)
        outs_p[1].append(_tail_rows(U, BP, TP, 0, CONV_CH))
        outs_p[2].append(mC_p)
        outs_p[3].append(mn_p.reshape(BP, HEADS, DK))
        outs_p[4].append(mm_p[:, :, 0, 0])
        outs_p[5].append(lh_p.reshape(BP, LRU_W))
        outs_p[6].append(_tail_rows(U, BP, TP, COL_LX, LRU_W))

        Us = _inproj(x1s, w_in_b, w_in_p, d).reshape(BS, TS, D_INP)
        hist = jnp.concatenate([
            state_gdn_conv[d], jnp.zeros((BS, 3, COL_LX - CONV_CH), f32),
            state_lru_conv[d], jnp.zeros((BS, 3, D_INP - COL_LX - LRU_W), f32)], axis=2)
        U8 = jnp.concatenate([jnp.zeros((BS, 1, D_INP), f32), hist, Us], axis=1).reshape(BS * GROUP, D_INP)
        og_s, gS_s = _gdn_sample(U8, state_gdn_S, d, gS_s, gdn_conv_w[d], hc, gdn_norm_w[d])
        hm_s, mC_s, n8, m8 = _mlstm_sample(
            U8, state_mlstm_C, d, mC_s, _state_row(state_mlstm_n[d].reshape(BS, QK)),
            _state_row(jnp.repeat(state_mlstm_m[d], DK, axis=1)), hc, mlstm_norm_w[d])
        ol_s, h8 = _lru_sample(U8, _state_row(state_lru_h[d]), *lru_w)
        toks = lambda a: a.reshape(BS, GROUP, -1)[:, TOK0:].reshape(MS, -1)
        last = lambda a: a.reshape(BS, GROUP, -1)[:, GROUP - 1]
        outs_s[1].append(Us[:, TS - 3:, 0:CONV_CH])
        outs_s[3].append(last(n8).reshape(BS, HEADS, DK))
        outs_s[4].append(last(m8).reshape(BS, HEADS, DK)[:, :, 0])
        outs_s[5].append(last(h8))
        outs_s[6].append(Us[:, TS - 3:, COL_LX:COL_LX + LRU_W])

        x2p = _outproj_ln((og_p, hm_p, ol_p), x1p, w_out_b, d, ln_g[d, 1], ln_b[d, 1])
        x2s = _outproj_ln((toks(og_s), toks(hm_s), toks(ol_s)), x1s, w_out_b, d, ln_g[d, 1], ln_b[d, 1])
        xp, xs = _ffn_ln_both(x2p, x2s, ffn2, d, ln_g[d, 2], ln_b[d, 2])

    y_prompt = xp.reshape(BP, TP, D)
    y_sample = xs.reshape(BS, TS, D)
    stack = lambda o: jnp.stack(o)
    return (y_prompt, y_sample,
            *[stack(o) for o in outs_p],
            gS_s, stack(outs_s[1]), mC_s, *[stack(outs_s[i]) for i in range(3, 7)])
```

```python
import functools
import math

import jax
import jax.numpy as jnp
from jax import lax
from jax.experimental import pallas as pl
from jax.experimental.pallas import tpu as pltpu

f32 = jnp.float32
bf16 = jnp.bfloat16

DEPTH = 2
D_MODEL = 2048
D_FF = 5632
HEADS = 6
DK = 128
LRU_W = 512
LRU_BLOCKS = 8
LRU_C = 8.0
CONV_W = 4
CHUNK = 64
ALPHA = (2 * DEPTH) ** 0.25
NORM_EPS = 1e-6
NEG = -1e30

QK = HEADS * DK
CONV_CH = 3 * QK
MIX_W = 4 * QK
CB_Q, CB_K, CB_V, CB_Z = 0, 6, 12, 18
CB_MQ, CB_MK, CB_MV, CB_MO = 24, 30, 36, 42
CB_GDN, CB_MLSTM = 0, 1
CB_LX, CB_LY = 12, 13
CB_GATES = 56
COL_LX = 6144
D_INP = 7296
GROUP = 8
TOK0 = 4
ROW_TILE = 512
HEAD_ROW_BLOCKS = 2
INPROJ_TN = 2432
SAMPLE_SEQS_PER_STEP = 64

VMEM_LIMIT = 60 * 1024 * 1024


def _cparams(sem):
    return pltpu.CompilerParams(dimension_semantics=sem, vmem_limit_bytes=VMEM_LIMIT)


def _silu(x):
    return x * jax.nn.sigmoid(x)


def _softplus(x):
    return jnp.maximum(x, 0.0) + jnp.log1p(jnp.exp(-jnp.abs(x)))


def _expm1(x):
    u = jnp.exp(x)
    um1 = u - 1.0
    lg = jnp.where(u == 1.0, 1.0, jnp.log(jnp.where(u == 0.0, 1.0, u)))
    return jnp.where(u == 1.0, x, jnp.where(u == 0.0, -1.0, um1 * x / lg))


def _layernorm(y, g, b):
    mu = jnp.mean(y, -1, keepdims=True)
    d = y - mu
    var = jnp.mean(d * d, -1, keepdims=True)
    return d * lax.rsqrt(var + NORM_EPS) * g + b


def _rms(x):
    return x * lax.rsqrt(jnp.mean(x * x, -1, keepdims=True) + NORM_EPS)


def _l2(x):
    return x * lax.rsqrt(jnp.sum(x * x, -1, keepdims=True) + NORM_EPS)


def _colsel(gates, idx):
    lane = lax.broadcasted_iota(jnp.int32, gates.shape, 1)
    return jnp.sum(jnp.where(lane == idx, gates, 0.0), axis=1, keepdims=True)


def _dot(a, b):
    return jnp.dot(a, b, preferred_element_type=f32)


def _bmm(a, b):
    return jnp.einsum('nij,njk->nik', a, b, preferred_element_type=f32)


def _bmm_nt(a, b):
    return jnp.einsum('nid,njd->nij', a, b, preferred_element_type=f32)


def _bmm_tn(a, b):
    return jnp.einsum('nck,ncv->nkv', a, b, preferred_element_type=f32)


def _ij(C):
    return (lax.broadcasted_iota(jnp.int32, (C, C), 0), lax.broadcasted_iota(jnp.int32, (C, C), 1))


def _cumsum_row(col3, ii, jj):
    return jnp.sum(jnp.where((ii <= jj)[None], col3, 0.0), axis=1, keepdims=True)


def _row2col(row3, ii, jj):
    return jnp.sum(jnp.where((ii == jj)[None], row3, 0.0), axis=2, keepdims=True)


def _col2row(col3, ii, jj):
    return jnp.sum(jnp.where((ii == jj)[None], col3, 0.0), axis=1, keepdims=True)


def _conv_carry(x, w_ref, xp_ref, tail):
    T = x.shape[0]
    xp_ref[0:8, :] = tail
    xp_ref[8:T + 8, :] = x
    y = w_ref[3:4, :] * x
    for k in range(CONV_W - 1):
        y = y + w_ref[k:k + 1, :] * xp_ref[pl.ds(5 + k, T), :]
    return y


def _conv_groups(x, w_ref):
    y = w_ref[3:4, :] * x
    for s in range(1, CONV_W):
        y = y + w_ref[3 - s:4 - s, :] * pltpu.roll(x, s, axis=0)
    return y


def _group_pos(shape):
    return lax.broadcasted_iota(jnp.int32, shape, 0) & (GROUP - 1)


def _ffn_ln_kernel(*refs, cast_weights, aliased):
    x_ref, wg_ref, wu_ref, wd_ref, g_ref, b_ref = refs[:6]
    n_in = 7 if aliased else 6
    o_ref = refs[n_in]
    xb_ref = refs[-1]
    j = pl.program_id(1)

    @pl.when(j == 0)
    def _():
        o_ref[...] = jnp.zeros_like(o_ref)
        xb_ref[...] = x_ref[...].astype(bf16)

    if cast_weights:
        wg, wu, wd = (r[...].astype(bf16) for r in (wg_ref, wu_ref, wd_ref))
        for w_out_ref, w in zip(refs[n_in + 1:n_in + 4], (wg, wu, wd)):
            w_out_ref[...] = w
    else:
        wg, wu, wd = wg_ref[...], wu_ref[...], wd_ref[...]
    xb = xb_ref[...]
    h = (_silu(_dot(xb, wg)) * _dot(xb, wu)).astype(bf16)
    o_ref[...] += _dot(h, wd)

    @pl.when(j == pl.num_programs(1) - 1)
    def _():
        y = ALPHA * x_ref[...] + 0.5 * o_ref[...]
        o_ref[...] = _layernorm(y, g_ref[...], b_ref[...])


def _ffn_ln(x, wg, wu, wd, g, b, *, tm, tf=512, row0=0, prev=None):
    M, D = x.shape
    F = wg.shape[1]
    rows = lambda i, j: (i + row0, 0)
    in_specs = [
        pl.BlockSpec((tm, D), rows),
        pl.BlockSpec((D, tf), lambda i, j: (0, j)),
        pl.BlockSpec((D, tf), lambda i, j: (0, j)),
        pl.BlockSpec((tf, D), lambda i, j: (j, 0)),
        pl.BlockSpec((1, D), lambda i, j: (0, 0)),
        pl.BlockSpec((1, D), lambda i, j: (0, 0)),
    ]
    operands = [x, wg, wu, wd, g.reshape(1, D), b.reshape(1, D)]
    if prev is not None:
        in_specs.append(pl.BlockSpec(memory_space=pl.ANY))
        operands.append(prev)
    return pl.pallas_call(
        functools.partial(_ffn_ln_kernel, cast_weights=False, aliased=prev is not None),
        grid=(M // tm - row0, F // tf),
        in_specs=in_specs,
        out_specs=pl.BlockSpec((tm, D), rows),
        out_shape=jax.ShapeDtypeStruct((M, D), f32),
        input_output_aliases={6: 0} if prev is not None else {},
        scratch_shapes=[pltpu.VMEM((tm, D), bf16)],
        compiler_params=_cparams(("parallel", "arbitrary")),
        name="ffn_ln",
    )(*operands)


def _ffn_ln_head(x, wg, wu, wd, d, g, b, *, tm, tf=256):
    M, D = x.shape
    F = wg.shape[2]
    return pl.pallas_call(
        functools.partial(_ffn_ln_kernel, cast_weights=True, aliased=False),
        grid=(1, F // tf),
        in_specs=[
            pl.BlockSpec((tm, D), lambda i, j: (0, 0), pipeline_mode=pl.Buffered(1)),
            pl.BlockSpec((None, D, tf), lambda i, j: (d, 0, j)),
            pl.BlockSpec((None, D, tf), lambda i, j: (d, 0, j)),
            pl.BlockSpec((None, tf, D), lambda i, j: (d, j, 0)),
            pl.BlockSpec((1, D), lambda i, j: (0, 0)),
            pl.BlockSpec((1, D), lambda i, j: (0, 0)),
        ],
        out_specs=[pl.BlockSpec((tm, D), lambda i, j: (0, 0)),
                   pl.BlockSpec((D, tf), lambda i, j: (0, j)),
                   pl.BlockSpec((D, tf), lambda i, j: (0, j)),
                   pl.BlockSpec((tf, D), lambda i, j: (j, 0))],
        out_shape=[jax.ShapeDtypeStruct((M, D), f32), jax.ShapeDtypeStruct((D, F), bf16),
                   jax.ShapeDtypeStruct((D, F), bf16), jax.ShapeDtypeStruct((F, D), bf16)],
        scratch_shapes=[pltpu.VMEM((tm, D), bf16)],
        compiler_params=_cparams(("arbitrary", "arbitrary")),
        name="ffn_ln_head",
    )(x, wg, wu, wd, g.reshape(1, D), b.reshape(1, D))


def _ffn_ln_both(xp, xs, w32, d, g, b):
    out, *wb = _ffn_ln_head(xp, *w32, d, g, b, tm=HEAD_ROW_BLOCKS * ROW_TILE)
    yp = _ffn_ln(xp, *wb, g, b, tm=ROW_TILE, row0=HEAD_ROW_BLOCKS, prev=out)
    ys = _ffn_ln(xs, *wb, g, b, tm=ROW_TILE)
    return yp, ys


def _inproj_kernel(x_ref, w0_ref, w_ref, o_ref):
    n = pl.program_id(0)
    xb = x_ref[...].astype(bf16)

    @pl.when(n == 0)
    def _():
        o_ref[...] = _dot(xb, w0_ref[...])

    @pl.when(n > 0)
    def _():
        o_ref[...] = _dot(xb, w_ref[...])


def _inproj(x, w_first, w_rest, d, *, tm=ROW_TILE, tn=INPROJ_TN):
    M, D = x.shape
    N = tn + w_rest.shape[2]
    return pl.pallas_call(
        _inproj_kernel,
        grid=(N // tn, M // tm),
        in_specs=[pl.BlockSpec((tm, D), lambda n, i: (i, 0)),
                  pl.BlockSpec((None, D, tn), lambda n, i: (d, 0, 0), pipeline_mode=pl.Buffered(1)),
                  pl.BlockSpec((None, D, tn), lambda n, i: (d, 0, jnp.maximum(n - 1, 0)))],
        out_specs=pl.BlockSpec((tm, tn), lambda n, i: (i, n)),
        out_shape=jax.ShapeDtypeStruct((M, N), f32),
        compiler_params=_cparams(("arbitrary", "arbitrary")),
        name="in_proj",
    )(x, w_first, w_rest)


def _outproj_ln_kernel(og_ref, hm_ref, ol_ref, x_ref, w_ref, g_ref, b_ref, o_ref):
    mix = (_dot(og_ref[...], w_ref[0:QK, :]) + _dot(hm_ref[...], w_ref[QK:2 * QK, :])
           + _dot(ol_ref[...], w_ref[2 * QK:, :]))
    o_ref[...] = _layernorm(ALPHA * x_ref[...] + mix, g_ref[...], b_ref[...])


def _outproj_ln(mix, x, w, d, g, b, *, tm=ROW_TILE):
    M, D = x.shape
    return pl.pallas_call(
        _outproj_ln_kernel,
        grid=(M // tm,),
        in_specs=[pl.BlockSpec((tm, wd), lambda i: (i, 0)) for wd in (QK, QK, LRU_W)]
        + [pl.BlockSpec((tm, D), lambda i: (i, 0)),
           pl.BlockSpec((None, D, D), lambda i: (d, 0, 0)),
           pl.BlockSpec((1, D), lambda i: (0, 0)),
           pl.BlockSpec((1, D), lambda i: (0, 0))],
        out_specs=pl.BlockSpec((tm, D), lambda i: (i, 0)),
        out_shape=jax.ShapeDtypeStruct((M, D), f32),
        compiler_params=_cparams(("parallel",)),
        name="out_proj_ln",
    )(*mix, x, w, g.reshape(1, D), b.reshape(1, D))


def _gdn_gates(gates, gc_ref):
    beta = jax.nn.sigmoid(gates)
    gl = -jnp.exp(gc_ref[0:1, :]) * _softplus(gates + gc_ref[1:2, :])
    return beta, gl


def _mlstm_gates(gates, gc_ref):
    i_pre = gates + gc_ref[2:3, :]
    logf = -_softplus(-(gates + gc_ref[3:4, :]))
    return i_pre, logf


def _gdn_chunk(q3, k3, v3, b3, g3, S):
    C = q3.shape[1]
    ii, jj = _ij(C)
    g_row = _cumsum_row(g3, ii, jj)
    g_col = _row2col(g_row, ii, jj)
    decay = jnp.exp(jnp.where((ii >= jj)[None], g_col - g_row, NEG))
    kb = k3.astype(bf16)
    kk = _bmm_nt(kb, kb)
    p = jnp.where((ii > jj)[None], -(b3 * kk * decay), 0.0)
    e = p
    for _ in range(int(math.log2(C)) - 1):
        pb = p.astype(bf16)
        p = _bmm(pb, pb)
        e = e + p + _bmm(e.astype(bf16), p.astype(bf16))
    eb = e.astype(bf16)
    e_g = jnp.exp(g_col)
    rv = b3 * v3
    rk = (b3 * e_g) * k3
    uv = rv + _bmm(eb, rv.astype(bf16))
    wks = (rk + _bmm(eb, rk.astype(bf16))).astype(bf16)
    qk = (_bmm_nt(q3.astype(bf16), kb) * decay).astype(bf16)
    g_last = g_row[:, :, C - 1:C]
    qd = (q3 * e_g).astype(bf16)
    kt = (k3 * jnp.exp(g_last - g_col)).astype(bf16)
    Sb = S.astype(bf16)
    Ub = (uv - _bmm(wks, Sb)).astype(bf16)
    o = _bmm(qd, Sb) + _bmm(qk, Ub)
    return o, jnp.exp(g_last) * S + _bmm_tn(kt, Ub)


def _mlstm_chunk(q3, k3, vb, i3, f3, Cm, nr, m):
    C = q3.shape[1]
    ii, jj = _ij(C)
    b_row = _cumsum_row(f3, ii, jj)
    b_col = _row2col(b_row, ii, jj)
    i_row = _col2row(i3, ii, jj)
    D = jnp.where((ii >= jj)[None], b_col - b_row + i_row, NEG)
    d_max = jnp.max(D, axis=2, keepdims=True)
    qb = q3.astype(bf16)
    qk = _bmm_nt(qb, k3.astype(bf16))
    m_t = jnp.maximum(b_col + m, d_max)
    inter = jnp.exp(b_col + m - m_t)
    Sw = jnp.exp(D - m_t) * qk
    num = inter * _bmm(qb, Cm.astype(bf16)) + _bmm(Sw.astype(bf16), vb)
    den = inter * jnp.sum(q3 * nr, -1, keepdims=True) + jnp.sum(Sw, -1, keepdims=True)
    hh = num / jnp.maximum(jnp.abs(den), jnp.exp(-m_t))
    m_new = m_t[:, C - 1:C]
    b_last = b_col[:, C - 1:C]
    kw = jnp.exp(b_last - b_col + i3 - m_new) * k3
    dec = jnp.exp(b_last + m - m_new)
    return (hh, dec * Cm + _bmm_tn(kw.astype(bf16), vb), dec * nr + jnp.sum(kw, axis=1, keepdims=True), m_new)


def _gdn_prompt_kernel(*refs, C, B):
    mains, gts = refs[0:2 * B:2], refs[1:2 * B:2]
    cw_ref, gc_ref, nw_ref, o_ref, s_ref, xp_ref, tail_ref = refs[2 * B:]
    n = pl.program_id(0)
    NC = B * HEADS

    @pl.when(n == 0)
    def _():
        tail_ref[...] = jnp.zeros_like(tail_ref)
        s_ref[...] = jnp.zeros_like(s_ref)

    qs, ks, vs, zs, bs, gs = [], [], [], [], [], []
    def conv_silu(b, col):
        cols = slice(col, col + DK)
        x = mains[b][:, cols]
        xp = jnp.concatenate([tail_ref[b, :, cols], x], axis=0)
        y = cw_ref[3:4, cols] * x
        for s in range(1, CONV_W):
            y = y + cw_ref[3 - s:4 - s, cols] * pltpu.roll(xp, s, axis=0)[8:C + 8]
        tail_ref[b, :, cols] = x[C - 8:C]
        return _silu(y)

    for b in range(B):
        beta, gl = _gdn_gates(gts[b][...], gc_ref)
        for h in range(HEADS):
            qs.append(conv_silu(b, h * DK))
            ks.append(conv_silu(b, QK + h * DK))
            vs.append(conv_silu(b, 2 * QK + h * DK))
            zs.append(mains[b][:, CONV_CH + h * DK:CONV_CH + (h + 1) * DK])
            bs.append(beta[:, h:h + 1])
            gs.append(gl[:, HEADS + h:HEADS + h + 1])
    q3 = _l2(jnp.stack(qs)) * (DK ** -0.5)
    k3 = _l2(jnp.stack(ks))
    v3 = jnp.stack(vs)
    b3 = jnp.stack(bs)
    g3 = jnp.stack(gs)
    o, S = _gdn_chunk(q3, k3, v3, b3, g3, s_ref[...].reshape(NC, DK, DK))
    s_ref[...] = S.reshape(B, HEADS, DK, DK)
    o = _rms(o) * nw_ref[...] * _silu(jnp.stack(zs))
    for b in range(B):
        for h in range(HEADS):
            o_ref[b, :, h * DK:(h + 1) * DK] = o[b * HEADS + h].astype(o_ref.dtype)


def _gdn_prompt(U, B, T, conv_w, hc, norm_w, *, C=CHUNK):
    N = T // C
    in_specs = []
    for b in range(B):
        in_specs.append(pl.BlockSpec((C, MIX_W), lambda n, b=b: (b * N + n, CB_GDN)))
        in_specs.append(pl.BlockSpec((C, DK), lambda n, b=b: (b * N + n, CB_GATES)))
    in_specs += [pl.BlockSpec((CONV_W, CONV_CH), lambda n: (0, 0)),
                 pl.BlockSpec((8, DK), lambda n: (0, 0)),
                 pl.BlockSpec((1, DK), lambda n: (0, 0))]
    o, S = pl.pallas_call(
        functools.partial(_gdn_prompt_kernel, C=C, B=B),
        grid=(N,),
        in_specs=in_specs,
        out_specs=[pl.BlockSpec((B, None, C, QK), lambda n: (0, n, 0, 0)),
                   pl.BlockSpec((B, HEADS, DK, DK), lambda n: (0, 0, 0, 0))],
        out_shape=[jax.ShapeDtypeStruct((B, N, C, QK), bf16),
                   jax.ShapeDtypeStruct((B, HEADS, DK, DK), f32)],
        scratch_shapes=[pltpu.VMEM((C + 8, CONV_CH), f32), pltpu.VMEM((B, 8, CONV_CH), f32)],
        compiler_params=_cparams(("arbitrary",)),
        name="gdn_prompt",
    )(*([U, U] * B), conv_w, hc, norm_w.reshape(1, DK))
    return o.reshape(B * T, QK), S


def _mlstm_prompt_kernel(*refs, C, B):
    mains, gts = refs[0:2 * B:2], refs[1:2 * B:2]
    gc_ref, nw_ref, h_ref, c_ref, n_ref, m_ref = refs[2 * B:]
    n = pl.program_id(0)
    NC = B * HEADS

    @pl.when(n == 0)
    def _():
        c_ref[...] = jnp.zeros_like(c_ref)
        n_ref[...] = jnp.zeros_like(n_ref)
        m_ref[...] = jnp.zeros_like(m_ref)

    qs, ks, vs, os_, is_, fs = [], [], [], [], [], []
    for b in range(B):
        i_pre, logf = _mlstm_gates(gts[b][...], gc_ref)
        for h in range(HEADS):
            qs.append(mains[b][:, h * DK:(h + 1) * DK])
            ks.append(mains[b][:, QK + h * DK:QK + (h + 1) * DK])
            vs.append(mains[b][:, 2 * QK + h * DK:2 * QK + (h + 1) * DK])
            os_.append(mains[b][:, 3 * QK + h * DK:3 * QK + (h + 1) * DK])
            is_.append(i_pre[:, 2 * HEADS + h:2 * HEADS + h + 1])
            fs.append(logf[:, 3 * HEADS + h:3 * HEADS + h + 1])
    q3 = jnp.stack(qs)
    k3 = jnp.stack(ks) * (DK ** -0.5)
    vb = jnp.stack(vs).astype(bf16)
    i3 = jnp.stack(is_)
    f3 = jnp.stack(fs)
    hh, Cm, nr, m = _mlstm_chunk(q3, k3, vb, i3, f3, c_ref[...].reshape(NC, DK, DK),
                                 n_ref[...].reshape(NC, 1, DK), m_ref[...].reshape(NC, 1, DK)[:, :, 0:1])
    nw = jnp.stack([nw_ref[h] for _ in range(B) for h in range(HEADS)])
    hh = _rms(hh) * nw * jax.nn.sigmoid(jnp.stack(os_))
    c_ref[...] = Cm.reshape(B, HEADS, DK, DK)
    n_ref[...] = nr.reshape(B, HEADS, 1, DK)
    m_ref[...] = jnp.broadcast_to(m, (NC, 1, DK)).reshape(B, HEADS, 1, DK)
    for b in range(B):
        for h in range(HEADS):
            h_ref[b, :, h * DK:(h + 1) * DK] = hh[b * HEADS + h].astype(h_ref.dtype)


def _mlstm_prompt(U, B, T, hc, norm_w, *, C=CHUNK):
    N = T // C
    in_specs = []
    for b in range(B):
        in_specs.append(pl.BlockSpec((C, MIX_W), lambda n, b=b: (b * N + n, CB_MLSTM)))
        in_specs.append(pl.BlockSpec((C, DK), lambda n, b=b: (b * N + n, CB_GATES)))
    in_specs += [pl.BlockSpec((8, DK), lambda n: (0, 0)),
                 pl.BlockSpec((HEADS, 1, DK), lambda n: (0, 0, 0))]
    st = lambda r: pl.BlockSpec((B, HEADS, r, DK), lambda n: (0, 0, 0, 0))
    h, Cm, nr, m = pl.pallas_call(
        functools.partial(_mlstm_prompt_kernel, C=C, B=B),
        grid=(N,),
        in_specs=in_specs,
        out_specs=[pl.BlockSpec((B, None, C, QK), lambda n: (0, n, 0, 0)), st(DK), st(1), st(1)],
        out_shape=[jax.ShapeDtypeStruct((B, N, C, QK), bf16),
                   jax.ShapeDtypeStruct((B, HEADS, DK, DK), f32),
                   jax.ShapeDtypeStruct((B, HEADS, 1, DK), f32),
                   jax.ShapeDtypeStruct((B, HEADS, 1, DK), f32)],
        compiler_params=_cparams(("arbitrary",)),
        name="mlstm_prompt",
    )(*([U, U] * B), hc, norm_w.reshape(HEADS, 1, DK))
    return h.reshape(B * T, QK), Cm, nr, m


def _lru_gates(c, wr_ref, br_ref, wi_ref, bi_ref, lam_ref):
    cb = c.astype(bf16)
    r = jax.nn.sigmoid(_dot(cb, wr_ref[...]) + br_ref[...])
    ig = jax.nn.sigmoid(_dot(cb, wi_ref[...]) + bi_ref[...])
    log_a = -LRU_C * r * _softplus(-lam_ref[...])
    a = jnp.exp(log_a)
    u = jnp.sqrt(-_expm1(2.0 * log_a)) * (ig * c)
    return a, u


def _lru_prompt_kernel(x_ref, y_ref, cw_ref, cb_ref, wr_ref, br_ref, wi_ref, bi_ref, lam_ref,
                       o_ref, hl_ref, xp_ref, tail_ref, hc_ref):
    tc = pl.program_id(1)
    Tc, W = x_ref.shape

    @pl.when(tc == 0)
    def _():
        tail_ref[...] = jnp.zeros_like(tail_ref)
        hc_ref[...] = jnp.zeros_like(hc_ref)

    x = x_ref[...]
    c = _conv_carry(x, cw_ref, xp_ref, tail_ref[...]) + cb_ref[...]
    tail_ref[...] = x[Tc - 8:Tc]
    a, u = _lru_gates(c, wr_ref, br_ref, wi_ref, bi_ref, lam_ref)
    row = lax.broadcasted_iota(jnp.int32, (Tc, W), 0)
    d = 1
    while d < Tc:
        keep = row >= d
        a_sh = jnp.where(keep, pltpu.roll(a, d, axis=0), 1.0)
        u_sh = jnp.where(keep, pltpu.roll(u, d, axis=0), 0.0)
        u = a * u_sh + u
        a = a * a_sh
        d *= 2
    hs = u + a * hc_ref[0:1, :]
    last = hs[Tc - 1:Tc]
    hc_ref[...] = jnp.broadcast_to(last, hc_ref.shape)
    hl_ref[...] = last
    o_ref[...] = (hs * jax.nn.gelu(y_ref[...])).astype(o_ref.dtype)


def _lru_prompt(U, B, T, cw, cb, wr, br, wi, bi, lam, *, Tc=512):
    W = LRU_W
    nt = T // Tc
    vec = lambda: pl.BlockSpec((1, W), lambda b, t: (0, 0))
    return pl.pallas_call(
        _lru_prompt_kernel,
        grid=(B, nt),
        in_specs=[pl.BlockSpec((Tc, W), lambda b, t: (b * nt + t, CB_LX)),
                  pl.BlockSpec((Tc, W), lambda b, t: (b * nt + t, CB_LY)),
                  pl.BlockSpec((CONV_W, W), lambda b, t: (0, 0)), vec(),
                  pl.BlockSpec((W, W), lambda b, t: (0, 0)), vec(),
                  pl.BlockSpec((W, W), lambda b, t: (0, 0)), vec(), vec()],
        out_specs=[pl.BlockSpec((Tc, W), lambda b, t: (b * nt + t, 0)),
                   pl.BlockSpec((None, 1, W), lambda b, t: (b, 0, 0))],
        out_shape=[jax.ShapeDtypeStruct((B * T, W), bf16),
                   jax.ShapeDtypeStruct((B, 1, W), f32)],
        scratch_shapes=[pltpu.VMEM((Tc + 8, W), f32), pltpu.VMEM((8, W), f32), pltpu.VMEM((8, W), f32)],
        compiler_params=_cparams(("parallel", "arbitrary")),
        name="lru_prompt",
    )(U, U, cw, cb.reshape(1, W), wr, br.reshape(1, W), wi, bi.reshape(1, W), lam.reshape(1, W))


def _gdn_sample_kernel(q_ref, k_ref, v_ref, z_ref, gt_ref, wq_ref, wk_ref, wv_ref, gc_ref, nw_ref, s0_ref,
                       *rest):
    o_ref, s_ref = rest[-2:]
    h = pl.program_id(0)
    BB = s0_ref.shape[0]
    R = BB * GROUP
    q = _l2(_silu(_conv_groups(q_ref[...], wq_ref))) * (DK ** -0.5)
    k = _l2(_silu(_conv_groups(k_ref[...], wk_ref)))
    v = _silu(_conv_groups(v_ref[...], wv_ref))
    beta, gl = _gdn_gates(gt_ref[...], gc_ref)
    is_tok = _group_pos((R, 1)) >= TOK0
    beta = jnp.where(is_tok, _colsel(beta, h), 0.0)
    gl = jnp.where(is_tok, _colsel(gl, HEADS + h), 0.0)
    grp = lambda a: a.reshape(BB, GROUP, a.shape[-1])
    o, S = _gdn_chunk(grp(q), grp(k), grp(v), grp(beta), grp(gl), s0_ref[...])
    s_ref[...] = S
    o_ref[...] = (_rms(o.reshape(R, DK)) * nw_ref[...] * _silu(z_ref[...])).astype(o_ref.dtype)


def _stacked_out(prev):
    if prev is None:
        return [], [], {}
    return [pl.BlockSpec(memory_space=pl.ANY)], [prev], None


def _gdn_sample(U8, S_all, d, S_prev, conv_w, hc, norm_w, *, BB=SAMPLE_SEQS_PER_STEP):
    B = S_all.shape[1]
    R = BB * GROUP
    tok = lambda cb: pl.BlockSpec((R, DK), lambda h, b: (b, cb + h))
    cw = lambda cb: pl.BlockSpec((CONV_W, DK), lambda h, b: (0, cb + h))
    st = pl.BlockSpec((None, BB, None, DK, DK), lambda h, b: (d, b, h, 0, 0))
    extra_specs, extra_ops, _ = _stacked_out(S_prev)
    n_in = 11
    return pl.pallas_call(
        _gdn_sample_kernel,
        grid=(HEADS, B // BB),
        in_specs=[tok(CB_Q), tok(CB_K), tok(CB_V), tok(CB_Z),
                  pl.BlockSpec((R, DK), lambda h, b: (b, CB_GATES)),
                  cw(CB_Q), cw(CB_K), cw(CB_V),
                  pl.BlockSpec((8, DK), lambda h, b: (0, 0)),
                  pl.BlockSpec((1, DK), lambda h, b: (0, 0)), st] + extra_specs,
        out_specs=[pl.BlockSpec((R, DK), lambda h, b: (b, h)), st],
        out_shape=[jax.ShapeDtypeStruct((B * GROUP, QK), bf16),
                   jax.ShapeDtypeStruct(S_all.shape, f32)],
        input_output_aliases={n_in: 1} if S_prev is not None else {},
        compiler_params=_cparams(("parallel", "parallel")),
        name="gdn_sample",
    )(U8, U8, U8, U8, U8, conv_w, conv_w, conv_w, hc, norm_w.reshape(1, DK), S_all, *extra_ops)


def _mlstm_sample_kernel(q_ref, k_ref, v_ref, og_ref, gt_ref, gc_ref, nw_ref, c0_ref, n0_ref, m0_ref,
                         *rest):
    h_ref, c_ref, n_ref, m_ref = rest[-4:]
    h = pl.program_id(0)
    BB = c0_ref.shape[0]
    R = BB * GROUP
    i_pre, logf = _mlstm_gates(gt_ref[...], gc_ref)
    is_tok = _group_pos((R, 1)) >= TOK0
    i_pre = jnp.where(is_tok, _colsel(i_pre, 2 * HEADS + h), NEG)
    logf = jnp.where(is_tok, _colsel(logf, 3 * HEADS + h), 0.0)
    grp = lambda a: a.reshape(BB, GROUP, a.shape[-1])
    old = lambda ref: grp(ref[...])[:, TOK0 - 1:TOK0, :]
    hh, Cm, nr, m = _mlstm_chunk(grp(q_ref[...]), grp(k_ref[...] * (DK ** -0.5)), grp(v_ref[...]).astype(bf16),
                                 grp(i_pre), grp(logf), c0_ref[...], old(n0_ref), old(m0_ref)[:, :, 0:1])
    c_ref[...] = Cm
    n_ref[...] = jnp.broadcast_to(nr, (BB, GROUP, DK)).reshape(R, DK)
    m_ref[...] = jnp.broadcast_to(m, (BB, GROUP, DK)).reshape(R, DK)
    h_ref[...] = (_rms(hh.reshape(R, DK)) * nw_ref[h] * jax.nn.sigmoid(og_ref[...])).astype(h_ref.dtype)


def _mlstm_sample(U8, C_all, d, C_prev, n0g, m0g, hc, norm_w, *, BB=SAMPLE_SEQS_PER_STEP):
    B = C_all.shape[1]
    R = BB * GROUP
    tok = lambda cb: pl.BlockSpec((R, DK), lambda h, b: (b, cb + h))
    row = pl.BlockSpec((R, DK), lambda h, b: (b, h))
    st = pl.BlockSpec((None, BB, None, DK, DK), lambda h, b: (d, b, h, 0, 0))
    extra_specs, extra_ops, _ = _stacked_out(C_prev)
    n_in = 10
    return pl.pallas_call(
        _mlstm_sample_kernel,
        grid=(HEADS, B // BB),
        in_specs=[tok(CB_MQ), tok(CB_MK), tok(CB_MV), tok(CB_MO),
                  pl.BlockSpec((R, DK), lambda h, b: (b, CB_GATES)),
                  pl.BlockSpec((8, DK), lambda h, b: (0, 0)),
                  pl.BlockSpec((HEADS, 1, DK), lambda h, b: (0, 0, 0)),
                  st, row, row] + extra_specs,
        out_specs=[row, st, row, row],
        out_shape=[jax.ShapeDtypeStruct((B * GROUP, QK), bf16),
                   jax.ShapeDtypeStruct(C_all.shape, f32),
                   jax.ShapeDtypeStruct((B * GROUP, QK), f32),
                   jax.ShapeDtypeStruct((B * GROUP, QK), f32)],
        input_output_aliases={n_in: 1} if C_prev is not None else {},
        compiler_params=_cparams(("parallel", "parallel")),
        name="mlstm_sample",
    )(U8, U8, U8, U8, U8, hc, norm_w.reshape(HEADS, 1, DK), C_all, n0g, m0g, *extra_ops)


def _lru_sample_kernel(x_ref, y_ref, h0_ref, cw_ref, cb_ref, wr_ref, br_ref, wi_ref, bi_ref, lam_ref,
                       o_ref, h_ref):
    c = _conv_groups(x_ref[...], cw_ref) + cb_ref[...]
    a, u = _lru_gates(c, wr_ref, br_ref, wi_ref, bi_ref, lam_ref)
    hs = h0_ref[...]
    pos = _group_pos(hs.shape)
    for t in range(GROUP - TOK0):
        hs = jnp.where(pos == TOK0 + t, a * pltpu.roll(hs, 1, axis=0) + u, hs)
    h_ref[...] = hs
    o_ref[...] = (hs * jax.nn.gelu(y_ref[...])).astype(o_ref.dtype)


def _lru_sample(U8, h0g, cw, cb, wr, br, wi, bi, lam, *, R=128):
    W = LRU_W
    rows = U8.shape[0]
    vec = lambda: pl.BlockSpec((1, W), lambda i: (0, 0))
    return pl.pallas_call(
        _lru_sample_kernel,
        grid=(rows // R,),
        in_specs=[pl.BlockSpec((R, W), lambda i: (i, CB_LX)),
                  pl.BlockSpec((R, W), lambda i: (i, CB_LY)),
                  pl.BlockSpec((R, W), lambda i: (i, 0)),
                  pl.BlockSpec((CONV_W, W), lambda i: (0, 0)), vec(),
                  pl.BlockSpec((W, W), lambda i: (0, 0)), vec(),
                  pl.BlockSpec((W, W), lambda i: (0, 0)), vec(), vec()],
        out_specs=[pl.BlockSpec((R, W), lambda i: (i, 0)), pl.BlockSpec((R, W), lambda i: (i, 0))],
        out_shape=[jax.ShapeDtypeStruct((rows, W), bf16), jax.ShapeDtypeStruct((rows, W), f32)],
        compiler_params=_cparams(("parallel",)),
        name="lru_sample",
    )(U8, U8, h0g, cw, cb.reshape(1, W), wr, br.reshape(1, W), wi, bi.reshape(1, W), lam.reshape(1, W))


def _permute_w_in(w_in):
    s = {}
    off = 0
    for name, size in (("qkv", CONV_CH), ("z", QK), ("gb", HEADS), ("ga", HEADS), ("mq", QK), ("mk", QK),
                       ("mv", QK), ("mo", QK), ("mi", HEADS), ("mf", HEADS), ("lx", LRU_W), ("ly", LRU_W)):
        s[name] = w_in[..., off:off + size]
        off += size
    pad = jnp.zeros(w_in.shape[:-1] + (DK - 4 * HEADS,), w_in.dtype)
    skip = INPROJ_TN - CONV_CH
    return jnp.concatenate([s["z"][..., skip:], s["mq"], s["mk"], s["mv"], s["mo"], s["lx"], s["ly"],
                            s["gb"], s["ga"], s["mi"], s["mf"], pad], axis=-1)


def _block_diag(w):
    nb, d, e = w.shape
    eye = jnp.eye(nb, dtype=w.dtype)
    return (w[:, :, None, :] * eye[:, None, :, None]).reshape(nb * d, nb * e)


def _gate_consts(a_log, dt_bias, i_bias, f_bias):
    row = lambda v, k: jnp.pad(v, (k * HEADS, DK - (k + 1) * HEADS))
    rows = [row(a_log, 1), row(dt_bias, 1), row(i_bias, 2), row(f_bias, 3)]
    return jnp.stack(rows + [jnp.zeros((DK,), f32)] * (8 - len(rows)))


def _state_row(state):
    B, W = state.shape
    return jnp.pad(state[:, None, :], ((0, 0), (TOK0 - 1, GROUP - TOK0), (0, 0))).reshape(B * GROUP, W)


def _tail_rows(U, B, T, col, width):
    return jnp.stack([lax.slice(U, ((b + 1) * T - (CONV_W - 1), col), ((b + 1) * T, col + width))
                      for b in range(B)])


def kernel(x_prompt, x_sample, state_gdn_S, state_gdn_conv, state_mlstm_C, state_mlstm_n, state_mlstm_m, state_lru_h, state_lru_conv, ffn1_wg, ffn1_wu, ffn1_wd, ln_g, ln_b, w_in, gdn_conv_w, gdn_A_log, gdn_dt_bias, gdn_norm_w, mlstm_i_bias, mlstm_f_bias, mlstm_norm_w, lru_conv_w, lru_conv_b, lru_wr, lru_br, lru_wi, lru_bi, lru_lambda, w_out, ffn2_wg, ffn2_wu, ffn2_wd):
    BP, TP, D = x_prompt.shape
    BS, TS, _ = x_sample.shape
    MP, MS = BP * TP, BS * TS
    ffn1 = (ffn1_wg, ffn1_wu, ffn1_wd)
    ffn2 = (ffn2_wg, ffn2_wu, ffn2_wd)
    w_in_b = w_in.astype(bf16)
    w_in_p = _permute_w_in(w_in_b)
    w_out_b = w_out.astype(bf16)

    xp, xs = x_prompt.reshape(MP, D), x_sample.reshape(MS, D)
    outs_p = [[] for _ in range(7)]
    outs_s = [[] for _ in range(7)]
    gS_s = mC_s = None
    for d in range(DEPTH):
        x1p, x1s = _ffn_ln_both(xp, xs, ffn1, d, ln_g[d, 0], ln_b[d, 0])
        U = _inproj(x1p, w_in_b, w_in_p, d)
        hc = _gate_consts(gdn_A_log[d], gdn_dt_bias[d], mlstm_i_bias[d], mlstm_f_bias[d])
        wr = _block_diag(lru_wr[d]).astype(bf16)
        wi = _block_diag(lru_wi[d]).astype(bf16)
        lru_w = (lru_conv_w[d], lru_conv_b[d], wr, lru_br[d], wi, lru_bi[d], lru_lambda[d])

        og_p, gS_p = _gdn_prompt(U, BP, TP, gdn_conv_w[d], hc, gdn_norm_w[d])
        hm_p, mC_p, mn_p, mm_p = _mlstm_prompt(U, BP, TP, hc, mlstm_norm_w[d])
        ol_p, lh_p = _lru_prompt(U, BP, TP, *lru_w)
        outs_p[0].append(gS_p)
        outs_p[1].append(_tail_rows(U, BP, TP, 0, CONV_CH))
        outs_p[2].append(mC_p)
        outs_p[3].append(mn_p.reshape(BP, HEADS, DK))
        outs_p[4].append(mm_p[:, :, 0, 0])
        outs_p[5].append(lh_p.reshape(BP, LRU_W))
        outs_p[6].append(_tail_rows(U, BP, TP, COL_LX, LRU_W))

        Us = _inproj(x1s, w_in_b, w_in_p, d).reshape(BS, TS, D_INP)
        hist = jnp.concatenate([
            state_gdn_conv[d], jnp.zeros((BS, 3, COL_LX - CONV_CH), f32),
            state_lru_conv[d], jnp.zeros((BS, 3, D_INP - COL_LX - LRU_W), f32)], axis=2)
        U8 = jnp.concatenate([jnp.zeros((BS, 1, D_INP), f32), hist, Us], axis=1).reshape(BS * GROUP, D_INP)
        og_s, gS_s = _gdn_sample(U8, state_gdn_S, d, gS_s, gdn_conv_w[d], hc, gdn_norm_w[d])
        hm_s, mC_s, n8, m8 = _mlstm_sample(
            U8, state_mlstm_C, d, mC_s, _state_row(state_mlstm_n[d].reshape(BS, QK)),
            _state_row(jnp.repeat(state_mlstm_m[d], DK, axis=1)), hc, mlstm_norm_w[d])
        ol_s, h8 = _lru_sample(U8, _state_row(state_lru_h[d]), *lru_w)
        toks = lambda a: a.reshape(BS, GROUP, -1)[:, TOK0:].reshape(MS, -1)
        last = lambda a: a.reshape(BS, GROUP, -1)[:, GROUP - 1]
        outs_s[1].append(Us[:, TS - 3:, 0:CONV_CH])
        outs_s[3].append(last(n8).reshape(BS, HEADS, DK))
        outs_s[4].append(last(m8).reshape(BS, HEADS, DK)[:, :, 0])
        outs_s[5].append(last(h8))
        outs_s[6].append(Us[:, TS - 3:, COL_LX:COL_LX + LRU_W])

        x2p = _outproj_ln((og_p, hm_p, ol_p), x1p, w_out_b, d, ln_g[d, 1], ln_b[d, 1])
        x2s = _outproj_ln((toks(og_s), toks(hm_s), toks(ol_s)), x1s, w_out_b, d, ln_g[d, 1], ln_b[d, 1])
        xp, xs = _ffn_ln_both(x2p, x2s, ffn2, d, ln_g[d, 2], ln_b[d, 2])

    y_prompt = xp.reshape(BP, TP, D)
    y_sample = xs.reshape(BS, TS, D)
    stack = lambda o: jnp.stack(o)
    return (y_prompt, y_sample,
            *[stack(o) for o in outs_p],
            gS_s, stack(outs_s[1]), mC_s, *[stack(outs_s[i]) for i in range(3, 7)])
```
